```python
import jax, jax.numpy as jnp
from jax import lax
import numpy as np

D_MODEL = 2048
BATCH = 4
SEQ = 8192
DEPTH = 1

MEM_LEN = 256
HGRN_WIDTH = D_MODEL // 2
HGRN_HEADS = 8
HGRN_KDIM = HGRN_WIDTH // HGRN_HEADS
HGRN_VDIM = HGRN_WIDTH // HGRN_HEADS
CONV_CH = D_MODEL - HGRN_WIDTH
CONV_GROUPS = 8
SHORT_CONV_K = 3
IN_COLS = 4 * HGRN_WIDTH + 3 * CONV_CH
CHUNK = 64
MEM_HEADS = 4
MEM_HEAD_DIM = D_MODEL // MEM_HEADS
D_FF = 5632
FFN_CONV_K = 3
EPS = 1e-6

kernel_name = "hgrn2_shortconv_hybrid_block"


def rmsnorm(x, w):
    xf = x.astype(jnp.float32)
    y = xf * lax.rsqrt(jnp.mean(xf * xf, axis=-1, keepdims=True) + EPS)
    return (y * w.astype(jnp.float32)).astype(x.dtype)


def causal_dwconv(x, w):
    k = w.shape[0]
    s = x.shape[1]
    xp = jnp.pad(x, ((0, 0), (k - 1, 0), (0, 0)))
    y = xp[:, 0:s] * w[0]
    for j in range(1, k):
        y = y + xp[:, j:j + s] * w[j]
    return y


def hgrn2_chunked(q, k, v, logf):
    bb, s, h, kd = q.shape
    vd = v.shape[-1]
    n = s // CHUNK

    def to_chunks(t):
        return t.reshape(bb, n, CHUNK, h, t.shape[-1]).transpose(1, 0, 3, 2, 4)

    qc, kc, vc, gc = to_chunks(q), to_chunks(k), to_chunks(v), to_chunks(logf)
    causal = jnp.tril(jnp.ones((CHUNK, CHUNK), dtype=bool))

    def step(state, inp):
        q_, k_, v_, g_ = inp
        b = jnp.cumsum(g_, axis=2)
        o_inter = jnp.einsum('bhtk,bhkv->bhtv', q_ * jnp.exp(b), state)
        diff = b[:, :, :, None, :] - b[:, :, None, :, :]
        decay = jnp.exp(jnp.where(causal[:, :, None], diff, -jnp.inf))
        scores = jnp.einsum('bhtk,bhtsk,bhsk->bhts', q_, decay, k_)
        o = o_inter + jnp.einsum('bhts,bhsv->bhtv', scores, v_)
        b_last = b[:, :, -1:, :]
        new_state = (jnp.exp(b_last[:, :, 0, :])[..., None] * state
                     + jnp.einsum('bhsk,bhsv->bhkv', k_ * jnp.exp(b_last - b), v_))
        return new_state, o

    s0 = jnp.zeros((bb, h, kd, vd), jnp.float32)
    _, o = lax.scan(step, s0, (qc, kc, vc, gc))
    return o.transpose(1, 0, 3, 2, 4).reshape(bb, s, h, vd)


def hybrid_mixer(h, w_in, lb, hgrn_norm_w, sconv_w, w_out):
    bb, s, _ = h.shape
    proj = h @ w_in
    W, C = HGRN_WIDTH, CONV_CH
    splits = [W, 2 * W, 3 * W, 4 * W, 4 * W + C, 4 * W + 2 * C]
    q, f_pre, i_in, g, cb, cc, ch = jnp.split(proj, splits, axis=-1)

    f = lb + (1.0 - lb) * jax.nn.sigmoid(f_pre.astype(jnp.float32))
    logf = jnp.log(f)
    k = 1.0 - f
    qf = jax.nn.silu(q.astype(jnp.float32))
    heads = lambda t, d: t.reshape(bb, s, HGRN_HEADS, d)
    o = hgrn2_chunked(heads(qf, HGRN_KDIM), heads(k, HGRN_KDIM),
                      heads(i_in.astype(jnp.float32), HGRN_VDIM), heads(logf, HGRN_KDIM))
    o = rmsnorm(o, hgrn_norm_w).reshape(bb, s, W).astype(h.dtype)
    o = o * jax.nn.silu(g)

    y = cb * causal_dwconv(cc * ch, sconv_w)

    return jnp.concatenate([o, y], axis=-1) @ w_out


def memory_cross_attention(h, mem_n, wq, wk, wv, wo):
    bb, s, _ = h.shape
    m = mem_n.shape[1]
    q = (h @ wq).reshape(bb, s, MEM_HEADS, MEM_HEAD_DIM)
    k = (mem_n @ wk).reshape(bb, m, MEM_HEADS, MEM_HEAD_DIM)
    v = (mem_n @ wv).reshape(bb, m, MEM_HEADS, MEM_HEAD_DIM)
    sc = jnp.einsum('bqhd,bmhd->bhqm', q, k).astype(jnp.float32) * (MEM_HEAD_DIM ** -0.5)
    p = jax.nn.softmax(sc, axis=-1).astype(v.dtype)
    o = jnp.einsum('bhqm,bmhd->bqhd', p, v).reshape(bb, s, D_MODEL)
    return o @ wo


def conv_ffn(h, w_gate, w_up, conv_w, conv_b, w_down):
    a = causal_dwconv(h @ w_gate, conv_w) + conv_b
    return (jax.nn.silu(a) * (h @ w_up)) @ w_down


def setup_inputs(seed: int = 0) -> dict:
    key = jax.random.key(seed)
    ks = jax.random.split(key, 24)
    f32 = jnp.float32
    nrm = lambda k, shape, scale: jax.random.normal(k, shape, f32) * scale
    gain = lambda k, shape: 1.0 + 0.02 * jax.random.normal(k, shape, f32)
    L = DEPTH
    return {
        "x": nrm(ks[0], (BATCH, SEQ, D_MODEL), 1.0),
        "mem": nrm(ks[1], (BATCH, MEM_LEN, D_MODEL), 1.0),
        "hgrn_lb": nrm(ks[2], (DEPTH + 1, HGRN_WIDTH), 0.1),
        "norm1_w": gain(ks[3], (L, D_MODEL)),
        "w_in": nrm(ks[4], (L, D_MODEL, IN_COLS), D_MODEL ** -0.5),
        "hgrn_norm_w": gain(ks[5], (L, HGRN_VDIM)),
        "sconv_w": nrm(ks[6], (L, SHORT_CONV_K, CONV_CH), SHORT_CONV_K ** -0.5),
        "w_out": nrm(ks[7], (L, D_MODEL, D_MODEL), D_MODEL ** -0.5),
        "norm2_w": gain(ks[8], (L, D_MODEL)),
        "mem_norm_w": gain(ks[9], (L, D_MODEL)),
        "wq": nrm(ks[10], (L, D_MODEL, D_MODEL), D_MODEL ** -0.5),
        "wk": nrm(ks[11], (L, D_MODEL, D_MODEL), D_MODEL ** -0.5),
        "wv": nrm(ks[12], (L, D_MODEL, D_MODEL), D_MODEL ** -0.5),
        "wo": nrm(ks[13], (L, D_MODEL, D_MODEL), D_MODEL ** -0.5),
        "norm3_w": gain(ks[14], (L, D_MODEL)),
        "w_gate": nrm(ks[15], (L, D_MODEL, D_FF), D_MODEL ** -0.5),
        "w_up": nrm(ks[16], (L, D_MODEL, D_FF), D_MODEL ** -0.5),
        "ffn_conv_w": nrm(ks[17], (L, FFN_CONV_K, D_FF), FFN_CONV_K ** -0.5),
        "ffn_conv_b": nrm(ks[18], (L, D_FF), 0.02),
        "w_down": nrm(ks[19], (L, D_FF, D_MODEL), D_FF ** -0.5),
        "final_norm_w": gain(ks[20], (D_MODEL,)),
    }


def reference(x, mem, hgrn_lb, norm1_w, w_in, hgrn_norm_w, sconv_w, w_out,
              norm2_w, mem_norm_w, wq, wk, wv, wo, norm3_w, w_gate, w_up,
              ffn_conv_w, ffn_conv_b, w_down, final_norm_w):
    lb_table = jnp.cumsum(jax.nn.softmax(hgrn_lb.astype(jnp.float32), axis=0), axis=0)
    for l in range(DEPTH):
        h = rmsnorm(x, norm1_w[l])
        x = x + hybrid_mixer(h, w_in[l], lb_table[l], hgrn_norm_w[l], sconv_w[l], w_out[l])
        h = rmsnorm(x, norm2_w[l])
        mem_n = rmsnorm(mem, mem_norm_w[l])
        x = x + memory_cross_attention(h, mem_n, wq[l], wk[l], wv[l], wo[l])
        h = rmsnorm(x, norm3_w[l])
        x = x + conv_ffn(h, w_gate[l], w_up[l], ffn_conv_w[l], ffn_conv_b[l], w_down[l])
    return rmsnorm(x, final_norm_w)
```

```python
import functools

import jax
import jax.numpy as jnp
from jax import lax
from jax.experimental import pallas as pl
from jax.experimental.pallas import tpu as pltpu

F32 = jnp.float32
BF16 = jnp.bfloat16

EPS = 1e-6
HGRN_HEADS = 8
HGRN_CHUNK = 64
CONV_TAPS = 3
MEM_HEADS = 4
LANES = 128
SUBLANES = 8
MXU_COLS = 256
VMEM_LIMIT_BYTES = 56 * 1024 * 1024

ROW_TILE = 512
HEADS_PER_STEP = 2
ATTN_HEADS_PER_STEP = 2
FFN_TILE = 512


def _rms(x, w):
    ms = jnp.mean(x * x, axis=-1, keepdims=True)
    return x * lax.rsqrt(ms + EPS) * w


def _silu(x):
    return x * jax.nn.sigmoid(x)


def _dot(a, b):
    return jnp.dot(a, b, preferred_element_type=F32)


def _dot_nt(a, b):
    return lax.dot_general(a, b, (((1,), (1,)), ((), ())), preferred_element_type=F32)


def _dot_tn(a, b):
    return lax.dot_general(a, b, (((0,), (0,)), ((), ())), preferred_element_type=F32)


def _shift_rows(cur, prev8, shift):
    rolled = pltpu.roll(cur, shift, axis=0)
    head = pltpu.roll(prev8, shift, axis=0)
    rid = lax.broadcasted_iota(jnp.int32, prev8.shape, 0)
    first = jnp.where(rid < shift, head, rolled[:SUBLANES])
    return jnp.concatenate([first, rolled[SUBLANES:]], axis=0)


def _causal_conv3(cur, prev8, w):
    return (w[0:1] * _shift_rows(cur, prev8, 2)
            + w[1:2] * _shift_rows(cur, prev8, 1)
            + w[2:3] * cur)


def _kv_kernel(mem_ref, nw_ref, w_ref, o_ref):
    h = _rms(mem_ref[...], nw_ref[...]).astype(BF16)
    o_ref[...] = _dot(h, w_ref[...]).astype(o_ref.dtype)


def _kv_proj(mem2d, norm_w, wkv):
    rows, d = mem2d.shape
    tn = 1024
    return pl.pallas_call(
        _kv_kernel,
        grid=(2, d // tn),
        in_specs=[
            pl.BlockSpec((rows, d), lambda s, n: (0, 0)),
            pl.BlockSpec((1, d), lambda s, n: (0, 0)),
            pl.BlockSpec((None, d, tn), lambda s, n: (s, 0, n)),
        ],
        out_specs=pl.BlockSpec((None, rows, tn), lambda s, n: (s, 0, n)),
        out_shape=jax.ShapeDtypeStruct((2, rows, d), BF16),
        compiler_params=pltpu.CompilerParams(
            dimension_semantics=("arbitrary", "arbitrary"),
            vmem_limit_bytes=VMEM_LIMIT_BYTES),
        name="kv_proj",
    )(mem2d, norm_w, wkv)


def _mixer_kernel(x_ref, nw_ref, win_ref, wout_ref, lb_ref, hnw_ref, scw_ref, o_ref,
                  h_scr, tri_scr, state_scr, carry_scr, *, tiles_per_seq, layer):
    i = pl.program_id(0)
    j = pl.program_id(1)
    tm = x_ref.shape[0]
    gw = HEADS_PER_STEP * LANES
    hd = LANES
    n_chunks = tm // HGRN_CHUNK

    @pl.when(j == 0)
    def _():
        xv = x_ref[...]
        h_scr[...] = _rms(xv, nw_ref[...]).astype(BF16)
        o_ref[...] = xv

    @pl.when((i == 0) & (j == 0))
    def _():
        r = lax.broadcasted_iota(jnp.int32, (tm, tm), 0)
        c = lax.broadcasted_iota(jnp.int32, (tm, tm), 1)
        same = (r // HGRN_CHUNK) == (c // HGRN_CHUNK)
        tri_scr[...] = jnp.where(same & (c <= r), 1.0, 0.0).astype(BF16)

    @pl.when(i % tiles_per_seq == 0)
    def _():
        state_scr[j] = jnp.zeros(state_scr.shape[1:], F32)
        carry_scr[j] = jnp.zeros(carry_scr.shape[1:], F32)

    p = _dot(h_scr[...], win_ref[...])
    q, f_pre, v, g, cb, cc, ch = (p[:, k * gw:(k + 1) * gw] for k in range(7))

    lb = jnp.sum(jax.nn.softmax(lb_ref[...], axis=0)[0:layer + 1], axis=0, keepdims=True)
    f = lb + (1.0 - lb) * jax.nn.sigmoid(f_pre)
    logf = jnp.log(f)
    kk = 1.0 - f
    qf = _silu(q)
    hi = logf.astype(BF16)
    lo = (logf - hi.astype(F32)).astype(BF16)
    tri = tri_scr[...]
    b = _dot(tri, hi) + _dot(tri, lo)
    vb = v.astype(BF16)

    t_id = lax.broadcasted_iota(jnp.int32, (HGRN_CHUNK, HGRN_CHUNK), 0)
    s_id = lax.broadcasted_iota(jnp.int32, (HGRN_CHUNK, HGRN_CHUNK), 1)
    causal = s_id <= t_id

    states = [state_scr[j, hh] for hh in range(HEADS_PER_STEP)]
    outs = [[] for _ in range(HEADS_PER_STEP)]
    for c in range(n_chunks):
        rows = slice(c * HGRN_CHUNK, (c + 1) * HGRN_CHUNK)
        b_c = b[rows]
        b_last = b_c[HGRN_CHUNK - 1:HGRN_CHUNK]
        qt = (qf[rows] * jnp.exp(b_c)).astype(BF16)
        kt = (kk[rows] * jnp.exp(-b_c)).astype(BF16)
        kd = (kk[rows] * jnp.exp(b_last - b_c)).astype(BF16)
        decay = jnp.exp(b_last)
        v_c = vb[rows]
        for hh in range(HEADS_PER_STEP):
            cols = slice(hh * hd, (hh + 1) * hd)
            st = states[hh]
            o_inter = _dot_nt(qt[:, cols], st.astype(BF16))
            sc = jnp.where(causal, _dot_nt(qt[:, cols], kt[:, cols]), 0.0)
            outs[hh].append(o_inter + _dot(sc.astype(BF16), v_c[:, cols]))
            states[hh] = st * decay[:, cols] + _dot_tn(v_c[:, cols], kd[:, cols])
    for hh in range(HEADS_PER_STEP):
        state_scr[j, hh] = states[hh]

    hnw = hnw_ref[...]
    o_heads = [_rms(jnp.concatenate(outs[hh], axis=0), hnw) for hh in range(HEADS_PER_STEP)]
    o_gated = jnp.concatenate(o_heads, axis=1) * _silu(g)

    u = cc * ch
    prev8 = carry_scr[j]
    carry_scr[j] = u[tm - SUBLANES:]
    y = cb * _causal_conv3(u, prev8, scw_ref[...])

    mixed = jnp.concatenate([o_gated, y], axis=1).astype(BF16)
    o_ref[...] += _dot(mixed, wout_ref[...])


def _mixer(x2d, norm_w, w_in_p, w_out_p, hgrn_lb, hgrn_norm_w, sconv_w, *, seq_len, layer):
    m, d = x2d.shape
    tm = ROW_TILE
    nj, _, pw = w_in_p.shape
    gw = HEADS_PER_STEP * LANES
    kern = functools.partial(_mixer_kernel, tiles_per_seq=seq_len // tm, layer=layer)
    return pl.pallas_call(
        kern,
        grid=(m // tm, nj),
        in_specs=[
            pl.BlockSpec((tm, d), lambda i, j: (i, 0)),
            pl.BlockSpec((1, d), lambda i, j: (0, 0)),
            pl.BlockSpec((None, d, pw), lambda i, j: (j, 0, 0)),
            pl.BlockSpec((None, 2 * gw, d), lambda i, j: (j, 0, 0)),
            pl.BlockSpec((hgrn_lb.shape[0], gw), lambda i, j: (0, j)),
            pl.BlockSpec((1, LANES), lambda i, j: (0, 0)),
            pl.BlockSpec((CONV_TAPS, gw), lambda i, j: (0, j)),
        ],
        out_specs=pl.BlockSpec((tm, d), lambda i, j: (i, 0)),
        out_shape=jax.ShapeDtypeStruct((m, d), F32),
        scratch_shapes=[
            pltpu.VMEM((tm, d), BF16),
            pltpu.VMEM((tm, tm), BF16),
            pltpu.VMEM((nj, HEADS_PER_STEP, LANES, LANES), F32),
            pltpu.VMEM((nj, SUBLANES, gw), F32),
        ],
        compiler_params=pltpu.CompilerParams(
            dimension_semantics=("arbitrary", "arbitrary"),
            vmem_limit_bytes=VMEM_LIMIT_BYTES),
        name="hybrid_mixer",
    )(x2d, norm_w, w_in_p, w_out_p, hgrn_lb, hgrn_norm_w, sconv_w)


def _attn_kernel(x_ref, nw_ref, wq_ref, k_ref, v_ref, wo_ref, o_ref, h_scr, *, head_dim):
    j = pl.program_id(1)

    @pl.when(j == 0)
    def _():
        xv = x_ref[...]
        h_scr[...] = _rms(xv, nw_ref[...]).astype(BF16)
        o_ref[...] = xv

    q = _dot(h_scr[...], wq_ref[...])
    scale = head_dim ** -0.5
    heads = []
    for hh in range(ATTN_HEADS_PER_STEP):
        cols = slice(hh * head_dim, (hh + 1) * head_dim)
        s = _dot_nt(q[:, cols].astype(BF16), k_ref[:, cols]) * scale
        s = s - jnp.max(s, axis=-1, keepdims=True)
        e = jnp.exp(s)
        pr = e / jnp.sum(e, axis=-1, keepdims=True)
        heads.append(_dot(pr.astype(BF16), v_ref[:, cols]))
    o = jnp.concatenate(heads, axis=1).astype(BF16)
    o_ref[...] += _dot(o, wo_ref[...])


def _attn(x2d, norm_w, wq, kv, wo, *, seq_len, mem_len):
    m, d = x2d.shape
    tm = ROW_TILE
    head_dim = d // MEM_HEADS
    tn = ATTN_HEADS_PER_STEP * head_dim
    tiles_per_seq = seq_len // tm
    kern = functools.partial(_attn_kernel, head_dim=head_dim)
    return pl.pallas_call(
        kern,
        grid=(m // tm, d // tn),
        in_specs=[
            pl.BlockSpec((tm, d), lambda i, j: (i, 0)),
            pl.BlockSpec((1, d), lambda i, j: (0, 0)),
            pl.BlockSpec((d, tn), lambda i, j: (0, j)),
            pl.BlockSpec((None, None, mem_len, tn), lambda i, j: (0, i // tiles_per_seq, 0, j)),
            pl.BlockSpec((None, None, mem_len, tn), lambda i, j: (1, i // tiles_per_seq, 0, j)),
            pl.BlockSpec((tn, d), lambda i, j: (j, 0)),
        ],
        out_specs=pl.BlockSpec((tm, d), lambda i, j: (i, 0)),
        out_shape=jax.ShapeDtypeStruct((m, d), F32),
        scratch_shapes=[pltpu.VMEM((tm, d), BF16)],
        compiler_params=pltpu.CompilerParams(
            dimension_semantics=("arbitrary", "arbitrary"),
            vmem_limit_bytes=VMEM_LIMIT_BYTES),
        name="mem_cross_attn",
    )(x2d, norm_w, wq, kv, kv, wo)


def _ffn_kernel(x_ref, nw_ref, wg_ref, wu_ref, cw_ref, cb_ref, wd_ref, fnw_ref, o_ref,
                h_scr, carry_scr, *, tiles_per_seq, final_norm):
    i = pl.program_id(0)
    f = pl.program_id(1)
    tm = x_ref.shape[0]

    @pl.when(f == 0)
    def _():
        xv = x_ref[...]
        h_scr[...] = _rms(xv, nw_ref[...]).astype(BF16)
        o_ref[...] = xv

    @pl.when(i % tiles_per_seq == 0)
    def _():
        carry_scr[f] = jnp.zeros(carry_scr.shape[1:], F32)

    h = h_scr[...]
    a = _dot(h, wg_ref[...])
    up = _dot(h, wu_ref[...])
    prev8 = carry_scr[f]
    carry_scr[f] = a[tm - SUBLANES:]
    act = _silu(_causal_conv3(a, prev8, cw_ref[...]) + cb_ref[...]) * up
    o_ref[...] += _dot(act.astype(BF16), wd_ref[...])

    if final_norm:
        @pl.when(f == pl.num_programs(1) - 1)
        def _():
            o_ref[...] = _rms(o_ref[...], fnw_ref[...])


def _ffn(x2d, norm_w, w_gate, w_up, conv_w, conv_b, w_down, final_w, *, seq_len, final_norm):
    m, d = x2d.shape
    d_ff = w_gate.shape[1]
    tm, tf = ROW_TILE, FFN_TILE
    nf = d_ff // tf
    kern = functools.partial(_ffn_kernel, tiles_per_seq=seq_len // tm, final_norm=final_norm)
    return pl.pallas_call(
        kern,
        grid=(m // tm, nf),
        in_specs=[
            pl.BlockSpec((tm, d), lambda i, f: (i, 0)),
            pl.BlockSpec((1, d), lambda i, f: (0, 0)),
            pl.BlockSpec((d, tf), lambda i, f: (0, f)),
            pl.BlockSpec((d, tf), lambda i, f: (0, f)),
            pl.BlockSpec((CONV_TAPS, tf), lambda i, f: (0, f)),
            pl.BlockSpec((1, tf), lambda i, f: (0, f)),
            pl.BlockSpec((tf, d), lambda i, f: (f, 0)),
            pl.BlockSpec((1, d), lambda i, f: (0, 0)),
        ],
        out_specs=pl.BlockSpec((tm, d), lambda i, f: (i, 0)),
        out_shape=jax.ShapeDtypeStruct((m, d), F32),
        scratch_shapes=[
            pltpu.VMEM((tm, d), BF16),
            pltpu.VMEM((nf, SUBLANES, tf), F32),
        ],
        compiler_params=pltpu.CompilerParams(
            dimension_semantics=("arbitrary", "arbitrary"),
            vmem_limit_bytes=VMEM_LIMIT_BYTES),
        name="conv_ffn",
    )(x2d, norm_w, w_gate, w_up, conv_w, conv_b, w_down, final_w)


def _permute_mixer_weights(w_in, w_out):
    d, in_cols = w_in.shape
    width = in_cols // 7
    gw = HEADS_PER_STEP * LANES
    nj = width // gw
    w_in_p = w_in.reshape(d, 7, nj, gw).transpose(2, 0, 1, 3).reshape(nj, d, 7 * gw)
    w_out_p = w_out.reshape(2, nj, gw, w_out.shape[1]).transpose(1, 0, 2, 3)
    w_out_p = w_out_p.reshape(nj, 2 * gw, w_out.shape[1])
    return w_in_p.astype(BF16), w_out_p.astype(BF16)


def kernel(x, mem, hgrn_lb, norm1_w, w_in, hgrn_norm_w, sconv_w, w_out, norm2_w, mem_norm_w,
           wq, wk, wv, wo, norm3_w, w_gate, w_up, ffn_conv_w, ffn_conv_b, w_down, final_norm_w):
    batch, seq_len, d = x.shape
    mem_len = mem.shape[1]
    depth = norm1_w.shape[0]
    assert seq_len % ROW_TILE == 0 and ROW_TILE % HGRN_CHUNK == 0
    assert w_in.shape[2] == 7 * HGRN_HEADS * LANES and w_gate.shape[2] % FFN_TILE == 0

    xs = x.reshape(batch * seq_len, d)
    mem2d = mem.reshape(batch * mem_len, d)
    row = lambda w: w.reshape(1, -1)
    for l in range(depth):
        w_in_p, w_out_p = _permute_mixer_weights(w_in[l], w_out[l])
        xs = _mixer(xs, row(norm1_w[l]), w_in_p, w_out_p, hgrn_lb,
                    row(hgrn_norm_w[l]), sconv_w[l], seq_len=seq_len, layer=l)
        kv = _kv_proj(mem2d, row(mem_norm_w[l]), jnp.stack([wk[l], wv[l]]).astype(BF16))
        kv = kv.reshape(2, batch, mem_len, d)
        xs = _attn(xs, row(norm2_w[l]), wq[l].astype(BF16), kv, wo[l].astype(BF16),
                   seq_len=seq_len, mem_len=mem_len)
        xs = _ffn(xs, row(norm3_w[l]), w_gate[l].astype(BF16), w_up[l].astype(BF16),
                  ffn_conv_w[l], row(ffn_conv_b[l]), w_down[l].astype(BF16),
                  row(final_norm_w), seq_len=seq_len, final_norm=(l == depth - 1))
    return xs.reshape(batch, seq_len, d)
```

```python
import functools

import jax
import jax.numpy as jnp
from jax import lax
from jax.experimental import pallas as pl
from jax.experimental.pallas import tpu as pltpu

F32 = jnp.float32
BF16 = jnp.bfloat16

EPS = 1e-6
HGRN_HEADS = 8
HGRN_CHUNK = 64
CONV_TAPS = 3
MEM_HEADS = 4
N_PROJ_GROUPS = 7
LANES = 128
SUBLANES = 8
BF16_ROWS = 16
VMEM_LIMIT_BYTES = 56 * 1024 * 1024

ROW_TILE = 512
HEADS_PER_STEP = 2
ATTN_HEADS_PER_STEP = 2
FFN_TILE = 512


def _rms(x, w):
    ms = jnp.mean(x * x, axis=-1, keepdims=True)
    return x * lax.rsqrt(ms + EPS) * w


def _silu(x):
    return x * jax.nn.sigmoid(x)


def _dot(a, b):
    return jnp.dot(a, b, preferred_element_type=F32)


def _dot_nt(a, b):
    return lax.dot_general(a, b, (((1,), (1,)), ((), ())), preferred_element_type=F32)


def _dot_tn(a, b):
    return lax.dot_general(a, b, (((0,), (0,)), ((), ())), preferred_element_type=F32)


def _shift_rows(cur, prev8, shift):
    rolled = pltpu.roll(cur, shift, axis=0)
    head = pltpu.roll(prev8, shift, axis=0)
    rid = lax.broadcasted_iota(jnp.int32, prev8.shape, 0)
    first = jnp.where(rid < shift, head, rolled[:SUBLANES])
    return jnp.concatenate([first, rolled[SUBLANES:]], axis=0)


def _causal_conv3(cur, prev8, w):
    return (w[0:1] * _shift_rows(cur, prev8, 2)
            + w[1:2] * _shift_rows(cur, prev8, 1)
            + w[2:3] * cur)


def _normalised_tile(x_ref, nw_ref, h_scr, tile, step):
    @pl.when((tile == 0) & (step == 0))
    def _():
        h_scr[0] = _rms(x_ref[...], nw_ref[...]).astype(BF16)

    return h_scr[tile % 2]


def _normalise_next_slice(xn_ref, nw_ref, h_scr, tile, step, n_steps):
    tm = xn_ref.shape[0]
    rows = pl.cdiv(pl.cdiv(tm, n_steps), BF16_ROWS) * BF16_ROWS
    r0 = pl.multiple_of(jnp.minimum(step * rows, tm - rows), BF16_ROWS)
    h_scr[1 - tile % 2, pl.ds(r0, rows), :] = (
        _rms(xn_ref[pl.ds(r0, rows), :], nw_ref[...]).astype(BF16))


def _row_tile_specs(tm, d, n_tiles):
    cur = pl.BlockSpec((tm, d), lambda i, j: (i, 0))
    nxt = pl.BlockSpec((tm, d), lambda i, j: (jnp.minimum(i + 1, n_tiles - 1), 0))
    return cur, nxt


def _kv_kernel(mem_ref, nw_ref, w_ref, o_ref):
    h = _rms(mem_ref[...], nw_ref[...]).astype(BF16)
    o_ref[...] = _dot(h, w_ref[...]).astype(o_ref.dtype)


def _kv_proj(mem2d, norm_w, wkv):
    rows, d = mem2d.shape
    tn = 1024
    return pl.pallas_call(
        _kv_kernel,
        grid=(2, d // tn),
        in_specs=[
            pl.BlockSpec((rows, d), lambda s, n: (0, 0)),
            pl.BlockSpec((1, d), lambda s, n: (0, 0)),
            pl.BlockSpec((None, d, tn), lambda s, n: (s, 0, n)),
        ],
        out_specs=pl.BlockSpec((None, rows, tn), lambda s, n: (s, 0, n)),
        out_shape=jax.ShapeDtypeStruct((2, rows, d), BF16),
        compiler_params=pltpu.CompilerParams(
            dimension_semantics=("arbitrary", "arbitrary"),
            vmem_limit_bytes=VMEM_LIMIT_BYTES),
        name="kv_proj",
    )(mem2d, norm_w, wkv)


def _block_diag2(a, b):
    z = jnp.zeros(a.shape, a.dtype)
    return jnp.concatenate([jnp.concatenate([a, z], axis=1),
                            jnp.concatenate([z, b], axis=1)], axis=0)


def _mixer_kernel(x_ref, xn_ref, nw_ref, wq_ref, wf_ref, wi_ref, wg_ref, wb_ref, wc_ref, wh_ref,
                  wo_hgrn_ref, wo_conv_ref, lb_ref, hnw_ref, scw_ref, o_ref,
                  h_scr, tri_scr, state_scr, carry_scr, *, tiles_per_seq, n_steps, layer):
    i = pl.program_id(0)
    j = pl.program_id(1)
    tm = x_ref.shape[0]
    hd = LANES
    n_chunks = tm // HGRN_CHUNK

    @pl.when(j == 0)
    def _():
        o_ref[...] = x_ref[...]

    @pl.when((i == 0) & (j == 0))
    def _():
        r = lax.broadcasted_iota(jnp.int32, (tm, tm), 0)
        c = lax.broadcasted_iota(jnp.int32, (tm, tm), 1)
        same = (r // HGRN_CHUNK) == (c // HGRN_CHUNK)
        tri_scr[...] = jnp.where(same & (c <= r), 1.0, 0.0).astype(BF16)

    @pl.when(i % tiles_per_seq == 0)
    def _():
        state_scr[j] = jnp.zeros(state_scr.shape[1:], F32)
        carry_scr[j] = jnp.zeros(carry_scr.shape[1:], F32)

    h = _normalised_tile(x_ref, nw_ref, h_scr, i, j)
    q, f_pre, v, g, cb, cc, ch = (_dot(h, w[...]) for w in
                                  (wq_ref, wf_ref, wi_ref, wg_ref, wb_ref, wc_ref, wh_ref))

    lb = jnp.sum(jax.nn.softmax(lb_ref[...], axis=0)[0:layer + 1], axis=0, keepdims=True)
    f = lb + (1.0 - lb) * jax.nn.sigmoid(f_pre)
    logf = jnp.log(f)
    kk = 1.0 - f
    qf = _silu(q)
    hi = logf.astype(BF16)
    lo = (logf - hi.astype(F32)).astype(BF16)
    tri = tri_scr[...]
    b = _dot(tri, hi) + _dot(tri, lo)
    vb = v.astype(BF16)

    t_id = lax.broadcasted_iota(jnp.int32, (HGRN_CHUNK, 2 * HGRN_CHUNK), 0)
    s_id = lax.broadcasted_iota(jnp.int32, (HGRN_CHUNK, 2 * HGRN_CHUNK), 1) % HGRN_CHUNK
    causal = s_id <= t_id

    st0 = state_scr[j, 0]
    st1 = state_scr[j, 1]
    outs = []
    for c in range(n_chunks):
        rows = slice(c * HGRN_CHUNK, (c + 1) * HGRN_CHUNK)
        b_c = b[rows]
        b_last = b_c[HGRN_CHUNK - 1:HGRN_CHUNK]
        qt = (qf[rows] * jnp.exp(b_c)).astype(BF16)
        kt = (kk[rows] * jnp.exp(-b_c)).astype(BF16)
        kd = (kk[rows] * jnp.exp(b_last - b_c)).astype(BF16)
        decay = jnp.exp(b_last)
        v_c = vb[rows]
        o_inter = _dot_nt(qt, _block_diag2(st0.astype(BF16), st1.astype(BF16)))
        sc = _dot_nt(qt, _block_diag2(kt[:, :hd], kt[:, hd:]))
        sc = jnp.where(causal, sc, 0.0).astype(BF16)
        outs.append(o_inter + _dot(sc, _block_diag2(v_c[:, :hd], v_c[:, hd:])))
        upd = _dot_tn(v_c, kd)
        st0 = st0 * decay[:, :hd] + upd[:hd, :hd]
        st1 = st1 * decay[:, hd:] + upd[hd:, hd:]
    state_scr[j, 0] = st0
    state_scr[j, 1] = st1

    hnw = hnw_ref[...]
    o = jnp.concatenate(outs, axis=0)
    o = jnp.concatenate([_rms(o[:, :hd], hnw), _rms(o[:, hd:], hnw)], axis=1)
    o_gated = (o * _silu(g)).astype(BF16)

    u = cc * ch
    prev8 = carry_scr[j]
    carry_scr[j] = u[tm - SUBLANES:]
    y = (cb * _causal_conv3(u, prev8, scw_ref[...])).astype(BF16)

    o_ref[...] += _dot(o_gated, wo_hgrn_ref[...]) + _dot(y, wo_conv_ref[...])
    _normalise_next_slice(xn_ref, nw_ref, h_scr, i, j, n_steps)


def _mixer(x2d, norm_w, w_in, w_out, hgrn_lb, hgrn_norm_w, sconv_w, *, seq_len, layer):
    m, d = x2d.shape
    tm = ROW_TILE
    gw = HEADS_PER_STEP * LANES
    nj = w_in.shape[1] // (N_PROJ_GROUPS * gw)
    n_tiles = m // tm
    x_cur, x_nxt = _row_tile_specs(tm, d, n_tiles)
    w_in_specs = [pl.BlockSpec((d, gw), lambda i, j, g=g: (0, g * nj + j))
                  for g in range(N_PROJ_GROUPS)]
    kern = functools.partial(_mixer_kernel, tiles_per_seq=seq_len // tm, n_steps=nj, layer=layer)
    return pl.pallas_call(
        kern,
        grid=(n_tiles, nj),
        in_specs=[
            x_cur, x_nxt,
            pl.BlockSpec((1, d), lambda i, j: (0, 0)),
            *w_in_specs,
            pl.BlockSpec((gw, d), lambda i, j: (j, 0)),
            pl.BlockSpec((gw, d), lambda i, j: (nj + j, 0)),
            pl.BlockSpec((hgrn_lb.shape[0], gw), lambda i, j: (0, j)),
            pl.BlockSpec((1, LANES), lambda i, j: (0, 0)),
            pl.BlockSpec((CONV_TAPS, gw), lambda i, j: (0, j)),
        ],
        out_specs=pl.BlockSpec((tm, d), lambda i, j: (i, 0)),
        out_shape=jax.ShapeDtypeStruct((m, d), F32),
        scratch_shapes=[
            pltpu.VMEM((2, tm, d), BF16),
            pltpu.VMEM((tm, tm), BF16),
            pltpu.VMEM((nj, HEADS_PER_STEP, LANES, LANES), F32),
            pltpu.VMEM((nj, SUBLANES, gw), F32),
        ],
        compiler_params=pltpu.CompilerParams(
            dimension_semantics=("arbitrary", "arbitrary"),
            vmem_limit_bytes=VMEM_LIMIT_BYTES),
        name="hybrid_mixer",
    )(x2d, x2d, norm_w, *([w_in] * N_PROJ_GROUPS), w_out, w_out, hgrn_lb, hgrn_norm_w, sconv_w)


def _attn_kernel(x_ref, xn_ref, nw_ref, wq_ref, k_ref, v_ref, wo_ref, o_ref, h_scr, *,
                 head_dim, n_steps):
    i = pl.program_id(0)
    j = pl.program_id(1)

    @pl.when(j == 0)
    def _():
        o_ref[...] = x_ref[...]

    h = _normalised_tile(x_ref, nw_ref, h_scr, i, j)
    q = _dot(h, wq_ref[...])
    scale = head_dim ** -0.5
    heads = []
    for hh in range(ATTN_HEADS_PER_STEP):
        cols = slice(hh * head_dim, (hh + 1) * head_dim)
        s = _dot_nt(q[:, cols].astype(BF16), k_ref[:, cols]) * scale
        s = s - jnp.max(s, axis=-1, keepdims=True)
        e = jnp.exp(s)
        pr = e / jnp.sum(e, axis=-1, keepdims=True)
        heads.append(_dot(pr.astype(BF16), v_ref[:, cols]))
    o = jnp.concatenate(heads, axis=1).astype(BF16)
    o_ref[...] += _dot(o, wo_ref[...])
    _normalise_next_slice(xn_ref, nw_ref, h_scr, i, j, n_steps)


def _attn(x2d, norm_w, wq, kv, wo, *, seq_len, mem_len):
    m, d = x2d.shape
    tm = ROW_TILE
    head_dim = d // MEM_HEADS
    tn = ATTN_HEADS_PER_STEP * head_dim
    tiles_per_seq = seq_len // tm
    x_cur, x_nxt = _row_tile_specs(tm, d, m // tm)
    kern = functools.partial(_attn_kernel, head_dim=head_dim, n_steps=d // tn)
    return pl.pallas_call(
        kern,
        grid=(m // tm, d // tn),
        in_specs=[
            x_cur, x_nxt,
            pl.BlockSpec((1, d), lambda i, j: (0, 0)),
            pl.BlockSpec((d, tn), lambda i, j: (0, j)),
            pl.BlockSpec((None, None, mem_len, tn), lambda i, j: (0, i // tiles_per_seq, 0, j)),
            pl.BlockSpec((None, None, mem_len, tn), lambda i, j: (1, i // tiles_per_seq, 0, j)),
            pl.BlockSpec((tn, d), lambda i, j: (j, 0)),
        ],
        out_specs=pl.BlockSpec((tm, d), lambda i, j: (i, 0)),
        out_shape=jax.ShapeDtypeStruct((m, d), F32),
        scratch_shapes=[pltpu.VMEM((2, tm, d), BF16)],
        compiler_params=pltpu.CompilerParams(
            dimension_semantics=("arbitrary", "arbitrary"),
            vmem_limit_bytes=VMEM_LIMIT_BYTES),
        name="mem_cross_attn",
    )(x2d, x2d, norm_w, wq, kv, kv, wo)


def _ffn_kernel(x_ref, xn_ref, nw_ref, wg_ref, wu_ref, cw_ref, cb_ref, wd_ref, fnw_ref, o_ref,
                h_scr, carry_scr, *, tiles_per_seq, n_steps, final_norm):
    i = pl.program_id(0)
    f = pl.program_id(1)
    tm = x_ref.shape[0]

    @pl.when(f == 0)
    def _():
        o_ref[...] = x_ref[...]

    @pl.when(i % tiles_per_seq == 0)
    def _():
        carry_scr[f] = jnp.zeros(carry_scr.shape[1:], F32)

    h = _normalised_tile(x_ref, nw_ref, h_scr, i, f)
    a = _dot(h, wg_ref[...])
    up = _dot(h, wu_ref[...])
    prev8 = carry_scr[f]
    carry_scr[f] = a[tm - SUBLANES:]
    act = _silu(_causal_conv3(a, prev8, cw_ref[...]) + cb_ref[...]) * up
    o_ref[...] += _dot(act.astype(BF16), wd_ref[...])
    _normalise_next_slice(xn_ref, nw_ref, h_scr, i, f, n_steps)

    if final_norm:
        @pl.when(f == n_steps - 1)
        def _():
            o_ref[...] = _rms(o_ref[...], fnw_ref[...])


def _ffn(x2d, norm_w, w_gate, w_up, conv_w, conv_b, w_down, final_w, *, seq_len, final_norm):
    m, d = x2d.shape
    d_ff = w_gate.shape[1]
    tm, tf = ROW_TILE, FFN_TILE
    nf = d_ff // tf
    x_cur, x_nxt = _row_tile_specs(tm, d, m // tm)
    kern = functools.partial(_ffn_kernel, tiles_per_seq=seq_len // tm, n_steps=nf,
                             final_norm=final_norm)
    return pl.pallas_call(
        kern,
        grid=(m // tm, nf),
        in_specs=[
            x_cur, x_nxt,
            pl.BlockSpec((1, d), lambda i, f: (0, 0)),
            pl.BlockSpec((d, tf), lambda i, f: (0, f)),
            pl.BlockSpec((d, tf), lambda i, f: (0, f)),
            pl.BlockSpec((CONV_TAPS, tf), lambda i, f: (0, f)),
            pl.BlockSpec((1, tf), lambda i, f: (0, f)),
            pl.BlockSpec((tf, d), lambda i, f: (f, 0)),
            pl.BlockSpec((1, d), lambda i, f: (0, 0)),
        ],
        out_specs=pl.BlockSpec((tm, d), lambda i, f: (i, 0)),
        out_shape=jax.ShapeDtypeStruct((m, d), F32),
        scratch_shapes=[
            pltpu.VMEM((2, tm, d), BF16),
            pltpu.VMEM((nf, SUBLANES, tf), F32),
        ],
        compiler_params=pltpu.CompilerParams(
            dimension_semantics=("arbitrary", "arbitrary"),
            vmem_limit_bytes=VMEM_LIMIT_BYTES),
        name="conv_ffn",
    )(x2d, x2d, norm_w, w_gate, w_up, conv_w, conv_b, w_down, final_w)


def kernel(x, mem, hgrn_lb, norm1_w, w_in, hgrn_norm_w, sconv_w, w_out, norm2_w, mem_norm_w,
           wq, wk, wv, wo, norm3_w, w_gate, w_up, ffn_conv_w, ffn_conv_b, w_down, final_norm_w):
    batch, seq_len, d = x.shape
    mem_len = mem.shape[1]
    depth = norm1_w.shape[0]
    assert seq_len % ROW_TILE == 0 and ROW_TILE % HGRN_CHUNK == 0
    assert HEADS_PER_STEP == 2 and HGRN_HEADS % HEADS_PER_STEP == 0
    assert w_in.shape[2] == N_PROJ_GROUPS * HGRN_HEADS * LANES and w_gate.shape[2] % FFN_TILE == 0

    xs = x.reshape(batch * seq_len, d)
    mem2d = mem.reshape(batch * mem_len, d)
    row = lambda w: w.reshape(1, -1)
    for l in range(depth):
        xs = _mixer(xs, row(norm1_w[l]), w_in[l].astype(BF16), w_out[l].astype(BF16), hgrn_lb,
                    row(hgrn_norm_w[l]), sconv_w[l], seq_len=seq_len, layer=l)
        kv = _kv_proj(mem2d, row(mem_norm_w[l]), jnp.stack([wk[l], wv[l]]).astype(BF16))
        kv = kv.reshape(2, batch, mem_len, d)
        xs = _attn(xs, row(norm2_w[l]), wq[l].astype(BF16), kv, wo[l].astype(BF16),
                   seq_len=seq_len, mem_len=mem_len)
        xs = _ffn(xs, row(norm3_w[l]), w_gate[l].astype(BF16), w_up[l].astype(BF16),
                  ffn_conv_w[l], row(ffn_conv_b[l]), w_down[l].astype(BF16),
                  row(final_norm_w), seq_len=seq_len, final_norm=(l == depth - 1))
    return xs.reshape(batch, seq_len, d)
```

```python
import functools

import jax
import jax.numpy as jnp
from jax import lax
from jax.experimental import pallas as pl
from jax.experimental.pallas import tpu as pltpu

F32 = jnp.float32
BF16 = jnp.bfloat16

EPS = 1e-6
HGRN_HEADS = 8
HGRN_CHUNK = 64
HGRN_BLOCK = 256
CONV_TAPS = 3
MEM_HEADS = 4
N_PROJ_GROUPS = 7
LANES = 128
SUBLANES = 8
BF16_ROWS = 16
VMEM_LIMIT_BYTES = 56 * 1024 * 1024
FFN_VMEM_LIMIT_BYTES = 60 * 1024 * 1024

ROW_TILE = 512
FFN_ROW_TILE = 1024
HEADS_PER_STEP = 2
ATTN_HEADS_PER_STEP = 2
FFN_TILE = 512


def _rms(x, w):
    ms = jnp.mean(x * x, axis=-1, keepdims=True)
    return x * lax.rsqrt(ms + EPS) * w


def _silu(x):
    return x * jax.nn.sigmoid(x)


def _dot(a, b):
    return jnp.dot(a, b, preferred_element_type=F32)


def _dot_nt(a, b):
    return lax.dot_general(a, b, (((1,), (1,)), ((), ())), preferred_element_type=F32)


def _dot_tn(a, b):
    return lax.dot_general(a, b, (((0,), (0,)), ((), ())), preferred_element_type=F32)


def _shift_rows(cur, prev8, shift):
    rolled = pltpu.roll(cur, shift, axis=0)
    head = pltpu.roll(prev8, shift, axis=0)
    rid = lax.broadcasted_iota(jnp.int32, prev8.shape, 0)
    first = jnp.where(rid < shift, head, rolled[:SUBLANES])
    return jnp.concatenate([first, rolled[SUBLANES:]], axis=0)


def _causal_conv3(cur, prev8, w):
    return (w[0:1] * _shift_rows(cur, prev8, 2)
            + w[1:2] * _shift_rows(cur, prev8, 1)
            + w[2:3] * cur)


def _normalised_tile(x_ref, nw_ref, h_scr, tile, step):
    @pl.when((tile == 0) & (step == 0))
    def _():
        h_scr[0] = _rms(x_ref[...], nw_ref[...]).astype(BF16)

    return h_scr[tile % 2]


def _normalise_next_slice(xn_ref, nw_ref, h_scr, tile, step, n_steps):
    tm = xn_ref.shape[0]
    rows = pl.cdiv(pl.cdiv(tm, n_steps), BF16_ROWS) * BF16_ROWS
    r0 = pl.multiple_of(jnp.minimum(step * rows, tm - rows), BF16_ROWS)
    h_scr[1 - tile % 2, pl.ds(r0, rows), :] = (
        _rms(xn_ref[pl.ds(r0, rows), :], nw_ref[...]).astype(BF16))


def _row_tile_specs(tm, d, n_tiles):
    cur = pl.BlockSpec((tm, d), lambda i, j: (i, 0))
    nxt = pl.BlockSpec((tm, d), lambda i, j: (jnp.minimum(i + 1, n_tiles - 1), 0))
    return cur, nxt


def _kv_kernel(mem_ref, nw_ref, w_ref, o_ref):
    h = _rms(mem_ref[...], nw_ref[...]).astype(BF16)
    o_ref[...] = _dot(h, w_ref[...]).astype(o_ref.dtype)


def _kv_proj(mem2d, norm_w, wkv):
    rows, d = mem2d.shape
    tn = 1024
    return pl.pallas_call(
        _kv_kernel,
        grid=(2, d // tn),
        in_specs=[
            pl.BlockSpec((rows, d), lambda s, n: (0, 0)),
            pl.BlockSpec((1, d), lambda s, n: (0, 0)),
            pl.BlockSpec((None, d, tn), lambda s, n: (s, 0, n)),
        ],
        out_specs=pl.BlockSpec((None, rows, tn), lambda s, n: (s, 0, n)),
        out_shape=jax.ShapeDtypeStruct((2, rows, d), BF16),
        compiler_params=pltpu.CompilerParams(
            dimension_semantics=("arbitrary", "arbitrary"),
            vmem_limit_bytes=VMEM_LIMIT_BYTES),
        name="kv_proj",
    )(mem2d, norm_w, wkv)


def _block_diag2(a, b):
    z = jnp.zeros(a.shape, a.dtype)
    return jnp.concatenate([jnp.concatenate([a, z], axis=1),
                            jnp.concatenate([z, b], axis=1)], axis=0)


def _mixer_kernel(x_ref, xn_ref, nw_ref, wq_ref, wf_ref, wi_ref, wg_ref, wb_ref, wc_ref, wh_ref,
                  wo_hgrn_ref, wo_conv_ref, lb_ref, hnw_ref, scw_ref, o_ref,
                  h_scr, tri_scr, state_scr, carry_scr, *, tiles_per_seq, n_steps, layer):
    i = pl.program_id(0)
    j = pl.program_id(1)
    tm = x_ref.shape[0]
    hd = LANES
    n_chunks = tm // HGRN_CHUNK

    @pl.when(j == 0)
    def _():
        o_ref[...] = x_ref[...]

    @pl.when((i == 0) & (j == 0))
    def _():
        r = lax.broadcasted_iota(jnp.int32, (tm, tm), 0)
        c = lax.broadcasted_iota(jnp.int32, (tm, tm), 1)
        same = (r // HGRN_CHUNK) == (c // HGRN_CHUNK)
        tri_scr[...] = jnp.where(same & (c <= r), 1.0, 0.0).astype(BF16)

    @pl.when(i % tiles_per_seq == 0)
    def _():
        state_scr[j] = jnp.zeros(state_scr.shape[1:], F32)
        carry_scr[j] = jnp.zeros(carry_scr.shape[1:], F32)

    h = _normalised_tile(x_ref, nw_ref, h_scr, i, j)
    q, f_pre, v, g, cb, cc, ch = (_dot(h, w[...]) for w in
                                  (wq_ref, wf_ref, wi_ref, wg_ref, wb_ref, wc_ref, wh_ref))

    lb = jnp.sum(jax.nn.softmax(lb_ref[...], axis=0)[0:layer + 1], axis=0, keepdims=True)
    f = lb + (1.0 - lb) * jax.nn.sigmoid(f_pre)
    logf = jnp.log(f)
    kk = 1.0 - f
    qf = _silu(q)
    hi = logf.astype(BF16)
    lo = (logf - hi.astype(F32)).astype(BF16)
    tri = tri_scr[...]
    b = _dot(tri, hi) + _dot(tri, lo)
    vb = v.astype(BF16)
    qt = (qf * jnp.exp(b)).astype(BF16)
    kt = (kk * jnp.exp(-b)).astype(BF16)

    t_id = lax.broadcasted_iota(jnp.int32, (HGRN_BLOCK, HGRN_BLOCK), 0)
    s_id = lax.broadcasted_iota(jnp.int32, (HGRN_BLOCK, HGRN_BLOCK), 1)
    keep = ((t_id // HGRN_CHUNK) == (s_id // HGRN_CHUNK)) & (s_id <= t_id)
    intra = []
    for r0 in range(0, tm, HGRN_BLOCK):
        rows = slice(r0, r0 + HGRN_BLOCK)
        sc = [jnp.where(keep, _dot_nt(qt[rows, cs], kt[rows, cs]), 0.0).astype(BF16)
              for cs in (slice(0, hd), slice(hd, 2 * hd))]
        intra.append(_dot(jnp.concatenate(sc, axis=1),
                          _block_diag2(vb[rows, :hd], vb[rows, hd:])))

    chunks = [slice(c * HGRN_CHUNK, (c + 1) * HGRN_CHUNK) for c in range(n_chunks)]
    b_lasts = [b[rows][HGRN_CHUNK - 1:HGRN_CHUNK] for rows in chunks]
    upds = [_dot_tn(vb[rows], (kk[rows] * jnp.exp(bl - b[rows])).astype(BF16))
            for rows, bl in zip(chunks, b_lasts)]
    st0 = state_scr[j, 0]
    st1 = state_scr[j, 1]
    starts = []
    for upd, bl in zip(upds, b_lasts):
        starts.append(_block_diag2(st0.astype(BF16), st1.astype(BF16)))
        decay = jnp.exp(bl)
        st0 = st0 * decay[:, :hd] + upd[:hd, :hd]
        st1 = st1 * decay[:, hd:] + upd[hd:, hd:]
    state_scr[j, 0] = st0
    state_scr[j, 1] = st1
    inter = [_dot_nt(qt[rows], start) for rows, start in zip(chunks, starts)]

    hnw = hnw_ref[...]
    o = jnp.concatenate(intra, axis=0) + jnp.concatenate(inter, axis=0)
    o = jnp.concatenate([_rms(o[:, :hd], hnw), _rms(o[:, hd:], hnw)], axis=1)
    o_gated = (o * _silu(g)).astype(BF16)

    u = cc * ch
    prev8 = carry_scr[j]
    carry_scr[j] = u[tm - SUBLANES:]
    y = (cb * _causal_conv3(u, prev8, scw_ref[...])).astype(BF16)

    o_ref[...] += _dot(o_gated, wo_hgrn_ref[...]) + _dot(y, wo_conv_ref[...])
    _normalise_next_slice(xn_ref, nw_ref, h_scr, i, j, n_steps)


def _mixer(x2d, norm_w, w_in, w_out, hgrn_lb, hgrn_norm_w, sconv_w, *, seq_len, layer):
    m, d = x2d.shape
    tm = ROW_TILE
    gw = HEADS_PER_STEP * LANES
    nj = w_in.shape[1] // (N_PROJ_GROUPS * gw)
    n_tiles = m // tm
    x_cur, x_nxt = _row_tile_specs(tm, d, n_tiles)
    w_in_specs = [pl.BlockSpec((d, gw), lambda i, j, g=g: (0, g * nj + j))
                  for g in range(N_PROJ_GROUPS)]
    kern = functools.partial(_mixer_kernel, tiles_per_seq=seq_len // tm, n_steps=nj, layer=layer)
    return pl.pallas_call(
        kern,
        grid=(n_tiles, nj),
        in_specs=[
            x_cur, x_nxt,
            pl.BlockSpec((1, d), lambda i, j: (0, 0)),
            *w_in_specs,
            pl.BlockSpec((gw, d), lambda i, j: (j, 0)),
            pl.BlockSpec((gw, d), lambda i, j: (nj + j, 0)),
            pl.BlockSpec((hgrn_lb.shape[0], gw), lambda i, j: (0, j)),
            pl.BlockSpec((1, LANES), lambda i, j: (0, 0)),
            pl.BlockSpec((CONV_TAPS, gw), lambda i, j: (0, j)),
        ],
        out_specs=pl.BlockSpec((tm, d), lambda i, j: (i, 0)),
        out_shape=jax.ShapeDtypeStruct((m, d), F32),
        scratch_shapes=[
            pltpu.VMEM((2, tm, d), BF16),
            pltpu.VMEM((tm, tm), BF16),
            pltpu.VMEM((nj, HEADS_PER_STEP, LANES, LANES), F32),
            pltpu.VMEM((nj, SUBLANES, gw), F32),
        ],
        compiler_params=pltpu.CompilerParams(
            dimension_semantics=("arbitrary", "arbitrary"),
            vmem_limit_bytes=VMEM_LIMIT_BYTES),
        name="hybrid_mixer",
    )(x2d, x2d, norm_w, *([w_in] * N_PROJ_GROUPS), w_out, w_out, hgrn_lb, hgrn_norm_w, sconv_w)


def _attn_kernel(x_ref, xn_ref, nw_ref, wq_ref, k_ref, v_ref, wo_ref, o_ref, h_scr, *,
                 head_dim, n_steps):
    i = pl.program_id(0)
    j = pl.program_id(1)

    @pl.when(j == 0)
    def _():
        o_ref[...] = x_ref[...]

    h = _normalised_tile(x_ref, nw_ref, h_scr, i, j)
    q = _dot(h, wq_ref[...])
    scale = head_dim ** -0.5
    heads = []
    for hh in range(ATTN_HEADS_PER_STEP):
        cols = slice(hh * head_dim, (hh + 1) * head_dim)
        s = _dot_nt(q[:, cols].astype(BF16), k_ref[:, cols]) * scale
        s = s - jnp.max(s, axis=-1, keepdims=True)
        e = jnp.exp(s)
        pr = e / jnp.sum(e, axis=-1, keepdims=True)
        heads.append(_dot(pr.astype(BF16), v_ref[:, cols]))
    o = jnp.concatenate(heads, axis=1).astype(BF16)
    o_ref[...] += _dot(o, wo_ref[...])
    _normalise_next_slice(xn_ref, nw_ref, h_scr, i, j, n_steps)


def _attn(x2d, norm_w, wq, kv, wo, *, seq_len, mem_len):
    m, d = x2d.shape
    tm = ROW_TILE
    head_dim = d // MEM_HEADS
    tn = ATTN_HEADS_PER_STEP * head_dim
    tiles_per_seq = seq_len // tm
    x_cur, x_nxt = _row_tile_specs(tm, d, m // tm)
    kern = functools.partial(_attn_kernel, head_dim=head_dim, n_steps=d // tn)
    return pl.pallas_call(
        kern,
        grid=(m // tm, d // tn),
        in_specs=[
            x_cur, x_nxt,
            pl.BlockSpec((1, d), lambda i, j: (0, 0)),
            pl.BlockSpec((d, tn), lambda i, j: (0, j)),
            pl.BlockSpec((None, None, mem_len, tn), lambda i, j: (0, i // tiles_per_seq, 0, j)),
            pl.BlockSpec((None, None, mem_len, tn), lambda i, j: (1, i // tiles_per_seq, 0, j)),
            pl.BlockSpec((tn, d), lambda i, j: (j, 0)),
        ],
        out_specs=pl.BlockSpec((tm, d), lambda i, j: (i, 0)),
        out_shape=jax.ShapeDtypeStruct((m, d), F32),
        scratch_shapes=[pltpu.VMEM((2, tm, d), BF16)],
        compiler_params=pltpu.CompilerParams(
            dimension_semantics=("arbitrary", "arbitrary"),
            vmem_limit_bytes=VMEM_LIMIT_BYTES),
        name="mem_cross_attn",
    )(x2d, x2d, norm_w, wq, kv, kv, wo)


def _ffn_kernel(x_ref, nw_ref, wg_ref, wu_ref, cw_ref, cb_ref, wd_ref, fnw_ref, o_ref,
                h_scr, carry_scr, *, tiles_per_seq, n_steps, final_norm):
    i = pl.program_id(0)
    f = pl.program_id(1)
    tm = x_ref.shape[0]

    @pl.when(f == 0)
    def _():
        xv = x_ref[...]
        h_scr[...] = _rms(xv, nw_ref[...]).astype(BF16)
        o_ref[...] = xv

    @pl.when(i % tiles_per_seq == 0)
    def _():
        carry_scr[f] = jnp.zeros(carry_scr.shape[1:], F32)

    h = h_scr[...]
    a = _dot(h, wg_ref[...])
    up = _dot(h, wu_ref[...])
    prev8 = carry_scr[f]
    carry_scr[f] = a[tm - SUBLANES:]
    act = _silu(_causal_conv3(a, prev8, cw_ref[...]) + cb_ref[...]) * up
    o_ref[...] += _dot(act.astype(BF16), wd_ref[...])

    if final_norm:
        @pl.when(f == n_steps - 1)
        def _():
            o_ref[...] = _rms(o_ref[...], fnw_ref[...])


def _ffn(x2d, norm_w, w_gate, w_up, conv_w, conv_b, w_down, final_w, *, seq_len, final_norm):
    m, d = x2d.shape
    d_ff = w_gate.shape[1]
    tm, tf = FFN_ROW_TILE, FFN_TILE
    nf = d_ff // tf
    kern = functools.partial(_ffn_kernel, tiles_per_seq=seq_len // tm, n_steps=nf,
                             final_norm=final_norm)
    return pl.pallas_call(
        kern,
        grid=(m // tm, nf),
        in_specs=[
            pl.BlockSpec((tm, d), lambda i, f: (i, 0)),
            pl.BlockSpec((1, d), lambda i, f: (0, 0)),
            pl.BlockSpec((d, tf), lambda i, f: (0, f)),
            pl.BlockSpec((d, tf), lambda i, f: (0, f)),
            pl.BlockSpec((CONV_TAPS, tf), lambda i, f: (0, f)),
            pl.BlockSpec((1, tf), lambda i, f: (0, f)),
            pl.BlockSpec((tf, d), lambda i, f: (f, 0)),
            pl.BlockSpec((1, d), lambda i, f: (0, 0)),
        ],
        out_specs=pl.BlockSpec((tm, d), lambda i, f: (i, 0)),
        out_shape=jax.ShapeDtypeStruct((m, d), F32),
        scratch_shapes=[
            pltpu.VMEM((tm, d), BF16),
            pltpu.VMEM((nf, SUBLANES, tf), F32),
        ],
        compiler_params=pltpu.CompilerParams(
            dimension_semantics=("arbitrary", "arbitrary"),
            vmem_limit_bytes=FFN_VMEM_LIMIT_BYTES),
        name="conv_ffn",
    )(x2d, norm_w, w_gate, w_up, conv_w, conv_b, w_down, final_w)


def kernel(x, mem, hgrn_lb, norm1_w, w_in, hgrn_norm_w, sconv_w, w_out, norm2_w, mem_norm_w,
           wq, wk, wv, wo, norm3_w, w_gate, w_up, ffn_conv_w, ffn_conv_b, w_down, final_norm_w):
    batch, seq_len, d = x.shape
    mem_len = mem.shape[1]
    depth = norm1_w.shape[0]
    assert seq_len % ROW_TILE == 0 and seq_len % FFN_ROW_TILE == 0
    assert ROW_TILE % HGRN_BLOCK == 0 and HGRN_BLOCK % HGRN_CHUNK == 0
    assert HEADS_PER_STEP == 2 and HGRN_HEADS % HEADS_PER_STEP == 0
    assert w_in.shape[2] == N_PROJ_GROUPS * HGRN_HEADS * LANES and w_gate.shape[2] % FFN_TILE == 0

    xs = x.reshape(batch * seq_len, d)
    mem2d = mem.reshape(batch * mem_len, d)
    row = lambda w: w.reshape(1, -1)
    for l in range(depth):
        xs = _mixer(xs, row(norm1_w[l]), w_in[l].astype(BF16), w_out[l].astype(BF16), hgrn_lb,
                    row(hgrn_norm_w[l]), sconv_w[l], seq_len=seq_len, layer=l)
        kv = _kv_proj(mem2d, row(mem_norm_w[l]), jnp.stack([wk[l], wv[l]]).astype(BF16))
        kv = kv.reshape(2, batch, mem_len, d)
        xs = _attn(xs, row(norm2_w[l]), wq[l].astype(BF16), kv, wo[l].astype(BF16),
                   seq_len=seq_len, mem_len=mem_len)
        xs = _ffn(xs, row(norm3_w[l]), w_gate[l].astype(BF16), w_up[l].astype(BF16),
                  ffn_conv_w[l], row(ffn_conv_b[l]), w_down[l].astype(BF16),
                  row(final_norm_w), seq_len=seq_len, final_norm=(l == depth - 1))
    return xs.reshape(batch, seq_len, d)
```

```python
import functools

import jax
import jax.numpy as jnp
from jax import lax
from jax.experimental import pallas as pl
from jax.experimental.pallas import tpu as pltpu

F32 = jnp.float32
BF16 = jnp.bfloat16

EPS = 1e-6
HGRN_HEADS = 8
HGRN_CHUNK = 64
HGRN_BLOCK = 256
CONV_TAPS = 3
MEM_HEADS = 4
N_PROJ_GROUPS = 7
LANES = 128
SUBLANES = 8
BF16_ROWS = 16
VMEM_LIMIT_BYTES = 56 * 1024 * 1024
FFN_VMEM_LIMIT_BYTES = 60 * 1024 * 1024

ROW_TILE = 512
FFN_ROW_TILE = 1024
HEADS_PER_STEP = 2
ATTN_HEADS_PER_STEP = 2
FFN_TILE = 512


def _rms(x, w):
    ms = jnp.mean(x * x, axis=-1, keepdims=True)
    return x * lax.rsqrt(ms + EPS) * w


def _silu(x):
    return x * jax.nn.sigmoid(x)


def _dot(a, b):
    return jnp.dot(a, b, preferred_element_type=F32)


def _dot_nt(a, b):
    return lax.dot_general(a, b, (((1,), (1,)), ((), ())), preferred_element_type=F32)


def _dot_tn(a, b):
    return lax.dot_general(a, b, (((0,), (0,)), ((), ())), preferred_element_type=F32)


def _shift_rows(cur, prev8, shift):
    rolled = pltpu.roll(cur, shift, axis=0)
    head = pltpu.roll(prev8, shift, axis=0)
    rid = lax.broadcasted_iota(jnp.int32, prev8.shape, 0)
    first = jnp.where(rid < shift, head, rolled[:SUBLANES])
    return jnp.concatenate([first, rolled[SUBLANES:]], axis=0)


def _causal_conv3(cur, prev8, w):
    return (w[0:1] * _shift_rows(cur, prev8, 2)
            + w[1:2] * _shift_rows(cur, prev8, 1)
            + w[2:3] * cur)


def _normalised_tile(x_ref, nw_ref, h_scr, tile, step):
    @pl.when((tile == 0) & (step == 0))
    def _():
        h_scr[0] = _rms(x_ref[...], nw_ref[...]).astype(BF16)

    return h_scr[tile % 2]


def _normalise_next_slice(xn_ref, nw_ref, h_scr, tile, step, n_steps):
    tm = xn_ref.shape[0]
    rows = pl.cdiv(pl.cdiv(tm, n_steps), BF16_ROWS) * BF16_ROWS
    r0 = pl.multiple_of(jnp.minimum(step * rows, tm - rows), BF16_ROWS)
    h_scr[1 - tile % 2, pl.ds(r0, rows), :] = (
        _rms(xn_ref[pl.ds(r0, rows), :], nw_ref[...]).astype(BF16))


def _row_tile_specs(tm, d, n_tiles):
    cur = pl.BlockSpec((tm, d), lambda i, j: (i, 0))
    nxt = pl.BlockSpec((tm, d), lambda i, j: (jnp.minimum(i + 1, n_tiles - 1), 0))
    return cur, nxt


def _kv_kernel(mem_ref, nw_ref, w_ref, o_ref):
    h = _rms(mem_ref[...], nw_ref[...]).astype(BF16)
    o_ref[...] = _dot(h, w_ref[...]).astype(o_ref.dtype)


def _kv_proj(mem2d, norm_w, wkv):
    rows, d = mem2d.shape
    tn = 1024
    return pl.pallas_call(
        _kv_kernel,
        grid=(2, d // tn),
        in_specs=[
            pl.BlockSpec((rows, d), lambda s, n: (0, 0)),
            pl.BlockSpec((1, d), lambda s, n: (0, 0)),
            pl.BlockSpec((None, d, tn), lambda s, n: (s, 0, n)),
        ],
        out_specs=pl.BlockSpec((None, rows, tn), lambda s, n: (s, 0, n)),
        out_shape=jax.ShapeDtypeStruct((2, rows, d), BF16),
        compiler_params=pltpu.CompilerParams(
            dimension_semantics=("arbitrary", "arbitrary"),
            vmem_limit_bytes=VMEM_LIMIT_BYTES),
        name="kv_proj",
    )(mem2d, norm_w, wkv)


def _block_diag2(a, b):
    z = jnp.zeros(a.shape, a.dtype)
    return jnp.concatenate([jnp.concatenate([a, z], axis=1),
                            jnp.concatenate([z, b], axis=1)], axis=0)


def _mixer_kernel(x_ref, xn_ref, nw_ref, wq_ref, wf_ref, wi_ref, wg_ref, wb_ref, wc_ref, wh_ref,
                  wo_hgrn_ref, wo_conv_ref, lb_ref, hnw_ref, scw_ref, o_ref,
                  h_scr, tri_scr, state_scr, carry_scr, *, tiles_per_seq, n_steps, layer):
    i = pl.program_id(0)
    j = pl.program_id(1)
    tm = x_ref.shape[0]
    hd = LANES
    n_chunks = tm // HGRN_CHUNK

    @pl.when(j == 0)
    def _():
        o_ref[...] = x_ref[...]

    @pl.when((i == 0) & (j == 0))
    def _():
        r = lax.broadcasted_iota(jnp.int32, (tm, tm), 0)
        c = lax.broadcasted_iota(jnp.int32, (tm, tm), 1)
        same = (r // HGRN_CHUNK) == (c // HGRN_CHUNK)
        tri_scr[...] = jnp.where(same & (c <= r), 1.0, 0.0).astype(BF16)

    @pl.when(i % tiles_per_seq == 0)
    def _():
        state_scr[j] = jnp.zeros(state_scr.shape[1:], F32)
        carry_scr[j] = jnp.zeros(carry_scr.shape[1:], F32)

    h = _normalised_tile(x_ref, nw_ref, h_scr, i, j)
    f_pre = _dot(h, wf_ref[...])
    q = _dot(h, wq_ref[...])

    lb = jnp.sum(jax.nn.softmax(lb_ref[...], axis=0)[0:layer + 1], axis=0, keepdims=True)
    f = lb + (1.0 - lb) * jax.nn.sigmoid(f_pre)
    logf = jnp.log(f)
    kk = 1.0 - f
    qf = _silu(q)
    hi = logf.astype(BF16)
    lo = (logf - hi.astype(F32)).astype(BF16)
    vb = _dot(h, wi_ref[...]).astype(BF16)
    g = _dot(h, wg_ref[...])
    tri = tri_scr[...]
    b = _dot(tri, hi) + _dot(tri, lo)
    cb = _dot(h, wb_ref[...])
    cc = _dot(h, wc_ref[...])
    qt = (qf * jnp.exp(b)).astype(BF16)
    kt = (kk * jnp.exp(-b)).astype(BF16)

    t_id = lax.broadcasted_iota(jnp.int32, (HGRN_BLOCK, HGRN_BLOCK), 0)
    s_id = lax.broadcasted_iota(jnp.int32, (HGRN_BLOCK, HGRN_BLOCK), 1)
    keep = ((t_id // HGRN_CHUNK) == (s_id // HGRN_CHUNK)) & (s_id <= t_id)
    blocks = [slice(r0, r0 + HGRN_BLOCK) for r0 in range(0, tm, HGRN_BLOCK)]
    heads = (slice(0, hd), slice(hd, 2 * hd))
    scores = [[_dot_nt(qt[rows, cs], kt[rows, cs]) for cs in heads] for rows in blocks]

    chunks = [slice(c * HGRN_CHUNK, (c + 1) * HGRN_CHUNK) for c in range(n_chunks)]
    b_lasts = [b[rows][HGRN_CHUNK - 1:HGRN_CHUNK] for rows in chunks]
    upds = [_dot_tn(vb[rows], (kk[rows] * jnp.exp(bl - b[rows])).astype(BF16))
            for rows, bl in zip(chunks, b_lasts)]
    ch = _dot(h, wh_ref[...])
    _normalise_next_slice(xn_ref, nw_ref, h_scr, i, j, n_steps)

    intra = [_dot(jnp.concatenate([jnp.where(keep, s, 0.0).astype(BF16) for s in sc], axis=1),
                  _block_diag2(vb[rows, :hd], vb[rows, hd:]))
             for rows, sc in zip(blocks, scores)]

    st0 = state_scr[j, 0]
    st1 = state_scr[j, 1]
    starts = []
    for upd, bl in zip(upds, b_lasts):
        starts.append(_block_diag2(st0.astype(BF16), st1.astype(BF16)))
        decay = jnp.exp(bl)
        st0 = st0 * decay[:, :hd] + upd[:hd, :hd]
        st1 = st1 * decay[:, hd:] + upd[hd:, hd:]
    state_scr[j, 0] = st0
    state_scr[j, 1] = st1
    inter = [_dot_nt(qt[rows], start) for rows, start in zip(chunks, starts)]

    u = cc * ch
    prev8 = carry_scr[j]
    carry_scr[j] = u[tm - SUBLANES:]
    y = (cb * _causal_conv3(u, prev8, scw_ref[...])).astype(BF16)
    out = _dot(y, wo_conv_ref[...])

    hnw = hnw_ref[...]
    o = jnp.concatenate(intra, axis=0) + jnp.concatenate(inter, axis=0)
    o = jnp.concatenate([_rms(o[:, :hd], hnw), _rms(o[:, hd:], hnw)], axis=1)
    o_gated = (o * _silu(g)).astype(BF16)

    o_ref[...] += out + _dot(o_gated, wo_hgrn_ref[...])


def _mixer(x2d, norm_w, w_in, w_out, hgrn_lb, hgrn_norm_w, sconv_w, *, seq_len, layer):
    m, d = x2d.shape
    tm = ROW_TILE
    gw = HEADS_PER_STEP * LANES
    nj = w_in.shape[1] // (N_PROJ_GROUPS * gw)
    n_tiles = m // tm
    x_cur, x_nxt = _row_tile_specs(tm, d, n_tiles)
    w_in_specs = [pl.BlockSpec((d, gw), lambda i, j, g=g: (0, g * nj + j))
                  for g in range(N_PROJ_GROUPS)]
    kern = functools.partial(_mixer_kernel, tiles_per_seq=seq_len // tm, n_steps=nj, layer=layer)
    return pl.pallas_call(
        kern,
        grid=(n_tiles, nj),
        in_specs=[
            x_cur, x_nxt,
            pl.BlockSpec((1, d), lambda i, j: (0, 0)),
            *w_in_specs,
            pl.BlockSpec((gw, d), lambda i, j: (j, 0)),
            pl.BlockSpec((gw, d), lambda i, j: (nj + j, 0)),
            pl.BlockSpec((hgrn_lb.shape[0], gw), lambda i, j: (0, j)),
            pl.BlockSpec((1, LANES), lambda i, j: (0, 0)),
            pl.BlockSpec((CONV_TAPS, gw), lambda i, j: (0, j)),
        ],
        out_specs=pl.BlockSpec((tm, d), lambda i, j: (i, 0)),
        out_shape=jax.ShapeDtypeStruct((m, d), F32),
        scratch_shapes=[
            pltpu.VMEM((2, tm, d), BF16),
            pltpu.VMEM((tm, tm), BF16),
            pltpu.VMEM((nj, HEADS_PER_STEP, LANES, LANES), F32),
            pltpu.VMEM((nj, SUBLANES, gw), F32),
        ],
        compiler_params=pltpu.CompilerParams(
            dimension_semantics=("arbitrary", "arbitrary"),
            vmem_limit_bytes=VMEM_LIMIT_BYTES),
        name="hybrid_mixer",
    )(x2d, x2d, norm_w, *([w_in] * N_PROJ_GROUPS), w_out, w_out, hgrn_lb, hgrn_norm_w, sconv_w)


def _attn_kernel(x_ref, xn_ref, nw_ref, wq_ref, k_ref, v_ref, wo_ref, o_ref, h_scr, *,
                 head_dim, n_steps):
    i = pl.program_id(0)
    j = pl.program_id(1)

    @pl.when(j == 0)
    def _():
        o_ref[...] = x_ref[...]

    h = _normalised_tile(x_ref, nw_ref, h_scr, i, j)
    scale = head_dim ** -0.5
    cols = [slice(hh * head_dim, (hh + 1) * head_dim) for hh in range(ATTN_HEADS_PER_STEP)]
    qs = [_dot(h, wq_ref[:, c]).astype(BF16) for c in cols]
    _normalise_next_slice(xn_ref, nw_ref, h_scr, i, j, n_steps)
    ss =[_dot_nt(q, k_ref[:, c]) * scale for q, c in zip(qs, cols)]
    out = None
    for s, c in zip(ss, cols):
        e = jnp.exp(s - jnp.max(s, axis=-1, keepdims=True))
        pr = (e / jnp.sum(e, axis=-1, keepdims=True)).astype(BF16)
        part = _dot(_dot(pr, v_ref[:, c]).astype(BF16), wo_ref[c, :])
        out = part if out is None else out + part
    o_ref[...] += out


def _attn(x2d, norm_w, wq, kv, wo, *, seq_len, mem_len):
    m, d = x2d.shape
    tm = ROW_TILE
    head_dim = d // MEM_HEADS
    tn = ATTN_HEADS_PER_STEP * head_dim
    tiles_per_seq = seq_len // tm
    x_cur, x_nxt = _row_tile_specs(tm, d, m // tm)
    kern = functools.partial(_attn_kernel, head_dim=head_dim, n_steps=d // tn)
    return pl.pallas_call(
        kern,
        grid=(m // tm, d // tn),
        in_specs=[
            x_cur, x_nxt,
            pl.BlockSpec((1, d), lambda i, j: (0, 0)),
            pl.BlockSpec((d, tn), lambda i, j: (0, j)),
            pl.BlockSpec((None, None, mem_len, tn), lambda i, j: (0, i // tiles_per_seq, 0, j)),
            pl.BlockSpec((None, None, mem_len, tn), lambda i, j: (1, i // tiles_per_seq, 0, j)),
            pl.BlockSpec((tn, d), lambda i, j: (j, 0)),
        ],
        out_specs=pl.BlockSpec((tm, d), lambda i, j: (i, 0)),
        out_shape=jax.ShapeDtypeStruct((m, d), F32),
        scratch_shapes=[pltpu.VMEM((2, tm, d), BF16)],
        compiler_params=pltpu.CompilerParams(
            dimension_semantics=("arbitrary", "arbitrary"),
            vmem_limit_bytes=VMEM_LIMIT_BYTES),
        name="mem_cross_attn",
    )(x2d, x2d, norm_w, wq, kv, kv, wo)


def _ffn_kernel(x_ref, nw_ref, wg_ref, wu_ref, cw_ref, cb_ref, wd_ref, fnw_ref, o_ref,
                h_scr, carry_scr, *, tiles_per_seq, n_steps, final_norm):
    i = pl.program_id(0)
    f = pl.program_id(1)
    tm = x_ref.shape[0]

    @pl.when(f == 0)
    def _():
        xv = x_ref[...]
        h_scr[...] = _rms(xv, nw_ref[...]).astype(BF16)
        o_ref[...] = xv

    @pl.when(i % tiles_per_seq == 0)
    def _():
        carry_scr[f] = jnp.zeros(carry_scr.shape[1:], F32)

    h = h_scr[...]
    a = _dot(h, wg_ref[...])
    up = _dot(h, wu_ref[...])
    prev8 = carry_scr[f]
    carry_scr[f] = a[tm - SUBLANES:]
    act = _silu(_causal_conv3(a, prev8, cw_ref[...]) + cb_ref[...]) * up
    o_ref[...] += _dot(act.astype(BF16), wd_ref[...])

    if final_norm:
        @pl.when(f == n_steps - 1)
        def _():
            o_ref[...] = _rms(o_ref[...], fnw_ref[...])


def _ffn(x2d, norm_w, w_gate, w_up, conv_w, conv_b, w_down, final_w, *, seq_len, final_norm):
    m, d = x2d.shape
    d_ff = w_gate.shape[1]
    tm, tf = FFN_ROW_TILE, FFN_TILE
    nf = d_ff // tf
    kern = functools.partial(_ffn_kernel, tiles_per_seq=seq_len // tm, n_steps=nf,
                             final_norm=final_norm)
    return pl.pallas_call(
        kern,
        grid=(m // tm, nf),
        in_specs=[
            pl.BlockSpec((tm, d), lambda i, f: (i, 0)),
            pl.BlockSpec((1, d), lambda i, f: (0, 0)),
            pl.BlockSpec((d, tf), lambda i, f: (0, f)),
            pl.BlockSpec((d, tf), lambda i, f: (0, f)),
            pl.BlockSpec((CONV_TAPS, tf), lambda i, f: (0, f)),
            pl.BlockSpec((1, tf), lambda i, f: (0, f)),
            pl.BlockSpec((tf, d), lambda i, f: (f, 0)),
            pl.BlockSpec((1, d), lambda i, f: (0, 0)),
        ],
        out_specs=pl.BlockSpec((tm, d), lambda i, f: (i, 0)),
        out_shape=jax.ShapeDtypeStruct((m, d), F32),
        scratch_shapes=[
            pltpu.VMEM((tm, d), BF16),
            pltpu.VMEM((nf, SUBLANES, tf), F32),
        ],
        compiler_params=pltpu.CompilerParams(
            dimension_semantics=("arbitrary", "arbitrary"),
            vmem_limit_bytes=FFN_VMEM_LIMIT_BYTES),
        name="conv_ffn",
    )(x2d, norm_w, w_gate, w_up, conv_w, conv_b, w_down, final_w)


def kernel(x, mem, hgrn_lb, norm1_w, w_in, hgrn_norm_w, sconv_w, w_out, norm2_w, mem_norm_w,
           wq, wk, wv, wo, norm3_w, w_gate, w_up, ffn_conv_w, ffn_conv_b, w_down, final_norm_w):
    batch, seq_len, d = x.shape
    mem_len = mem.shape[1]
    depth = norm1_w.shape[0]
    assert seq_len % ROW_TILE == 0 and seq_len % FFN_ROW_TILE == 0
    assert ROW_TILE % HGRN_BLOCK == 0 and HGRN_BLOCK % HGRN_CHUNK == 0
    assert HEADS_PER_STEP == 2 and HGRN_HEADS % HEADS_PER_STEP == 0
    assert w_in.shape[2] == N_PROJ_GROUPS * HGRN_HEADS * LANES and w_gate.shape[2] % FFN_TILE == 0

    xs = x.reshape(batch * seq_len, d)
    mem2d = mem.reshape(batch * mem_len, d)
    row = lambda w: w.reshape(1, -1)
    for l in range(depth):
        xs = _mixer(xs, row(norm1_w[l]), w_in[l].astype(BF16), w_out[l].astype(BF16), hgrn_lb,
                    row(hgrn_norm_w[l]), sconv_w[l], seq_len=seq_len, layer=l)
        kv = _kv_proj(mem2d, row(mem_norm_w[l]), jnp.stack([wk[l], wv[l]]).astype(BF16))
        kv = kv.reshape(2, batch, mem_len, d)
        xs = _attn(xs, row(norm2_w[l]), wq[l].astype(BF16), kv, wo[l].astype(BF16),
                   seq_len=seq_len, mem_len=mem_len)
        xs = _ffn(xs, row(norm3_w[l]), w_gate[l].astype(BF16), w_up[l].astype(BF16),
                  ffn_conv_w[l], row(ffn_conv_b[l]), w_down[l].astype(BF16),
                  row(final_norm_w), seq_len=seq_len, final_norm=(l == depth - 1))
    return xs.reshape(batch, seq_len, d)
```

```python
import functools

import jax
import jax.numpy as jnp
from jax import lax
from jax.experimental import pallas as pl
from jax.experimental.pallas import tpu as pltpu

F32 = jnp.float32
BF16 = jnp.bfloat16

EPS = 1e-6
HGRN_HEADS = 8
HGRN_CHUNK = 64
HGRN_BLOCK = 256
CONV_TAPS = 3
MEM_HEADS = 4
N_PROJ_GROUPS = 7
LANES = 128
SUBLANES = 8
BF16_ROWS = 16
VMEM_LIMIT_BYTES = 56 * 1024 * 1024
FFN_VMEM_LIMIT_BYTES = 60 * 1024 * 1024

ROW_TILE = 512
FFN_ROW_TILE = 1024
HEADS_PER_STEP = 2
FFN_TILE = 512


def _rms(x, w):
    ms = jnp.mean(x * x, axis=-1, keepdims=True)
    return x * lax.rsqrt(ms + EPS) * w


def _silu(x):
    return x * jax.nn.sigmoid(x)


def _dot(a, b):
    return jnp.dot(a, b, preferred_element_type=F32)


def _dot_nt(a, b):
    return lax.dot_general(a, b, (((1,), (1,)), ((), ())), preferred_element_type=F32)


def _dot_tn(a, b):
    return lax.dot_general(a, b, (((0,), (0,)), ((), ())), preferred_element_type=F32)


def _shift_rows(cur, prev8, shift):
    rolled = pltpu.roll(cur, shift, axis=0)
    head = pltpu.roll(prev8, shift, axis=0)
    rid = lax.broadcasted_iota(jnp.int32, prev8.shape, 0)
    first = jnp.where(rid < shift, head, rolled[:SUBLANES])
    return jnp.concatenate([first, rolled[SUBLANES:]], axis=0)


def _causal_conv3(cur, prev8, w):
    return (w[0:1] * _shift_rows(cur, prev8, 2)
            + w[1:2] * _shift_rows(cur, prev8, 1)
            + w[2:3] * cur)


def _normalised_tile(x_ref, nw_ref, h_scr, tile, step):
    @pl.when((tile == 0) & (step == 0))
    def _():
        h_scr[0] = _rms(x_ref[...], nw_ref[...]).astype(BF16)

    return h_scr[tile % 2]


def _normalise_next_slice(xn_ref, nw_ref, h_scr, tile, step, n_steps):
    tm = xn_ref.shape[0]
    rows = pl.cdiv(pl.cdiv(tm, n_steps), BF16_ROWS) * BF16_ROWS
    r0 = pl.multiple_of(jnp.minimum(step * rows, tm - rows), BF16_ROWS)
    h_scr[1 - tile % 2, pl.ds(r0, rows), :] = (
        _rms(xn_ref[pl.ds(r0, rows), :], nw_ref[...]).astype(BF16))


def _row_tile_specs(tm, d, n_tiles):
    cur = pl.BlockSpec((tm, d), lambda i, j: (i, 0))
    nxt = pl.BlockSpec((tm, d), lambda i, j: (jnp.minimum(i + 1, n_tiles - 1), 0))
    return cur, nxt


def _kv_kernel(mem_ref, nw_ref, w_ref, o_ref):
    h = _rms(mem_ref[...], nw_ref[...]).astype(BF16)
    o_ref[...] = _dot(h, w_ref[...]).astype(o_ref.dtype)


def _kv_proj(mem2d, norm_w, wkv):
    rows, d = mem2d.shape
    tn = 1024
    return pl.pallas_call(
        _kv_kernel,
        grid=(2, d // tn),
        in_specs=[
            pl.BlockSpec((rows, d), lambda s, n: (0, 0)),
            pl.BlockSpec((1, d), lambda s, n: (0, 0)),
            pl.BlockSpec((None, d, tn), lambda s, n: (s, 0, n)),
        ],
        out_specs=pl.BlockSpec((None, rows, tn), lambda s, n: (s, 0, n)),
        out_shape=jax.ShapeDtypeStruct((2, rows, d), BF16),
        compiler_params=pltpu.CompilerParams(
            dimension_semantics=("arbitrary", "arbitrary"),
            vmem_limit_bytes=VMEM_LIMIT_BYTES),
        name="kv_proj",
    )(mem2d, norm_w, wkv)


def _block_diag2(a, b):
    z = jnp.zeros(a.shape, a.dtype)
    return jnp.concatenate([jnp.concatenate([a, z], axis=1),
                            jnp.concatenate([z, b], axis=1)], axis=0)


def _mixer_kernel(x_ref, xn_ref, nw_ref, wq_ref, wf_ref, wi_ref, wg_ref, wb_ref, wc_ref, wh_ref,
                  wo_hgrn_ref, wo_conv_ref, lb_ref, hnw_ref, scw_ref, o_ref,
                  h_scr, tri_scr, state_scr, carry_scr, *, tiles_per_seq, n_steps, layer):
    i = pl.program_id(0)
    j = pl.program_id(1)
    tm = x_ref.shape[0]
    hd = LANES
    n_chunks = tm // HGRN_CHUNK

    @pl.when(j == 0)
    def _():
        o_ref[...] = x_ref[...]

    @pl.when((i == 0) & (j == 0))
    def _():
        r = lax.broadcasted_iota(jnp.int32, (tm, tm), 0)
        c = lax.broadcasted_iota(jnp.int32, (tm, tm), 1)
        same = (r // HGRN_CHUNK) == (c // HGRN_CHUNK)
        tri_scr[...] = jnp.where(same & (c <= r), 1.0, 0.0).astype(BF16)

    @pl.when(i % tiles_per_seq == 0)
    def _():
        state_scr[j] = jnp.zeros(state_scr.shape[1:], F32)
        carry_scr[j] = jnp.zeros(carry_scr.shape[1:], F32)

    h = _normalised_tile(x_ref, nw_ref, h_scr, i, j)
    f_pre = _dot(h, wf_ref[...])
    q = _dot(h, wq_ref[...])

    lb = jnp.sum(jax.nn.softmax(lb_ref[...], axis=0)[0:layer + 1], axis=0, keepdims=True)
    f = lb + (1.0 - lb) * jax.nn.sigmoid(f_pre)
    logf = jnp.log(f)
    kk = 1.0 - f
    qf = _silu(q)
    hi = logf.astype(BF16)
    lo = (logf - hi.astype(F32)).astype(BF16)
    vb = _dot(h, wi_ref[...]).astype(BF16)
    g = _dot(h, wg_ref[...])
    _normalise_next_slice(xn_ref, nw_ref, h_scr, i, j, n_steps)
    h = h_scr[i % 2]
    tri = tri_scr[...]
    b = _dot(tri, hi) + _dot(tri, lo)
    cb = _dot(h, wb_ref[...])
    cc = _dot(h, wc_ref[...])
    qt = (qf * jnp.exp(b)).astype(BF16)
    kt = (kk * jnp.exp(-b)).astype(BF16)

    t_id = lax.broadcasted_iota(jnp.int32, (HGRN_BLOCK, HGRN_BLOCK), 0)
    s_id = lax.broadcasted_iota(jnp.int32, (HGRN_BLOCK, HGRN_BLOCK), 1)
    keep = ((t_id // HGRN_CHUNK) == (s_id // HGRN_CHUNK)) & (s_id <= t_id)
    blocks = [slice(r0, r0 + HGRN_BLOCK) for r0 in range(0, tm, HGRN_BLOCK)]
    heads = (slice(0, hd), slice(hd, 2 * hd))
    scores = [[_dot_nt(qt[rows, cs], kt[rows, cs]) for cs in heads] for rows in blocks]

    chunks = [slice(c * HGRN_CHUNK, (c + 1) * HGRN_CHUNK) for c in range(n_chunks)]
    b_lasts = [b[rows][HGRN_CHUNK - 1:HGRN_CHUNK] for rows in chunks]
    upds = [_dot_tn(vb[rows], (kk[rows] * jnp.exp(bl - b[rows])).astype(BF16))
            for rows, bl in zip(chunks, b_lasts)]
    ch = _dot(h, wh_ref[...])

    intra = [_dot(jnp.concatenate([jnp.where(keep, s, 0.0).astype(BF16) for s in sc], axis=1),
                  _block_diag2(vb[rows, :hd], vb[rows, hd:]))
             for rows, sc in zip(blocks, scores)]

    st0 = state_scr[j, 0]
    st1 = state_scr[j, 1]
    starts = []
    for upd, bl in zip(upds, b_lasts):
        starts.append(_block_diag2(st0.astype(BF16), st1.astype(BF16)))
        decay = jnp.exp(bl)
        st0 = st0 * decay[:, :hd] + upd[:hd, :hd]
        st1 = st1 * decay[:, hd:] + upd[hd:, hd:]
    state_scr[j, 0] = st0
    state_scr[j, 1] = st1
    inter = [_dot_nt(qt[rows], start) for rows, start in zip(chunks, starts)]

    u = cc * ch
    prev8 = carry_scr[j]
    carry_scr[j] = u[tm - SUBLANES:]
    y = (cb * _causal_conv3(u, prev8, scw_ref[...])).astype(BF16)
    out = _dot(y, wo_conv_ref[...])

    hnw = hnw_ref[...]
    o = jnp.concatenate(intra, axis=0) + jnp.concatenate(inter, axis=0)
    o = jnp.concatenate([_rms(o[:, :hd], hnw), _rms(o[:, hd:], hnw)], axis=1)
    o_gated = (o * _silu(g)).astype(BF16)

    o_ref[...] += out + _dot(o_gated, wo_hgrn_ref[...])


def _mixer(x2d, norm_w, w_in, w_out, hgrn_lb, hgrn_norm_w, sconv_w, *, seq_len, layer):
    m, d = x2d.shape
    tm = ROW_TILE
    gw = HEADS_PER_STEP * LANES
    nj = w_in.shape[1] // (N_PROJ_GROUPS * gw)
    n_tiles = m // tm
    x_cur, x_nxt = _row_tile_specs(tm, d, n_tiles)
    w_in_specs = [pl.BlockSpec((d, gw), lambda i, j, g=g: (0, g * nj + j))
                  for g in range(N_PROJ_GROUPS)]
    kern = functools.partial(_mixer_kernel, tiles_per_seq=seq_len // tm, n_steps=nj, layer=layer)
    return pl.pallas_call(
        kern,
        grid=(n_tiles, nj),
        in_specs=[
            x_cur, x_nxt,
            pl.BlockSpec((1, d), lambda i, j: (0, 0)),
            *w_in_specs,
            pl.BlockSpec((gw, d), lambda i, j: (j, 0)),
            pl.BlockSpec((gw, d), lambda i, j: (nj + j, 0)),
            pl.BlockSpec((hgrn_lb.shape[0], gw), lambda i, j: (0, j)),
            pl.BlockSpec((1, LANES), lambda i, j: (0, 0)),
            pl.BlockSpec((CONV_TAPS, gw), lambda i, j: (0, j)),
        ],
        out_specs=pl.BlockSpec((tm, d), lambda i, j: (i, 0)),
        out_shape=jax.ShapeDtypeStruct((m, d), F32),
        scratch_shapes=[
            pltpu.VMEM((2, tm, d), BF16),
            pltpu.VMEM((tm, tm), BF16),
            pltpu.VMEM((nj, HEADS_PER_STEP, LANES, LANES), F32),
            pltpu.VMEM((nj, SUBLANES, gw), F32),
        ],
        compiler_params=pltpu.CompilerParams(
            dimension_semantics=("arbitrary", "arbitrary"),
            vmem_limit_bytes=VMEM_LIMIT_BYTES),
        name="hybrid_mixer",
    )(x2d, x2d, norm_w, *([w_in] * N_PROJ_GROUPS), w_out, w_out, hgrn_lb, hgrn_norm_w, sconv_w)


def _attn_kernel(x_ref, xn_ref, nw_ref, wq_ref, k_ref, v_ref, wo_ref, o_ref, h_scr, *, head_dim):
    i = pl.program_id(0)
    h = _normalised_tile(x_ref, nw_ref, h_scr, i, 0)
    scale = head_dim ** -0.5
    cols = [slice(hh * head_dim, (hh + 1) * head_dim) for hh in range(MEM_HEADS)]
    half = MEM_HEADS // 2
    qs = [_dot(h, wq_ref[:, c]).astype(BF16) for c in cols[:half]]
    _normalise_next_slice(xn_ref, nw_ref, h_scr, i, 0, 1)
    h = h_scr[i % 2]
    qs += [_dot(h, wq_ref[:, c]).astype(BF16) for c in cols[half:]]
    ss = [_dot_nt(q, k_ref[:, c]) * scale for q, c in zip(qs, cols)]
    out = x_ref[...]
    for s, c in zip(ss, cols):
        e = jnp.exp(s - jnp.max(s, axis=-1, keepdims=True))
        pr = (e / jnp.sum(e, axis=-1, keepdims=True)).astype(BF16)
        out = out + _dot(_dot(pr, v_ref[:, c]).astype(BF16), wo_ref[c, :])
    o_ref[...] = out


def _attn(x2d, norm_w, wq, kv, wo, *, seq_len, mem_len):
    m, d = x2d.shape
    tm = ROW_TILE
    tiles_per_seq = seq_len // tm
    n_tiles = m // tm
    resident = dict(pipeline_mode=pl.Buffered(1))
    kern = functools.partial(_attn_kernel, head_dim=d // MEM_HEADS)
    return pl.pallas_call(
        kern,
        grid=(n_tiles,),
        in_specs=[
            pl.BlockSpec((tm, d), lambda i: (i, 0)),
            pl.BlockSpec((tm, d), lambda i: (jnp.minimum(i + 1, n_tiles - 1), 0)),
            pl.BlockSpec((1, d), lambda i: (0, 0)),
            pl.BlockSpec((d, d), lambda i: (0, 0), **resident),
            pl.BlockSpec((None, None, mem_len, d), lambda i: (0, i // tiles_per_seq, 0, 0)),
            pl.BlockSpec((None, None, mem_len, d), lambda i: (1, i // tiles_per_seq, 0, 0)),
            pl.BlockSpec((d, d), lambda i: (0, 0), **resident),
        ],
        out_specs=pl.BlockSpec((tm, d), lambda i: (i, 0)),
        out_shape=jax.ShapeDtypeStruct((m, d), F32),
        scratch_shapes=[pltpu.VMEM((2, tm, d), BF16)],
        compiler_params=pltpu.CompilerParams(
            dimension_semantics=("arbitrary",),
            vmem_limit_bytes=VMEM_LIMIT_BYTES),
        name="mem_cross_attn",
    )(x2d, x2d, norm_w, wq, kv, kv, wo)


def _ffn_kernel(x_ref, nw_ref, wg_ref, wu_ref, cw_ref, cb_ref, wd_ref, fnw_ref, o_ref,
                h_scr, carry_scr, *, tiles_per_seq, n_steps, final_norm):
    i = pl.program_id(0)
    f = pl.program_id(1)
    tm = x_ref.shape[0]

    @pl.when(f == 0)
    def _():
        xv = x_ref[...]
        h_scr[...] = _rms(xv, nw_ref[...]).astype(BF16)
        o_ref[...] = xv

    @pl.when(i % tiles_per_seq == 0)
    def _():
        carry_scr[f] = jnp.zeros(carry_scr.shape[1:], F32)

    h = h_scr[...]
    a = _dot(h, wg_ref[...])
    up = _dot(h, wu_ref[...])
    prev8 = carry_scr[f]
    carry_scr[f] = a[tm - SUBLANES:]
    act = _silu(_causal_conv3(a, prev8, cw_ref[...]) + cb_ref[...]) * up
    o_ref[...] += _dot(act.astype(BF16), wd_ref[...])

    if final_norm:
        @pl.when(f == n_steps - 1)
        def _():
            o_ref[...] = _rms(o_ref[...], fnw_ref[...])


def _ffn(x2d, norm_w, w_gate, w_up, conv_w, conv_b, w_down, final_w, *, seq_len, final_norm):
    m, d = x2d.shape
    d_ff = w_gate.shape[1]
    tm, tf = FFN_ROW_TILE, FFN_TILE
    nf = d_ff // tf
    kern = functools.partial(_ffn_kernel, tiles_per_seq=seq_len // tm, n_steps=nf,
                             final_norm=final_norm)
    return pl.pallas_call(
        kern,
        grid=(m // tm, nf),
        in_specs=[
            pl.BlockSpec((tm, d), lambda i, f: (i, 0)),
            pl.BlockSpec((1, d), lambda i, f: (0, 0)),
            pl.BlockSpec((d, tf), lambda i, f: (0, f)),
            pl.BlockSpec((d, tf), lambda i, f: (0, f)),
            pl.BlockSpec((CONV_TAPS, tf), lambda i, f: (0, f)),
            pl.BlockSpec((1, tf), lambda i, f: (0, f)),
            pl.BlockSpec((tf, d), lambda i, f: (f, 0)),
            pl.BlockSpec((1, d), lambda i, f: (0, 0)),
        ],
        out_specs=pl.BlockSpec((tm, d), lambda i, f: (i, 0)),
        out_shape=jax.ShapeDtypeStruct((m, d), F32),
        scratch_shapes=[
            pltpu.VMEM((tm, d), BF16),
            pltpu.VMEM((nf, SUBLANES, tf), F32),
        ],
        compiler_params=pltpu.CompilerParams(
            dimension_semantics=("arbitrary", "arbitrary"),
            vmem_limit_bytes=FFN_VMEM_LIMIT_BYTES),
        name="conv_ffn",
    )(x2d, norm_w, w_gate, w_up, conv_w, conv_b, w_down, final_w)


def kernel(x, mem, hgrn_lb, norm1_w, w_in, hgrn_norm_w, sconv_w, w_out, norm2_w, mem_norm_w,
           wq, wk, wv, wo, norm3_w, w_gate, w_up, ffn_conv_w, ffn_conv_b, w_down, final_norm_w):
    batch, seq_len, d = x.shape
    mem_len = mem.shape[1]
    depth = norm1_w.shape[0]
    assert seq_len % ROW_TILE == 0 and seq_len % FFN_ROW_TILE == 0
    assert ROW_TILE % HGRN_BLOCK == 0 and HGRN_BLOCK % HGRN_CHUNK == 0
    assert HEADS_PER_STEP == 2 and HGRN_HEADS % HEADS_PER_STEP == 0
    assert w_in.shape[2] == N_PROJ_GROUPS * HGRN_HEADS * LANES and w_gate.shape[2] % FFN_TILE == 0

    xs = x.reshape(batch * seq_len, d)
    mem2d = mem.reshape(batch * mem_len, d)
    row = lambda w: w.reshape(1, -1)
    for l in range(depth):
        xs = _mixer(xs, row(norm1_w[l]), w_in[l].astype(BF16), w_out[l].astype(BF16), hgrn_lb,
                    row(hgrn_norm_w[l]), sconv_w[l], seq_len=seq_len, layer=l)
        kv = _kv_proj(mem2d, row(mem_norm_w[l]), jnp.stack([wk[l], wv[l]]).astype(BF16))
        kv = kv.reshape(2, batch, mem_len, d)
        xs = _attn(xs, row(norm2_w[l]), wq[l].astype(BF16), kv, wo[l].astype(BF16),
                   seq_len=seq_len, mem_len=mem_len)
        xs = _ffn(xs, row(norm3_w[l]), w_gate[l].astype(BF16), w_up[l].astype(BF16),
                  ffn_conv_w[l], row(ffn_conv_b[l]), w_down[l].astype(BF16),
                  row(final_norm_w), seq_len=seq_len, final_norm=(l == depth - 1))
    return xs.reshape(batch, seq_len, d)
```

```python
import functools

import jax
import jax.numpy as jnp
from jax import lax
from jax.experimental import pallas as pl
from jax.experimental.pallas import tpu as pltpu

F32 = jnp.float32
BF16 = jnp.bfloat16

EPS = 1e-6
HGRN_HEADS = 8
HGRN_CHUNK = 64
HGRN_BLOCK = 128
CONV_TAPS = 3
MEM_HEADS = 4
N_PROJ_GROUPS = 7
LANES = 128
SUBLANES = 8
BF16_ROWS = 16
VMEM_LIMIT_BYTES = 56 * 1024 * 1024
FFN_VMEM_LIMIT_BYTES = 60 * 1024 * 1024

ROW_TILE = 512
FFN_ROW_TILE = 1024
HEADS_PER_STEP = 2
FFN_TILE = 512


def _rms(x, w):
    ms = jnp.mean(x * x, axis=-1, keepdims=True)
    return x * lax.rsqrt(ms + EPS) * w


def _silu(x):
    return x * jax.nn.sigmoid(x)


def _dot(a, b):
    return jnp.dot(a, b, preferred_element_type=F32)


def _dot_nt(a, b):
    return lax.dot_general(a, b, (((1,), (1,)), ((), ())), preferred_element_type=F32)


def _dot_tn(a, b):
    return lax.dot_general(a, b, (((0,), (0,)), ((), ())), preferred_element_type=F32)


def _shift_rows(cur, prev8, shift):
    rolled = pltpu.roll(cur, shift, axis=0)
    head = pltpu.roll(prev8, shift, axis=0)
    rid = lax.broadcasted_iota(jnp.int32, prev8.shape, 0)
    first = jnp.where(rid < shift, head, rolled[:SUBLANES])
    return jnp.concatenate([first, rolled[SUBLANES:]], axis=0)


def _causal_conv3(cur, prev8, w):
    return (w[0:1] * _shift_rows(cur, prev8, 2)
            + w[1:2] * _shift_rows(cur, prev8, 1)
            + w[2:3] * cur)


def _normalised_tile(x_ref, nw_ref, h_scr, tile, step):
    @pl.when((tile == 0) & (step == 0))
    def _():
        h_scr[0] = _rms(x_ref[...], nw_ref[...]).astype(BF16)

    return h_scr[tile % 2]


def _normalise_next_slice(xn_ref, nw_ref, h_scr, tile, step, n_steps):
    tm = xn_ref.shape[0]
    rows = pl.cdiv(pl.cdiv(tm, n_steps), BF16_ROWS) * BF16_ROWS
    r0 = pl.multiple_of(jnp.minimum(step * rows, tm - rows), BF16_ROWS)
    h_scr[1 - tile % 2, pl.ds(r0, rows), :] = (
        _rms(xn_ref[pl.ds(r0, rows), :], nw_ref[...]).astype(BF16))


def _row_tile_specs(tm, d, n_tiles):
    cur = pl.BlockSpec((tm, d), lambda i, j: (i, 0))
    nxt = pl.BlockSpec((tm, d), lambda i, j: (jnp.minimum(i + 1, n_tiles - 1), 0))
    return cur, nxt


def _kv_kernel(mem_ref, nw_ref, w_ref, o_ref):
    h = _rms(mem_ref[...], nw_ref[...]).astype(BF16)
    o_ref[...] = _dot(h, w_ref[...]).astype(o_ref.dtype)


def _kv_proj(mem2d, norm_w, wkv):
    rows, d = mem2d.shape
    tn = 1024
    return pl.pallas_call(
        _kv_kernel,
        grid=(2, d // tn),
        in_specs=[
            pl.BlockSpec((rows, d), lambda s, n: (0, 0)),
            pl.BlockSpec((1, d), lambda s, n: (0, 0)),
            pl.BlockSpec((None, d, tn), lambda s, n: (s, 0, n)),
        ],
        out_specs=pl.BlockSpec((None, rows, tn), lambda s, n: (s, 0, n)),
        out_shape=jax.ShapeDtypeStruct((2, rows, d), BF16),
        compiler_params=pltpu.CompilerParams(
            dimension_semantics=("arbitrary", "arbitrary"),
            vmem_limit_bytes=VMEM_LIMIT_BYTES),
        name="kv_proj",
    )(mem2d, norm_w, wkv)


def _chunk_cumsum(x):
    pos = lax.broadcasted_iota(jnp.int32, x.shape, 0) % HGRN_CHUNK
    s = 1
    while s < HGRN_CHUNK:
        x = x + jnp.where(pos >= s, pltpu.roll(x, s, axis=0), 0.0)
        s *= 2
    return x


def _block_diag2(a, b):
    z = jnp.zeros(a.shape, a.dtype)
    return jnp.concatenate([jnp.concatenate([a, z], axis=1),
                            jnp.concatenate([z, b], axis=1)], axis=0)


def _mixer_kernel(x_ref, xn_ref, nw_ref, wq_ref, wf_ref, wi_ref, wg_ref, wb_ref, wc_ref, wh_ref,
                  wo_hgrn_ref, wo_conv_ref, lb_ref, hnw_ref, scw_ref, o_ref,
                  h_scr, state_scr, carry_scr, *, tiles_per_seq, n_steps, layer):
    i = pl.program_id(0)
    j = pl.program_id(1)
    tm = x_ref.shape[0]
    hd = LANES
    n_chunks = tm // HGRN_CHUNK

    @pl.when(j == 0)
    def _():
        o_ref[...] = x_ref[...]

    @pl.when(i % tiles_per_seq == 0)
    def _():
        state_scr[j] = jnp.zeros(state_scr.shape[1:], F32)
        carry_scr[j] = jnp.zeros(carry_scr.shape[1:], F32)

    h = _normalised_tile(x_ref, nw_ref, h_scr, i, j)
    f_pre = _dot(h, wf_ref[...])
    q = _dot(h, wq_ref[...])

    lb = jnp.sum(jax.nn.softmax(lb_ref[...], axis=0)[0:layer + 1], axis=0, keepdims=True)
    f = lb + (1.0 - lb) * jax.nn.sigmoid(f_pre)
    logf = jnp.log(f)
    kk = 1.0 - f
    qf = _silu(q)
    b = _chunk_cumsum(logf)
    vb = _dot(h, wi_ref[...]).astype(BF16)
    g = _dot(h, wg_ref[...])
    _normalise_next_slice(xn_ref, nw_ref, h_scr, i, j, n_steps)
    h = h_scr[i % 2]
    cb = _dot(h, wb_ref[...])
    cc = _dot(h, wc_ref[...])
    qt = (qf * jnp.exp(b)).astype(BF16)
    kt = (kk * jnp.exp(-b)).astype(BF16)

    t_id = lax.broadcasted_iota(jnp.int32, (HGRN_BLOCK, HGRN_BLOCK), 0)
    s_id = lax.broadcasted_iota(jnp.int32, (HGRN_BLOCK, HGRN_BLOCK), 1)
    keep = ((t_id // HGRN_CHUNK) == (s_id // HGRN_CHUNK)) & (s_id <= t_id)
    blocks = [slice(r0, r0 + HGRN_BLOCK) for r0 in range(0, tm, HGRN_BLOCK)]
    heads = (slice(0, hd), slice(hd, 2 * hd))
    scores = [[_dot_nt(qt[rows, cs], kt[rows, cs]) for cs in heads] for rows in blocks]

    chunks = [slice(c * HGRN_CHUNK, (c + 1) * HGRN_CHUNK) for c in range(n_chunks)]
    b_lasts = [b[rows][HGRN_CHUNK - 1:HGRN_CHUNK] for rows in chunks]
    upds = [_dot_tn(vb[rows], (kk[rows] * jnp.exp(bl - b[rows])).astype(BF16))
            for rows, bl in zip(chunks, b_lasts)]
    ch = _dot(h, wh_ref[...])

    intra = [_dot(jnp.concatenate([jnp.where(keep, s, 0.0).astype(BF16) for s in sc], axis=1),
                  _block_diag2(vb[rows, :hd], vb[rows, hd:]))
             for rows, sc in zip(blocks, scores)]

    st0 = state_scr[j, 0]
    st1 = state_scr[j, 1]
    starts = []
    for upd, bl in zip(upds, b_lasts):
        starts.append(_block_diag2(st0.astype(BF16), st1.astype(BF16)))
        decay = jnp.exp(bl)
        st0 = st0 * decay[:, :hd] + upd[:hd, :hd]
        st1 = st1 * decay[:, hd:] + upd[hd:, hd:]
    state_scr[j, 0] = st0
    state_scr[j, 1] = st1
    inter = [_dot_nt(qt[rows], start) for rows, start in zip(chunks, starts)]

    u = cc * ch
    prev8 = carry_scr[j]
    carry_scr[j] = u[tm - SUBLANES:]
    y = (cb * _causal_conv3(u, prev8, scw_ref[...])).astype(BF16)
    out = _dot(y, wo_conv_ref[...])

    hnw = hnw_ref[...]
    o = jnp.concatenate(intra, axis=0) + jnp.concatenate(inter, axis=0)
    o = jnp.concatenate([_rms(o[:, :hd], hnw), _rms(o[:, hd:], hnw)], axis=1)
    o_gated = (o * _silu(g)).astype(BF16)

    o_ref[...] += out + _dot(o_gated, wo_hgrn_ref[...])


def _mixer(x2d, norm_w, w_in, w_out, hgrn_lb, hgrn_norm_w, sconv_w, *, seq_len, layer):
    m, d = x2d.shape
    tm = ROW_TILE
    gw = HEADS_PER_STEP * LANES
    nj = w_in.shape[1] // (N_PROJ_GROUPS * gw)
    n_tiles = m // tm
    x_cur, x_nxt = _row_tile_specs(tm, d, n_tiles)
    w_in_specs = [pl.BlockSpec((d, gw), lambda i, j, g=g: (0, g * nj + j))
                  for g in range(N_PROJ_GROUPS)]
    kern = functools.partial(_mixer_kernel, tiles_per_seq=seq_len // tm, n_steps=nj, layer=layer)
    return pl.pallas_call(
        kern,
        grid=(n_tiles, nj),
        in_specs=[
            x_cur, x_nxt,
            pl.BlockSpec((1, d), lambda i, j: (0, 0)),
            *w_in_specs,
            pl.BlockSpec((gw, d), lambda i, j: (j, 0)),
            pl.BlockSpec((gw, d), lambda i, j: (nj + j, 0)),
            pl.BlockSpec((hgrn_lb.shape[0], gw), lambda i, j: (0, j)),
            pl.BlockSpec((1, LANES), lambda i, j: (0, 0)),
            pl.BlockSpec((CONV_TAPS, gw), lambda i, j: (0, j)),
        ],
        out_specs=pl.BlockSpec((tm, d), lambda i, j: (i, 0)),
        out_shape=jax.ShapeDtypeStruct((m, d), F32),
        scratch_shapes=[
            pltpu.VMEM((2, tm, d), BF16),
            pltpu.VMEM((nj, HEADS_PER_STEP, LANES, LANES), F32),
            pltpu.VMEM((nj, SUBLANES, gw), F32),
        ],
        compiler_params=pltpu.CompilerParams(
            dimension_semantics=("arbitrary", "arbitrary"),
            vmem_limit_bytes=VMEM_LIMIT_BYTES),
        name="hybrid_mixer",
    )(x2d, x2d, norm_w, *([w_in] * N_PROJ_GROUPS), w_out, w_out, hgrn_lb, hgrn_norm_w, sconv_w)


def _attn_kernel(x_ref, xn_ref, nw_ref, wq_ref, k_ref, v_ref, wo_ref, o_ref, h_scr, *, head_dim):
    i = pl.program_id(0)
    h = _normalised_tile(x_ref, nw_ref, h_scr, i, 0)
    scale = head_dim ** -0.5
    cols = [slice(hh * head_dim, (hh + 1) * head_dim) for hh in range(MEM_HEADS)]
    half = MEM_HEADS // 2
    qs = [_dot(h, wq_ref[:, c]).astype(BF16) for c in cols[:half]]
    _normalise_next_slice(xn_ref, nw_ref, h_scr, i, 0, 1)
    h = h_scr[i % 2]
    qs += [_dot(h, wq_ref[:, c]).astype(BF16) for c in cols[half:]]
    ss = [_dot_nt(q, k_ref[:, c]) * scale for q, c in zip(qs, cols)]
    out = x_ref[...]
    for s, c in zip(ss, cols):
        e = jnp.exp(s - jnp.max(s, axis=-1, keepdims=True))
        pr = (e / jnp.sum(e, axis=-1, keepdims=True)).astype(BF16)
        out = out + _dot(_dot(pr, v_ref[:, c]).astype(BF16), wo_ref[c, :])
    o_ref[...] = out


def _attn(x2d, norm_w, wq, kv, wo, *, seq_len, mem_len):
    m, d = x2d.shape
    tm = ROW_TILE
    tiles_per_seq = seq_len // tm
    n_tiles = m // tm
    resident = dict(pipeline_mode=pl.Buffered(1))
    kern = functools.partial(_attn_kernel, head_dim=d // MEM_HEADS)
    return pl.pallas_call(
        kern,
        grid=(n_tiles,),
        in_specs=[
            pl.BlockSpec((tm, d), lambda i: (i, 0)),
            pl.BlockSpec((tm, d), lambda i: (jnp.minimum(i + 1, n_tiles - 1), 0)),
            pl.BlockSpec((1, d), lambda i: (0, 0)),
            pl.BlockSpec((d, d), lambda i: (0, 0), **resident),
            pl.BlockSpec((None, None, mem_len, d), lambda i: (0, i // tiles_per_seq, 0, 0)),
            pl.BlockSpec((None, None, mem_len, d), lambda i: (1, i // tiles_per_seq, 0, 0)),
            pl.BlockSpec((d, d), lambda i: (0, 0), **resident),
        ],
        out_specs=pl.BlockSpec((tm, d), lambda i: (i, 0)),
        out_shape=jax.ShapeDtypeStruct((m, d), F32),
        scratch_shapes=[pltpu.VMEM((2, tm, d), BF16)],
        compiler_params=pltpu.CompilerParams(
            dimension_semantics=("arbitrary",),
            vmem_limit_bytes=VMEM_LIMIT_BYTES),
        name="mem_cross_attn",
    )(x2d, x2d, norm_w, wq, kv, kv, wo)


def _ffn_kernel(x_ref, nw_ref, wg_ref, wu_ref, cw_ref, cb_ref, wd_ref, fnw_ref, o_ref,
                h_scr, carry_scr, *, tiles_per_seq, n_steps, final_norm):
    i = pl.program_id(0)
    f = pl.program_id(1)
    tm = x_ref.shape[0]

    @pl.when(f == 0)
    def _():
        xv = x_ref[...]
        h_scr[...] = _rms(xv, nw_ref[...]).astype(BF16)
        o_ref[...] = xv

    @pl.when(i % tiles_per_seq == 0)
    def _():
        carry_scr[f] = jnp.zeros(carry_scr.shape[1:], F32)

    h = h_scr[...]
    a = _dot(h, wg_ref[...])
    up = _dot(h, wu_ref[...])
    prev8 = carry_scr[f]
    carry_scr[f] = a[tm - SUBLANES:]
    act = _silu(_causal_conv3(a, prev8, cw_ref[...]) + cb_ref[...]) * up
    o_ref[...] += _dot(act.astype(BF16), wd_ref[...])

    if final_norm:
        @pl.when(f == n_steps - 1)
        def _():
            o_ref[...] = _rms(o_ref[...], fnw_ref[...])


def _ffn(x2d, norm_w, w_gate, w_up, conv_w, conv_b, w_down, final_w, *, seq_len, final_norm):
    m, d = x2d.shape
    d_ff = w_gate.shape[1]
    tm, tf = FFN_ROW_TILE, FFN_TILE
    nf = d_ff // tf
    kern = functools.partial(_ffn_kernel, tiles_per_seq=seq_len // tm, n_steps=nf,
                             final_norm=final_norm)
    return pl.pallas_call(
        kern,
        grid=(m // tm, nf),
        in_specs=[
            pl.BlockSpec((tm, d), lambda i, f: (i, 0)),
            pl.BlockSpec((1, d), lambda i, f: (0, 0)),
            pl.BlockSpec((d, tf), lambda i, f: (0, f)),
            pl.BlockSpec((d, tf), lambda i, f: (0, f)),
            pl.BlockSpec((CONV_TAPS, tf), lambda i, f: (0, f)),
            pl.BlockSpec((1, tf), lambda i, f: (0, f)),
            pl.BlockSpec((tf, d), lambda i, f: (f, 0)),
            pl.BlockSpec((1, d), lambda i, f: (0, 0)),
        ],
        out_specs=pl.BlockSpec((tm, d), lambda i, f: (i, 0)),
        out_shape=jax.ShapeDtypeStruct((m, d), F32),
        scratch_shapes=[
            pltpu.VMEM((tm, d), BF16),
            pltpu.VMEM((nf, SUBLANES, tf), F32),
        ],
        compiler_params=pltpu.CompilerParams(
            dimension_semantics=("arbitrary", "arbitrary"),
            vmem_limit_bytes=FFN_VMEM_LIMIT_BYTES),
        name="conv_ffn",
    )(x2d, norm_w, w_gate, w_up, conv_w, conv_b, w_down, final_w)


def kernel(x, mem, hgrn_lb, norm1_w, w_in, hgrn_norm_w, sconv_w, w_out, norm2_w, mem_norm_w,
           wq, wk, wv, wo, norm3_w, w_gate, w_up, ffn_conv_w, ffn_conv_b, w_down, final_norm_w):
    batch, seq_len, d = x.shape
    mem_len = mem.shape[1]
    depth = norm1_w.shape[0]
    assert seq_len % ROW_TILE == 0 and seq_len % FFN_ROW_TILE == 0
    assert ROW_TILE % HGRN_BLOCK == 0 and HGRN_BLOCK % HGRN_CHUNK == 0
    assert HEADS_PER_STEP == 2 and HGRN_HEADS % HEADS_PER_STEP == 0
    assert w_in.shape[2] == N_PROJ_GROUPS * HGRN_HEADS * LANES and w_gate.shape[2] % FFN_TILE == 0

    xs = x.reshape(batch * seq_len, d)
    mem2d = mem.reshape(batch * mem_len, d)
    row = lambda w: w.reshape(1, -1)
    for l in range(depth):
        xs = _mixer(xs, row(norm1_w[l]), w_in[l].astype(BF16), w_out[l].astype(BF16), hgrn_lb,
                    row(hgrn_norm_w[l]), sconv_w[l], seq_len=seq_len, layer=l)
        kv = _kv_proj(mem2d, row(mem_norm_w[l]), jnp.stack([wk[l], wv[l]]).astype(BF16))
        kv = kv.reshape(2, batch, mem_len, d)
        xs = _attn(xs, row(norm2_w[l]), wq[l].astype(BF16), kv, wo[l].astype(BF16),
                   seq_len=seq_len, mem_len=mem_len)
        xs = _ffn(xs, row(norm3_w[l]), w_gate[l].astype(BF16), w_up[l].astype(BF16),
                  ffn_conv_w[l], row(ffn_conv_b[l]), w_down[l].astype(BF16),
                  row(final_norm_w), seq_len=seq_len, final_norm=(l == depth - 1))
    return xs.reshape(batch, seq_len, d)
```

```python
import functools

import jax
import jax.numpy as jnp
from jax import lax
from jax.experimental import pallas as pl
from jax.experimental.pallas import tpu as pltpu

F32 = jnp.float32
BF16 = jnp.bfloat16

EPS = 1e-6
HGRN_HEADS = 8
HGRN_CHUNK = 64
HGRN_BLOCK = 128
CONV_TAPS = 3
MEM_HEADS = 4
N_PROJ_GROUPS = 7
LANES = 128
SUBLANES = 8
BF16_ROWS = 16
VMEM_LIMIT_BYTES = 56 * 1024 * 1024
FFN_VMEM_LIMIT_BYTES = 60 * 1024 * 1024
ATTN_VMEM_LIMIT_BYTES = 62 * 1024 * 1024

ROW_TILE = 512
FFN_ROW_TILE = 1024
HEADS_PER_STEP = 2
FFN_TILE = 512


def _rms(x, w):
    ms = jnp.mean(x * x, axis=-1, keepdims=True)
    return x * lax.rsqrt(ms + EPS) * w


def _silu(x):
    return x * jax.nn.sigmoid(x)


def _dot(a, b):
    return jnp.dot(a, b, preferred_element_type=F32)


def _dot_nt(a, b):
    return lax.dot_general(a, b, (((1,), (1,)), ((), ())), preferred_element_type=F32)


def _dot_tn(a, b):
    return lax.dot_general(a, b, (((0,), (0,)), ((), ())), preferred_element_type=F32)


def _shift_rows(cur, prev8, shift):
    rolled = pltpu.roll(cur, shift, axis=0)
    head = pltpu.roll(prev8, shift, axis=0)
    rid = lax.broadcasted_iota(jnp.int32, prev8.shape, 0)
    first = jnp.where(rid < shift, head, rolled[:SUBLANES])
    return jnp.concatenate([first, rolled[SUBLANES:]], axis=0)


def _causal_conv3(cur, prev8, w):
    return (w[0:1] * _shift_rows(cur, prev8, 2)
            + w[1:2] * _shift_rows(cur, prev8, 1)
            + w[2:3] * cur)


def _normalised_tile(x_ref, nw_ref, h_scr, tile, step):
    @pl.when((tile == 0) & (step == 0))
    def _():
        h_scr[0] = _rms(x_ref[...], nw_ref[...]).astype(BF16)

    return h_scr[tile % 2]


def _normalise_next_slice(xn_ref, nw_ref, h_scr, tile, step, n_steps):
    tm = xn_ref.shape[0]
    rows = pl.cdiv(pl.cdiv(tm, n_steps), BF16_ROWS) * BF16_ROWS
    r0 = pl.multiple_of(jnp.minimum(step * rows, tm - rows), BF16_ROWS)
    h_scr[1 - tile % 2, pl.ds(r0, rows), :] = (
        _rms(xn_ref[pl.ds(r0, rows), :], nw_ref[...]).astype(BF16))


def _row_tile_specs(tm, d, n_tiles):
    cur = pl.BlockSpec((tm, d), lambda i, j: (i, 0))
    nxt = pl.BlockSpec((tm, d), lambda i, j: (jnp.minimum(i + 1, n_tiles - 1), 0))
    return cur, nxt


def _kv_kernel(mem_ref, nw_ref, w_ref, o_ref):
    h = _rms(mem_ref[...], nw_ref[...]).astype(BF16)
    o_ref[...] = _dot(h, w_ref[...]).astype(o_ref.dtype)


def _kv_proj(mem2d, norm_w, wkv):
    rows, d = mem2d.shape
    tn = 1024
    return pl.pallas_call(
        _kv_kernel,
        grid=(2, d // tn),
        in_specs=[
            pl.BlockSpec((rows, d), lambda s, n: (0, 0)),
            pl.BlockSpec((1, d), lambda s, n: (0, 0)),
            pl.BlockSpec((None, d, tn), lambda s, n: (s, 0, n)),
        ],
        out_specs=pl.BlockSpec((None, rows, tn), lambda s, n: (s, 0, n)),
        out_shape=jax.ShapeDtypeStruct((2, rows, d), BF16),
        compiler_params=pltpu.CompilerParams(
            dimension_semantics=("arbitrary", "arbitrary"),
            vmem_limit_bytes=VMEM_LIMIT_BYTES),
        name="kv_proj",
    )(mem2d, norm_w, wkv)


def _chunk_cumsum(x):
    pos = lax.broadcasted_iota(jnp.int32, x.shape, 0) % HGRN_CHUNK
    s = 1
    while s < HGRN_CHUNK:
        x = x + jnp.where(pos >= s, pltpu.roll(x, s, axis=0), 0.0)
        s *= 2
    return x


def _block_diag2(a, b):
    z = jnp.zeros(a.shape, a.dtype)
    return jnp.concatenate([jnp.concatenate([a, z], axis=1),
                            jnp.concatenate([z, b], axis=1)], axis=0)


def _mixer_kernel(x_ref, xn_ref, nw_ref, wq_ref, wf_ref, wi_ref, wg_ref, wb_ref, wc_ref, wh_ref,
                  wo_hgrn_ref, wo_conv_ref, lb_ref, hnw_ref, scw_ref, o_ref,
                  h_scr, state_scr, carry_scr, *, tiles_per_seq, n_steps, layer):
    i = pl.program_id(0)
    j = pl.program_id(1)
    tm = x_ref.shape[0]
    hd = LANES
    n_chunks = tm // HGRN_CHUNK

    @pl.when(j == 0)
    def _():
        o_ref[...] = x_ref[...]

    @pl.when(i % tiles_per_seq == 0)
    def _():
        state_scr[j] = jnp.zeros(state_scr.shape[1:], F32)
        carry_scr[j] = jnp.zeros(carry_scr.shape[1:], F32)

    h = _normalised_tile(x_ref, nw_ref, h_scr, i, j)
    f_pre = _dot(h, wf_ref[...])
    q = _dot(h, wq_ref[...])

    lb = jnp.sum(jax.nn.softmax(lb_ref[...], axis=0)[0:layer + 1], axis=0, keepdims=True)
    f = lb + (1.0 - lb) * jax.nn.sigmoid(f_pre)
    logf = jnp.log(f)
    kk = 1.0 - f
    qf = _silu(q)
    b = _chunk_cumsum(logf)
    vb = _dot(h, wi_ref[...]).astype(BF16)
    g = _dot(h, wg_ref[...])
    _normalise_next_slice(xn_ref, nw_ref, h_scr, i, j, n_steps)
    h = h_scr[i % 2]
    cb = _dot(h, wb_ref[...])
    cc = _dot(h, wc_ref[...])
    qt = (qf * jnp.exp(b)).astype(BF16)
    kt = (kk * jnp.exp(-b)).astype(BF16)

    t_id = lax.broadcasted_iota(jnp.int32, (HGRN_BLOCK, HGRN_BLOCK), 0)
    s_id = lax.broadcasted_iota(jnp.int32, (HGRN_BLOCK, HGRN_BLOCK), 1)
    keep = ((t_id // HGRN_CHUNK) == (s_id // HGRN_CHUNK)) & (s_id <= t_id)
    blocks = [slice(r0, r0 + HGRN_BLOCK) for r0 in range(0, tm, HGRN_BLOCK)]
    heads = (slice(0, hd), slice(hd, 2 * hd))
    scores = [[_dot_nt(qt[rows, cs], kt[rows, cs]) for cs in heads] for rows in blocks]

    chunks = [slice(c * HGRN_CHUNK, (c + 1) * HGRN_CHUNK) for c in range(n_chunks)]
    b_lasts = [b[rows][HGRN_CHUNK - 1:HGRN_CHUNK] for rows in chunks]
    upds = [_dot_tn(vb[rows], (kk[rows] * jnp.exp(bl - b[rows])).astype(BF16))
            for rows, bl in zip(chunks, b_lasts)]
    ch = _dot(h, wh_ref[...])

    intra = [_dot(jnp.concatenate([jnp.where(keep, s, 0.0).astype(BF16) for s in sc], axis=1),
                  _block_diag2(vb[rows, :hd], vb[rows, hd:]))
             for rows, sc in zip(blocks, scores)]

    st0 = state_scr[j, 0]
    st1 = state_scr[j, 1]
    starts = []
    for upd, bl in zip(upds, b_lasts):
        starts.append(_block_diag2(st0.astype(BF16), st1.astype(BF16)))
        decay = jnp.exp(bl)
        st0 = st0 * decay[:, :hd] + upd[:hd, :hd]
        st1 = st1 * decay[:, hd:] + upd[hd:, hd:]
    state_scr[j, 0] = st0
    state_scr[j, 1] = st1
    inter = [_dot_nt(qt[rows], start) for rows, start in zip(chunks, starts)]

    u = cc * ch
    prev8 = carry_scr[j]
    carry_scr[j] = u[tm - SUBLANES:]
    y = (cb * _causal_conv3(u, prev8, scw_ref[...])).astype(BF16)
    out = _dot(y, wo_conv_ref[...])

    hnw = hnw_ref[...]
    o = jnp.concatenate(intra, axis=0) + jnp.concatenate(inter, axis=0)
    o = jnp.concatenate([_rms(o[:, :hd], hnw), _rms(o[:, hd:], hnw)], axis=1)
    o_gated = (o * _silu(g)).astype(BF16)

    o_ref[...] += out + _dot(o_gated, wo_hgrn_ref[...])


def _mixer(x2d, norm_w, w_in, w_out, hgrn_lb, hgrn_norm_w, sconv_w, *, seq_len, layer):
    m, d = x2d.shape
    tm = ROW_TILE
    gw = HEADS_PER_STEP * LANES
    nj = w_in.shape[1] // (N_PROJ_GROUPS * gw)
    n_tiles = m // tm
    x_cur, x_nxt = _row_tile_specs(tm, d, n_tiles)
    w_in_specs = [pl.BlockSpec((d, gw), lambda i, j, g=g: (0, g * nj + j))
                  for g in range(N_PROJ_GROUPS)]
    kern = functools.partial(_mixer_kernel, tiles_per_seq=seq_len // tm, n_steps=nj, layer=layer)
    return pl.pallas_call(
        kern,
        grid=(n_tiles, nj),
        in_specs=[
            x_cur, x_nxt,
            pl.BlockSpec((1, d), lambda i, j: (0, 0)),
            *w_in_specs,
            pl.BlockSpec((gw, d), lambda i, j: (j, 0)),
            pl.BlockSpec((gw, d), lambda i, j: (nj + j, 0)),
            pl.BlockSpec((hgrn_lb.shape[0], gw), lambda i, j: (0, j)),
            pl.BlockSpec((1, LANES), lambda i, j: (0, 0)),
            pl.BlockSpec((CONV_TAPS, gw), lambda i, j: (0, j)),
        ],
        out_specs=pl.BlockSpec((tm, d), lambda i, j: (i, 0)),
        out_shape=jax.ShapeDtypeStruct((m, d), F32),
        scratch_shapes=[
            pltpu.VMEM((2, tm, d), BF16),
            pltpu.VMEM((nj, HEADS_PER_STEP, LANES, LANES), F32),
            pltpu.VMEM((nj, SUBLANES, gw), F32),
        ],
        compiler_params=pltpu.CompilerParams(
            dimension_semantics=("arbitrary", "arbitrary"),
            vmem_limit_bytes=VMEM_LIMIT_BYTES),
        name="hybrid_mixer",
    )(x2d, x2d, norm_w, *([w_in] * N_PROJ_GROUPS), w_out, w_out, hgrn_lb, hgrn_norm_w, sconv_w)


def _attn_kernel(x_ref, xn_ref, nw_ref, wq_ref, k_ref, v_ref, wo_ref, *rest, head_dim, n_cast):
    cast_in, o_ref = rest[:n_cast], rest[n_cast]
    cast_out, h_scr = rest[n_cast + 1:2 * n_cast + 1], rest[2 * n_cast + 1]
    i = pl.program_id(0)
    h = _normalised_tile(x_ref, nw_ref, h_scr, i, 0)
    scale = head_dim ** -0.5
    cols = [slice(hh * head_dim, (hh + 1) * head_dim) for hh in range(MEM_HEADS)]
    half = MEM_HEADS // 2
    qs = [_dot(h, wq_ref[:, c]).astype(BF16) for c in cols[:half]]
    _normalise_next_slice(xn_ref, nw_ref, h_scr, i, 0, 1)
    for src, dst in zip(cast_in, cast_out):
        dst[...] = src[...].astype(dst.dtype)
    h = h_scr[i % 2]
    qs +=[_dot(h, wq_ref[:, c]).astype(BF16) for c in cols[half:]]
    ss = [_dot_nt(q, k_ref[:, c]) * scale for q, c in zip(qs, cols)]
    out = x_ref[...]
    for s, c in zip(ss, cols):
        e = jnp.exp(s - jnp.max(s, axis=-1, keepdims=True))
        pr = (e / jnp.sum(e, axis=-1, keepdims=True)).astype(BF16)
        out = out + _dot(_dot(pr, v_ref[:, c]).astype(BF16), wo_ref[c, :])
    o_ref[...] = out


def _cast_row_block(rows, n_steps):
    rb = pl.cdiv(pl.cdiv(rows, n_steps), BF16_ROWS) * BF16_ROWS
    while rows % rb:
        rb += BF16_ROWS
    return rb


def _attn(x2d, norm_w, wq, kv, wo, to_cast, *, seq_len, mem_len):
    m, d = x2d.shape
    tm = ROW_TILE
    tiles_per_seq = seq_len // tm
    n_tiles = m // tm
    resident = dict(pipeline_mode=pl.Buffered(1))

    def cast_spec(w):
        rb = _cast_row_block(w.shape[0], n_tiles)
        last = w.shape[0] // rb - 1
        return pl.BlockSpec((rb, w.shape[1]), lambda i: (jnp.minimum(i, last), 0))

    cast_specs = [cast_spec(w) for w in to_cast]
    kern = functools.partial(_attn_kernel, head_dim=d // MEM_HEADS, n_cast=len(to_cast))
    out, *casted = pl.pallas_call(
        kern,
        grid=(n_tiles,),
        in_specs=[
            pl.BlockSpec((tm, d), lambda i: (i, 0)),
            pl.BlockSpec((tm, d), lambda i: (jnp.minimum(i + 1, n_tiles - 1), 0)),
            pl.BlockSpec((1, d), lambda i: (0, 0)),
            pl.BlockSpec((d, d), lambda i: (0, 0), **resident),
            pl.BlockSpec((None, None, mem_len, d), lambda i: (0, i // tiles_per_seq, 0, 0)),
            pl.BlockSpec((None, None, mem_len, d), lambda i: (1, i // tiles_per_seq, 0, 0)),
            pl.BlockSpec((d, d), lambda i: (0, 0), **resident),
            *cast_specs,
        ],
        out_specs=[pl.BlockSpec((tm, d), lambda i: (i, 0)), *cast_specs],
        out_shape=[jax.ShapeDtypeStruct((m, d), F32),
                   *(jax.ShapeDtypeStruct(w.shape, BF16) for w in to_cast)],
        scratch_shapes=[pltpu.VMEM((2, tm, d), BF16)],
        compiler_params=pltpu.CompilerParams(
            dimension_semantics=("arbitrary",),
            vmem_limit_bytes=ATTN_VMEM_LIMIT_BYTES),
        name="mem_cross_attn",
    )(x2d, x2d, norm_w, wq, kv, kv, wo, *to_cast)
    return out, casted


def _ffn_kernel(x_ref, nw_ref, wg_ref, wu_ref, cw_ref, cb_ref, wd_ref, fnw_ref, o_ref,
                h_scr, carry_scr, *, tiles_per_seq, n_steps, final_norm):
    i = pl.program_id(0)
    f = pl.program_id(1)
    tm = x_ref.shape[0]

    @pl.when(f == 0)
    def _():
        xv = x_ref[...]
        h_scr[...] = _rms(xv, nw_ref[...]).astype(BF16)
        o_ref[...] = xv

    @pl.when(i % tiles_per_seq == 0)
    def _():
        carry_scr[f] = jnp.zeros(carry_scr.shape[1:], F32)

    h = h_scr[...]
    a = _dot(h, wg_ref[...])
    up = _dot(h, wu_ref[...])
    prev8 = carry_scr[f]
    carry_scr[f] = a[tm - SUBLANES:]
    act = _silu(_causal_conv3(a, prev8, cw_ref[...]) + cb_ref[...]) * up
    o_ref[...] += _dot(act.astype(BF16), wd_ref[...])

    if final_norm:
        @pl.when(f == n_steps - 1)
        def _():
            o_ref[...] = _rms(o_ref[...], fnw_ref[...])


def _ffn(x2d, norm_w, w_gate, w_up, conv_w, conv_b, w_down, final_w, *, seq_len, final_norm):
    m, d = x2d.shape
    d_ff = w_gate.shape[1]
    tm, tf = FFN_ROW_TILE, FFN_TILE
    nf = d_ff // tf
    kern = functools.partial(_ffn_kernel, tiles_per_seq=seq_len // tm, n_steps=nf,
                             final_norm=final_norm)
    return pl.pallas_call(
        kern,
        grid=(m // tm, nf),
        in_specs=[
            pl.BlockSpec((tm, d), lambda i, f: (i, 0)),
            pl.BlockSpec((1, d), lambda i, f: (0, 0)),
            pl.BlockSpec((d, tf), lambda i, f: (0, f)),
            pl.BlockSpec((d, tf), lambda i, f: (0, f)),
            pl.BlockSpec((CONV_TAPS, tf), lambda i, f: (0, f)),
            pl.BlockSpec((1, tf), lambda i, f: (0, f)),
            pl.BlockSpec((tf, d), lambda i, f: (f, 0)),
            pl.BlockSpec((1, d), lambda i, f: (0, 0)),
        ],
        out_specs=pl.BlockSpec((tm, d), lambda i, f: (i, 0)),
        out_shape=jax.ShapeDtypeStruct((m, d), F32),
        scratch_shapes=[
            pltpu.VMEM((tm, d), BF16),
            pltpu.VMEM((nf, SUBLANES, tf), F32),
        ],
        compiler_params=pltpu.CompilerParams(
            dimension_semantics=("arbitrary", "arbitrary"),
            vmem_limit_bytes=FFN_VMEM_LIMIT_BYTES),
        name="conv_ffn",
    )(x2d, norm_w, w_gate, w_up, conv_w, conv_b, w_down, final_w)


def kernel(x, mem, hgrn_lb, norm1_w, w_in, hgrn_norm_w, sconv_w, w_out, norm2_w, mem_norm_w,
           wq, wk, wv, wo, norm3_w, w_gate, w_up, ffn_conv_w, ffn_conv_b, w_down, final_norm_w):
    batch, seq_len, d = x.shape
    mem_len = mem.shape[1]
    depth = norm1_w.shape[0]
    assert seq_len % ROW_TILE == 0 and seq_len % FFN_ROW_TILE == 0
    assert ROW_TILE % HGRN_BLOCK == 0 and HGRN_BLOCK % HGRN_CHUNK == 0
    assert HEADS_PER_STEP == 2 and HGRN_HEADS % HEADS_PER_STEP == 0
    assert w_in.shape[2] == N_PROJ_GROUPS * HGRN_HEADS * LANES and w_gate.shape[2] % FFN_TILE == 0

    xs = x.reshape(batch * seq_len, d)
    mem2d = mem.reshape(batch * mem_len, d)
    row = lambda w: w.reshape(1, -1)
    for l in range(depth):
        xs = _mixer(xs, row(norm1_w[l]), w_in[l].astype(BF16), w_out[l].astype(BF16), hgrn_lb,
                    row(hgrn_norm_w[l]), sconv_w[l], seq_len=seq_len, layer=l)
        kv = _kv_proj(mem2d, row(mem_norm_w[l]), jnp.stack([wk[l], wv[l]]).astype(BF16))
        kv = kv.reshape(2, batch, mem_len, d)
        xs, (wg_b, wu_b, wd_b) = _attn(
            xs, row(norm2_w[l]), wq[l].astype(BF16), kv, wo[l].astype(BF16),
            (w_gate[l], w_up[l], w_down[l]), seq_len=seq_len, mem_len=mem_len)
        xs = _ffn(xs, row(norm3_w[l]), wg_b, wu_b, ffn_conv_w[l], row(ffn_conv_b[l]), wd_b,
                  row(final_norm_w), seq_len=seq_len, final_norm=(l == depth - 1))
    return xs.reshape(batch, seq_len, d)
```

```python
import functools

import jax
import jax.numpy as jnp
from jax import lax
from jax.experimental import pallas as pl
from jax.experimental.pallas import tpu as pltpu

F32 = jnp.float32
BF16 = jnp.bfloat16

EPS = 1e-6
HGRN_HEADS = 8
HGRN_CHUNK = 64
HGRN_SAFE_LOG_DECAY = 75.0
HGRN_BLOCK = 128
CONV_TAPS = 3
MEM_HEADS = 4
N_PROJ_GROUPS = 7
LANES = 128
SUBLANES = 8
BF16_ROWS = 16
VMEM_LIMIT_BYTES = 56 * 1024 * 1024
FFN_VMEM_LIMIT_BYTES = 60 * 1024 * 1024
ATTN_VMEM_LIMIT_BYTES = 62 * 1024 * 1024

ROW_TILE = 512
FFN_ROW_TILE = 1024
HEADS_PER_STEP = 2
FFN_TILE = 512


def _rms(x, w):
    ms = jnp.mean(x * x, axis=-1, keepdims=True)
    return x * lax.rsqrt(ms + EPS) * w


def _silu(x):
    return x * jax.nn.sigmoid(x)


def _dot(a, b):
    return jnp.dot(a, b, preferred_element_type=F32)


def _dot_nt(a, b):
    return lax.dot_general(a, b, (((1,), (1,)), ((), ())), preferred_element_type=F32)


def _dot_tn(a, b):
    return lax.dot_general(a, b, (((0,), (0,)), ((), ())), preferred_element_type=F32)


def _shift_rows(cur, prev8, shift):
    rolled = pltpu.roll(cur, shift, axis=0)
    head = pltpu.roll(prev8, shift, axis=0)
    rid = lax.broadcasted_iota(jnp.int32, prev8.shape, 0)
    first = jnp.where(rid < shift, head, rolled[:SUBLANES])
    return jnp.concatenate([first, rolled[SUBLANES:]], axis=0)


def _causal_conv3(cur, prev8, w):
    return (w[0:1] * _shift_rows(cur, prev8, 2)
            + w[1:2] * _shift_rows(cur, prev8, 1)
            + w[2:3] * cur)


def _normalised_tile(x_ref, nw_ref, h_scr, tile, step):
    @pl.when((tile == 0) & (step == 0))
    def _():
        h_scr[0] = _rms(x_ref[...], nw_ref[...]).astype(BF16)

    return h_scr[tile % 2]


def _normalise_next_slice(xn_ref, nw_ref, h_scr, tile, step, n_steps):
    tm = xn_ref.shape[0]
    rows = pl.cdiv(pl.cdiv(tm, n_steps), BF16_ROWS) * BF16_ROWS
    r0 = pl.multiple_of(jnp.minimum(step * rows, tm - rows), BF16_ROWS)
    h_scr[1 - tile % 2, pl.ds(r0, rows), :] = (
        _rms(xn_ref[pl.ds(r0, rows), :], nw_ref[...]).astype(BF16))


def _row_tile_specs(tm, d, n_tiles):
    cur = pl.BlockSpec((tm, d), lambda i, j: (i, 0))
    nxt = pl.BlockSpec((tm, d), lambda i, j: (jnp.minimum(i + 1, n_tiles - 1), 0))
    return cur, nxt


def _kv_kernel(mem_ref, nw_ref, w_ref, o_ref):
    h = _rms(mem_ref[...], nw_ref[...]).astype(BF16)
    o_ref[...] = _dot(h, w_ref[...]).astype(o_ref.dtype)


def _kv_proj(mem2d, norm_w, wkv):
    rows, d = mem2d.shape
    tn = 1024
    return pl.pallas_call(
        _kv_kernel,
        grid=(2, d // tn),
        in_specs=[
            pl.BlockSpec((rows, d), lambda s, n: (0, 0)),
            pl.BlockSpec((1, d), lambda s, n: (0, 0)),
            pl.BlockSpec((None, d, tn), lambda s, n: (s, 0, n)),
        ],
        out_specs=pl.BlockSpec((None, rows, tn), lambda s, n: (s, 0, n)),
        out_shape=jax.ShapeDtypeStruct((2, rows, d), BF16),
        compiler_params=pltpu.CompilerParams(
            dimension_semantics=("arbitrary", "arbitrary"),
            vmem_limit_bytes=VMEM_LIMIT_BYTES),
        name="kv_proj",
    )(mem2d, norm_w, wkv)


def _chunk_cumsum(x):
    pos = lax.broadcasted_iota(jnp.int32, x.shape, 0) % HGRN_CHUNK
    s = 1
    while s < HGRN_CHUNK:
        x = x + jnp.where(pos >= s, pltpu.roll(x, s, axis=0), 0.0)
        s *= 2
    return x


def _block_diag2(a, b):
    z = jnp.zeros(a.shape, a.dtype)
    return jnp.concatenate([jnp.concatenate([a, z], axis=1),
                            jnp.concatenate([z, b], axis=1)], axis=0)


def _intra_chunk_by_offset(qf, kk, b, v):
    pos = lax.broadcasted_iota(jnp.int32, qf.shape, 0) % HGRN_CHUNK
    ones = jnp.ones((LANES, LANES), BF16)
    head_sum = _block_diag2(ones, ones)

    def one_offset(d, acc):
        in_chunk = pos >= d
        decay = jnp.exp(jnp.where(in_chunk, b - pltpu.roll(b, d, axis=0), 0.0))
        w = jnp.where(in_chunk, qf * pltpu.roll(kk, d, axis=0) * decay, 0.0)
        return acc + _dot(w.astype(BF16), head_sum) * pltpu.roll(v, d, axis=0)

    return lax.fori_loop(0, HGRN_CHUNK, one_offset, jnp.zeros(qf.shape, F32))


def _mixer_kernel(x_ref, xn_ref, nw_ref, wq_ref, wf_ref, wi_ref, wg_ref, wb_ref, wc_ref, wh_ref,
                  wo_hgrn_ref, wo_conv_ref, lb_ref, hnw_ref, scw_ref, o_ref, bmin_ref,
                  h_scr, state_scr, carry_scr, *, tiles_per_seq, n_steps, layer, stable):
    i = pl.program_id(0)
    j = pl.program_id(1)
    tm = x_ref.shape[0]
    hd = LANES
    n_chunks = tm // HGRN_CHUNK

    @pl.when(j == 0)
    def _():
        o_ref[...] = x_ref[...]

    @pl.when(i % tiles_per_seq == 0)
    def _():
        state_scr[j] = jnp.zeros(state_scr.shape[1:], F32)
        carry_scr[j] = jnp.zeros(carry_scr.shape[1:], F32)

    h = _normalised_tile(x_ref, nw_ref, h_scr, i, j)
    f_pre = _dot(h, wf_ref[...])
    q = _dot(h, wq_ref[...])

    lb = jnp.sum(jax.nn.softmax(lb_ref[...], axis=0)[0:layer + 1], axis=0, keepdims=True)
    f = lb + (1.0 - lb) * jax.nn.sigmoid(f_pre)
    logf = jnp.log(f)
    kk = 1.0 - f
    qf = _silu(q)
    b = _chunk_cumsum(logf)
    v = _dot(h, wi_ref[...])
    vb = v.astype(BF16)
    g = _dot(h, wg_ref[...])
    _normalise_next_slice(xn_ref, nw_ref, h_scr, i, j, n_steps)
    h = h_scr[i % 2]
    cb = _dot(h, wb_ref[...])
    cc = _dot(h, wc_ref[...])
    qt = (qf * jnp.exp(b)).astype(BF16)

    blocks = [slice(r0, r0 + HGRN_BLOCK) for r0 in range(0, tm, HGRN_BLOCK)]
    heads = (slice(0, hd), slice(hd, 2 * hd))
    if not stable:
        kt = (kk * jnp.exp(-b)).astype(BF16)
        t_id = lax.broadcasted_iota(jnp.int32, (HGRN_BLOCK, HGRN_BLOCK), 0)
        s_id = lax.broadcasted_iota(jnp.int32, (HGRN_BLOCK, HGRN_BLOCK), 1)
        keep = ((t_id // HGRN_CHUNK) == (s_id // HGRN_CHUNK)) & (s_id <= t_id)
        scores = [[_dot_nt(qt[rows, cs], kt[rows, cs]) for cs in heads] for rows in blocks]

    chunks = [slice(c * HGRN_CHUNK, (c + 1) * HGRN_CHUNK) for c in range(n_chunks)]
    b_lasts = [b[rows][HGRN_CHUNK - 1:HGRN_CHUNK] for rows in chunks]
    upds = [_dot_tn(vb[rows], (kk[rows] * jnp.exp(bl - b[rows])).astype(BF16))
            for rows, bl in zip(chunks, b_lasts)]
    ch = _dot(h, wh_ref[...])

    if stable:
        intra = _intra_chunk_by_offset(qf, kk, b, v)
    else:
        intra = jnp.concatenate(
            [_dot(jnp.concatenate([jnp.where(keep, s, 0.0).astype(BF16) for s in sc], axis=1),
                  _block_diag2(vb[rows, :hd], vb[rows, hd:]))
             for rows, sc in zip(blocks, scores)], axis=0)

    st0 = state_scr[j, 0]
    st1 = state_scr[j, 1]
    starts = []
    for upd, bl in zip(upds, b_lasts):
        starts.append(_block_diag2(st0.astype(BF16), st1.astype(BF16)))
        decay = jnp.exp(bl)
        st0 = st0 * decay[:, :hd] + upd[:hd, :hd]
        st1 = st1 * decay[:, hd:] + upd[hd:, hd:]
    state_scr[j, 0] = st0
    state_scr[j, 1] = st1
    inter = [_dot_nt(qt[rows], start) for rows, start in zip(chunks, starts)]

    u = cc * ch
    prev8 = carry_scr[j]
    carry_scr[j] = u[tm - SUBLANES:]
    y = (cb * _causal_conv3(u, prev8, scw_ref[...])).astype(BF16)
    out = _dot(y, wo_conv_ref[...])

    hnw = hnw_ref[...]
    o = intra + jnp.concatenate(inter, axis=0)
    o = jnp.concatenate([_rms(o[:, :hd], hnw), _rms(o[:, hd:], hnw)], axis=1)
    o_gated = (o * _silu(g)).astype(BF16)

    o_ref[...] += out + _dot(o_gated, wo_hgrn_ref[...])

    b_min = functools.reduce(jnp.minimum, b_lasts)

    @pl.when(i == 0)
    def _():
        bmin_ref[j] = b_min

    @pl.when(i > 0)
    def _():
        bmin_ref[j] = jnp.minimum(bmin_ref[j], b_min)


def _mixer(x2d, norm_w, w_in, w_out, hgrn_lb, hgrn_norm_w, sconv_w, *, seq_len, layer, stable):
    m, d = x2d.shape
    tm = ROW_TILE
    gw = HEADS_PER_STEP * LANES
    nj = w_in.shape[1] // (N_PROJ_GROUPS * gw)
    n_tiles = m // tm
    x_cur, x_nxt = _row_tile_specs(tm, d, n_tiles)
    w_in_specs = [pl.BlockSpec((d, gw), lambda i, j, g=g: (0, g * nj + j))
                  for g in range(N_PROJ_GROUPS)]
    kern = functools.partial(_mixer_kernel, tiles_per_seq=seq_len // tm, n_steps=nj, layer=layer,
                             stable=stable)
    return pl.pallas_call(
        kern,
        grid=(n_tiles, nj),
        in_specs=[
            x_cur, x_nxt,
            pl.BlockSpec((1, d), lambda i, j: (0, 0)),
            *w_in_specs,
            pl.BlockSpec((gw, d), lambda i, j: (j, 0)),
            pl.BlockSpec((gw, d), lambda i, j: (nj + j, 0)),
            pl.BlockSpec((hgrn_lb.shape[0], gw), lambda i, j: (0, j)),
            pl.BlockSpec((1, LANES), lambda i, j: (0, 0)),
            pl.BlockSpec((CONV_TAPS, gw), lambda i, j: (0, j)),
        ],
        out_specs=[pl.BlockSpec((tm, d), lambda i, j: (i, 0)),
                   pl.BlockSpec((nj, 1, gw), lambda i, j: (0, 0, 0))],
        out_shape=[jax.ShapeDtypeStruct((m, d), F32),
                   jax.ShapeDtypeStruct((nj, 1, gw), F32)],
        scratch_shapes=[
            pltpu.VMEM((2, tm, d), BF16),
            pltpu.VMEM((nj, HEADS_PER_STEP, LANES, LANES), F32),
            pltpu.VMEM((nj, SUBLANES, gw), F32),
        ],
        compiler_params=pltpu.CompilerParams(
            dimension_semantics=("arbitrary", "arbitrary"),
            vmem_limit_bytes=VMEM_LIMIT_BYTES),
        name="hybrid_mixer_stable" if stable else "hybrid_mixer",
    )(x2d, x2d, norm_w, *([w_in] * N_PROJ_GROUPS), w_out, w_out, hgrn_lb, hgrn_norm_w, sconv_w)


def _attn_kernel(x_ref, xn_ref, nw_ref, wq_ref, k_ref, v_ref, wo_ref, *rest, head_dim, n_cast):
    cast_in, o_ref = rest[:n_cast], rest[n_cast]
    cast_out, h_scr = rest[n_cast + 1:2 * n_cast + 1], rest[2 * n_cast + 1]
    i = pl.program_id(0)
    h = _normalised_tile(x_ref, nw_ref, h_scr, i, 0)
    scale = head_dim ** -0.5
    cols = [slice(hh * head_dim, (hh + 1) * head_dim) for hh in range(MEM_HEADS)]
    half = MEM_HEADS // 2
    qs = [_dot(h, wq_ref[:, c]).astype(BF16) for c in cols[:half]]
    _normalise_next_slice(xn_ref, nw_ref, h_scr, i, 0, 1)
    for src, dst in zip(cast_in, cast_out):
        dst[...] = src[...].astype(dst.dtype)
    h = h_scr[i % 2]
    qs +=[_dot(h, wq_ref[:, c]).astype(BF16) for c in cols[half:]]
    ss = [_dot_nt(q, k_ref[:, c]) * scale for q, c in zip(qs, cols)]
    out = x_ref[...]
    for s, c in zip(ss, cols):
        e = jnp.exp(s - jnp.max(s, axis=-1, keepdims=True))
        pr = (e / jnp.sum(e, axis=-1, keepdims=True)).astype(BF16)
        out = out + _dot(_dot(pr, v_ref[:, c]).astype(BF16), wo_ref[c, :])
    o_ref[...] = out


def _cast_row_block(rows, n_steps):
    rb = pl.cdiv(pl.cdiv(rows, n_steps), BF16_ROWS) * BF16_ROWS
    while rows % rb:
        rb += BF16_ROWS
    return rb


def _attn(x2d, norm_w, wq, kv, wo, to_cast, *, seq_len, mem_len):
    m, d = x2d.shape
    tm = ROW_TILE
    tiles_per_seq = seq_len // tm
    n_tiles = m // tm
    resident = dict(pipeline_mode=pl.Buffered(1))

    def cast_spec(w):
        rb = _cast_row_block(w.shape[0], n_tiles)
        last = w.shape[0] // rb - 1
        return pl.BlockSpec((rb, w.shape[1]), lambda i: (jnp.minimum(i, last), 0))

    cast_specs = [cast_spec(w) for w in to_cast]
    kern = functools.partial(_attn_kernel, head_dim=d // MEM_HEADS, n_cast=len(to_cast))
    out, *casted = pl.pallas_call(
        kern,
        grid=(n_tiles,),
        in_specs=[
            pl.BlockSpec((tm, d), lambda i: (i, 0)),
            pl.BlockSpec((tm, d), lambda i: (jnp.minimum(i + 1, n_tiles - 1), 0)),
            pl.BlockSpec((1, d), lambda i: (0, 0)),
            pl.BlockSpec((d, d), lambda i: (0, 0), **resident),
            pl.BlockSpec((None, None, mem_len, d), lambda i: (0, i // tiles_per_seq, 0, 0)),
            pl.BlockSpec((None, None, mem_len, d), lambda i: (1, i // tiles_per_seq, 0, 0)),
            pl.BlockSpec((d, d), lambda i: (0, 0), **resident),
            *cast_specs,
        ],
        out_specs=[pl.BlockSpec((tm, d), lambda i: (i, 0)), *cast_specs],
        out_shape=[jax.ShapeDtypeStruct((m, d), F32),
                   *(jax.ShapeDtypeStruct(w.shape, BF16) for w in to_cast)],
        scratch_shapes=[pltpu.VMEM((2, tm, d), BF16)],
        compiler_params=pltpu.CompilerParams(
            dimension_semantics=("arbitrary",),
            vmem_limit_bytes=ATTN_VMEM_LIMIT_BYTES),
        name="mem_cross_attn",
    )(x2d, x2d, norm_w, wq, kv, kv, wo, *to_cast)
    return out, casted


def _ffn_kernel(x_ref, nw_ref, wg_ref, wu_ref, cw_ref, cb_ref, wd_ref, fnw_ref, o_ref,
                h_scr, carry_scr, *, tiles_per_seq, n_steps, final_norm):
    i = pl.program_id(0)
    f = pl.program_id(1)
    tm = x_ref.shape[0]

    @pl.when(f == 0)
    def _():
        xv = x_ref[...]
        h_scr[...] = _rms(xv, nw_ref[...]).astype(BF16)
        o_ref[...] = xv

    @pl.when(i % tiles_per_seq == 0)
    def _():
        carry_scr[f] = jnp.zeros(carry_scr.shape[1:], F32)

    h = h_scr[...]
    a = _dot(h, wg_ref[...])
    up = _dot(h, wu_ref[...])
    prev8 = carry_scr[f]
    carry_scr[f] = a[tm - SUBLANES:]
    act = _silu(_causal_conv3(a, prev8, cw_ref[...]) + cb_ref[...]) * up
    o_ref[...] += _dot(act.astype(BF16), wd_ref[...])

    if final_norm:
        @pl.when(f == n_steps - 1)
        def _():
            o_ref[...] = _rms(o_ref[...], fnw_ref[...])


def _ffn(x2d, norm_w, w_gate, w_up, conv_w, conv_b, w_down, final_w, *, seq_len, final_norm):
    m, d = x2d.shape
    d_ff = w_gate.shape[1]
    tm, tf = FFN_ROW_TILE, FFN_TILE
    nf = d_ff // tf
    kern = functools.partial(_ffn_kernel, tiles_per_seq=seq_len // tm, n_steps=nf,
                             final_norm=final_norm)
    return pl.pallas_call(
        kern,
        grid=(m // tm, nf),
        in_specs=[
            pl.BlockSpec((tm, d), lambda i, f: (i, 0)),
            pl.BlockSpec((1, d), lambda i, f: (0, 0)),
            pl.BlockSpec((d, tf), lambda i, f: (0, f)),
            pl.BlockSpec((d, tf), lambda i, f: (0, f)),
            pl.BlockSpec((CONV_TAPS, tf), lambda i, f: (0, f)),
            pl.BlockSpec((1, tf), lambda i, f: (0, f)),
            pl.BlockSpec((tf, d), lambda i, f: (f, 0)),
            pl.BlockSpec((1, d), lambda i, f: (0, 0)),
        ],
        out_specs=pl.BlockSpec((tm, d), lambda i, f: (i, 0)),
        out_shape=jax.ShapeDtypeStruct((m, d), F32),
        scratch_shapes=[
            pltpu.VMEM((tm, d), BF16),
            pltpu.VMEM((nf, SUBLANES, tf), F32),
        ],
        compiler_params=pltpu.CompilerParams(
            dimension_semantics=("arbitrary", "arbitrary"),
            vmem_limit_bytes=FFN_VMEM_LIMIT_BYTES),
        name="conv_ffn",
    )(x2d, norm_w, w_gate, w_up, conv_w, conv_b, w_down, final_w)


def kernel(x, mem, hgrn_lb, norm1_w, w_in, hgrn_norm_w, sconv_w, w_out, norm2_w, mem_norm_w,
           wq, wk, wv, wo, norm3_w, w_gate, w_up, ffn_conv_w, ffn_conv_b, w_down, final_norm_w):
    batch, seq_len, d = x.shape
    mem_len = mem.shape[1]
    depth = norm1_w.shape[0]
    assert seq_len % ROW_TILE == 0 and seq_len % FFN_ROW_TILE == 0
    assert ROW_TILE % HGRN_BLOCK == 0 and HGRN_BLOCK % HGRN_CHUNK == 0
    assert HEADS_PER_STEP == 2 and HGRN_HEADS % HEADS_PER_STEP == 0
    assert w_in.shape[2] == N_PROJ_GROUPS * HGRN_HEADS * LANES and w_gate.shape[2] % FFN_TILE == 0

    xs = x.reshape(batch * seq_len, d)
    mem2d = mem.reshape(batch * mem_len, d)
    row = lambda w: w.reshape(1, -1)
    for l in range(depth):
        mixer = functools.partial(
            _mixer, xs, row(norm1_w[l]), w_in[l].astype(BF16), w_out[l].astype(BF16), hgrn_lb,
            row(hgrn_norm_w[l]), sconv_w[l], seq_len=seq_len, layer=l)
        fast, b_min = mixer(stable=False)
        xs = lax.cond(jnp.min(b_min) < -HGRN_SAFE_LOG_DECAY,
                      lambda: mixer(stable=True)[0], lambda: fast)
        kv = _kv_proj(mem2d, row(mem_norm_w[l]), jnp.stack([wk[l], wv[l]]).astype(BF16))
        kv = kv.reshape(2, batch, mem_len, d)
        xs, (wg_b, wu_b, wd_b) = _attn(
            xs, row(norm2_w[l]), wq[l].astype(BF16), kv, wo[l].astype(BF16),
            (w_gate[l], w_up[l], w_down[l]), seq_len=seq_len, mem_len=mem_len)
        xs = _ffn(xs, row(norm3_w[l]), wg_b, wu_b, ffn_conv_w[l], row(ffn_conv_b[l]), wd_b,
                  row(final_norm_w), seq_len=seq_len, final_norm=(l == depth - 1))
    return xs.reshape(batch, seq_len, d)
```

```python
import functools

import jax
import jax.numpy as jnp
from jax import lax
from jax.experimental import pallas as pl
from jax.experimental.pallas import tpu as pltpu

F32 = jnp.float32
BF16 = jnp.bfloat16

EPS = 1e-6
HGRN_HEADS = 8
HGRN_CHUNK = 64
HGRN_SAFE_LOG_DECAY = 75.0
HGRN_BLOCK = 128
CONV_TAPS = 3
MEM_HEADS = 4
N_PROJ_GROUPS = 7
LANES = 128
SUBLANES = 8
BF16_ROWS = 16
KV_VMEM_LIMIT_BYTES = 56 * 1024 * 1024
MIXER_VMEM_LIMIT_BYTES = 60 * 1024 * 1024
FFN_VMEM_LIMIT_BYTES = 60 * 1024 * 1024
ATTN_VMEM_LIMIT_BYTES = 62 * 1024 * 1024

MIXER_ROW_TILE = 256
ATTN_ROW_TILE = 512
FFN_ROW_TILE = 1024
FFN_TILE = 512


def _rms(x, w):
    ms = jnp.mean(x * x, axis=-1, keepdims=True)
    return x * lax.rsqrt(ms + EPS) * w


def _silu(x):
    return x * jax.nn.sigmoid(x)


def _dot(a, b):
    return jnp.dot(a, b, preferred_element_type=F32)


def _dot_nt(a, b):
    return lax.dot_general(a, b, (((1,), (1,)), ((), ())), preferred_element_type=F32)


def _dot_tn(a, b):
    return lax.dot_general(a, b, (((0,), (0,)), ((), ())), preferred_element_type=F32)


def _shift_rows(cur, prev8, shift):
    rolled = pltpu.roll(cur, shift, axis=0)
    head = pltpu.roll(prev8, shift, axis=0)
    rid = lax.broadcasted_iota(jnp.int32, prev8.shape, 0)
    first = jnp.where(rid < shift, head, rolled[:SUBLANES])
    return jnp.concatenate([first, rolled[SUBLANES:]], axis=0)


def _causal_conv3(cur, prev8, w):
    return (w[0:1] * _shift_rows(cur, prev8, 2)
            + w[1:2] * _shift_rows(cur, prev8, 1)
            + w[2:3] * cur)


def _normalised_tile(x_ref, nw_ref, h_scr, tile):
    @pl.when(tile == 0)
    def _():
        h_scr[0] = _rms(x_ref[...], nw_ref[...]).astype(BF16)

    return h_scr[tile % 2]


def _normalise_next_tile(xn_ref, nw_ref, h_scr, tile):
    h_scr[1 - tile % 2] = _rms(xn_ref[...], nw_ref[...]).astype(BF16)
    return h_scr[tile % 2]


def _row_tile_specs(tm, d, n_tiles):
    cur = pl.BlockSpec((tm, d), lambda i: (i, 0))
    nxt = pl.BlockSpec((tm, d), lambda i: (jnp.minimum(i + 1, n_tiles - 1), 0))
    return cur, nxt


def _resident(shape):
    return pl.BlockSpec(shape, lambda i: (0,) * len(shape), pipeline_mode=pl.Buffered(1))


def _kv_kernel(mem_ref, nw_ref, w_ref, o_ref):
    h = _rms(mem_ref[...], nw_ref[...]).astype(BF16)
    o_ref[...] = _dot(h, w_ref[...]).astype(o_ref.dtype)


def _kv_proj(mem2d, norm_w, wkv):
    rows, d = mem2d.shape
    tn = 1024
    return pl.pallas_call(
        _kv_kernel,
        grid=(2, d // tn),
        in_specs=[
            pl.BlockSpec((rows, d), lambda s, n: (0, 0)),
            pl.BlockSpec((1, d), lambda s, n: (0, 0)),
            pl.BlockSpec((None, d, tn), lambda s, n: (s, 0, n)),
        ],
        out_specs=pl.BlockSpec((None, rows, tn), lambda s, n: (s, 0, n)),
        out_shape=jax.ShapeDtypeStruct((2, rows, d), BF16),
        compiler_params=pltpu.CompilerParams(
            dimension_semantics=("arbitrary", "arbitrary"),
            vmem_limit_bytes=KV_VMEM_LIMIT_BYTES),
        name="kv_proj",
    )(mem2d, norm_w, wkv)


def _chunk_cumsum(x):
    pos = lax.broadcasted_iota(jnp.int32, x.shape, 0) % HGRN_CHUNK
    s = 1
    while s < HGRN_CHUNK:
        x = x + jnp.where(pos >= s, pltpu.roll(x, s, axis=0), 0.0)
        s *= 2
    return x


def _block_diag2(a, b):
    z = jnp.zeros(a.shape, a.dtype)
    return jnp.concatenate([jnp.concatenate([a, z], axis=1),
                            jnp.concatenate([z, b], axis=1)], axis=0)


def _intra_chunk_by_offset(qf, kk, b, v):
    pos = lax.broadcasted_iota(jnp.int32, qf.shape, 0) % HGRN_CHUNK
    ones = jnp.ones((LANES, LANES), BF16)
    head_sum = _block_diag2(ones, ones)

    def one_offset(d, acc):
        in_chunk = pos >= d
        decay = jnp.exp(jnp.where(in_chunk, b - pltpu.roll(b, d, axis=0), 0.0))
        w = jnp.where(in_chunk, qf * pltpu.roll(kk, d, axis=0) * decay, 0.0)
        return acc + _dot(w.astype(BF16), head_sum) * pltpu.roll(v, d, axis=0)

    return lax.fori_loop(0, HGRN_CHUNK, one_offset, jnp.zeros(qf.shape, F32))


def _mixer_kernel(x_ref, xn_ref, nw_ref, win_ref, wout_ref, lb_ref, hnw_ref, scw_ref,
                  o_ref, bmin_ref, h_scr, state_scr, carry_scr, *, tiles_per_seq, layer, stable):
    i = pl.program_id(0)
    tm = x_ref.shape[0]
    hd = LANES
    gw = 2 * hd
    width = lb_ref.shape[1]
    n_pairs = width // gw
    n_chunks = tm // HGRN_CHUNK
    chunks = [slice(c * HGRN_CHUNK, (c + 1) * HGRN_CHUNK) for c in range(n_chunks)]
    blocks = [slice(r0, r0 + HGRN_BLOCK) for r0 in range(0, tm, HGRN_BLOCK)]

    @pl.when(i % tiles_per_seq == 0)
    def _():
        state_scr[...] = jnp.zeros(state_scr.shape, F32)
        carry_scr[...] = jnp.zeros(carry_scr.shape, F32)

    lb_all = jnp.sum(jax.nn.softmax(lb_ref[...], axis=0)[0:layer + 1], axis=0, keepdims=True)
    hnw = hnw_ref[...]
    scw = scw_ref[...]
    if not stable:
        t_id = lax.broadcasted_iota(jnp.int32, (HGRN_BLOCK, 2 * HGRN_BLOCK), 0)
        s_id = lax.broadcasted_iota(jnp.int32, (HGRN_BLOCK, 2 * HGRN_BLOCK), 1) % HGRN_BLOCK
        keep = ((t_id // HGRN_CHUNK) == (s_id // HGRN_CHUNK)) & (s_id <= t_id)

    h = _normalised_tile(x_ref, nw_ref, h_scr, i)
    gated, convs, b_mins = [], [], []
    for p in range(n_pairs):
        lanes = slice(p * gw, (p + 1) * gw)

        def proj(group):
            return _dot(h, win_ref[:, group * width + p * gw:group * width + (p + 1) * gw])

        f_pre = proj(1)
        q = proj(0)
        lb = lb_all[:, lanes]
        f = lb + (1.0 - lb) * jax.nn.sigmoid(f_pre)
        logf = jnp.log(f)
        kk = 1.0 - f
        qf = _silu(q)
        b = _chunk_cumsum(logf)
        v = proj(2)
        vb = v.astype(BF16)
        g = proj(3)
        if p == 0:
            h = _normalise_next_tile(xn_ref, nw_ref, h_scr, i)
        cb = proj(4)
        cc = proj(5)
        qt = (qf * jnp.exp(b)).astype(BF16)

        if not stable:
            kt = (kk * jnp.exp(-b)).astype(BF16)
            scores = [_dot_nt(qt[rows], _block_diag2(kt[rows, :hd], kt[rows, hd:]))
                      for rows in blocks]

        b_lasts = [b[rows][HGRN_CHUNK - 1:HGRN_CHUNK] for rows in chunks]
        upds = [_dot_tn(vb[rows], (kk[rows] * jnp.exp(bl - b[rows])).astype(BF16))
                for rows, bl in zip(chunks, b_lasts)]
        ch = proj(6)

        if stable:
            intra = _intra_chunk_by_offset(qf, kk, b, v)
        else:
            intra = jnp.concatenate(
                [_dot(jnp.where(keep, sc, 0.0).astype(BF16),
                      _block_diag2(vb[rows, :hd], vb[rows, hd:]))
                 for rows, sc in zip(blocks, scores)], axis=0)

        st0 = state_scr[2 * p]
        st1 = state_scr[2 * p + 1]
        starts = []
        for upd, bl in zip(upds, b_lasts):
            starts.append(_block_diag2(st0.astype(BF16), st1.astype(BF16)))
            decay = jnp.exp(bl)
            st0 = st0 * decay[:, :hd] + upd[:hd, :hd]
            st1 = st1 * decay[:, hd:] + upd[hd:, hd:]
        state_scr[2 * p] = st0
        state_scr[2 * p + 1] = st1
        inter = [_dot_nt(qt[rows], start) for rows, start in zip(chunks, starts)]

        u = cc * ch
        prev8 = carry_scr[p]
        carry_scr[p] = u[tm - SUBLANES:]
        convs.append((cb * _causal_conv3(u, prev8, scw[:, lanes])).astype(BF16))

        o = intra + jnp.concatenate(inter, axis=0)
        o = jnp.concatenate([_rms(o[:, :hd], hnw), _rms(o[:, hd:], hnw)], axis=1)
        gated.append((o * _silu(g)).astype(BF16))
        b_mins.append(functools.reduce(jnp.minimum, b_lasts))

    mixed = jnp.concatenate(gated + convs, axis=1)
    o_ref[...] = x_ref[...] + _dot(mixed, wout_ref[...])

    b_min = jnp.concatenate(b_mins, axis=1)

    @pl.when(i == 0)
    def _():
        bmin_ref[...] = b_min

    @pl.when(i > 0)
    def _():
        bmin_ref[...] = jnp.minimum(bmin_ref[...], b_min)


def _mixer(x2d, norm_w, w_in, w_out, hgrn_lb, hgrn_norm_w, sconv_w, *, seq_len, layer, stable):
    m, d = x2d.shape
    tm = MIXER_ROW_TILE
    width = hgrn_lb.shape[1]
    n_tiles = m // tm
    kern = functools.partial(_mixer_kernel, tiles_per_seq=seq_len // tm, layer=layer,
                             stable=stable)
    return pl.pallas_call(
        kern,
        grid=(n_tiles,),
        in_specs=[
            *_row_tile_specs(tm, d, n_tiles),
            _resident((1, d)),
            _resident(w_in.shape),
            _resident(w_out.shape),
            _resident(hgrn_lb.shape),
            _resident((1, LANES)),
            _resident(sconv_w.shape),
        ],
        out_specs=[pl.BlockSpec((tm, d), lambda i: (i, 0)),
                   pl.BlockSpec((1, width), lambda i: (0, 0))],
        out_shape=[jax.ShapeDtypeStruct((m, d), F32),
                   jax.ShapeDtypeStruct((1, width), F32)],
        scratch_shapes=[
            pltpu.VMEM((2, tm, d), BF16),
            pltpu.VMEM((HGRN_HEADS, LANES, LANES), F32),
            pltpu.VMEM((width // (2 * LANES), SUBLANES, 2 * LANES), F32),
        ],
        compiler_params=pltpu.CompilerParams(
            dimension_semantics=("arbitrary",),
            vmem_limit_bytes=MIXER_VMEM_LIMIT_BYTES),
        name="hybrid_mixer_stable" if stable else "hybrid_mixer",
    )(x2d, x2d, norm_w, w_in, w_out, hgrn_lb, hgrn_norm_w, sconv_w)


def _attn_kernel(x_ref, xn_ref, nw_ref, wq_ref, k_ref, v_ref, wo_ref, *rest, head_dim, n_cast):
    cast_in, o_ref = rest[:n_cast], rest[n_cast]
    cast_out, h_scr = rest[n_cast + 1:2 * n_cast + 1], rest[2 * n_cast + 1]
    i = pl.program_id(0)
    h = _normalised_tile(x_ref, nw_ref, h_scr, i)
    scale = head_dim ** -0.5
    cols = [slice(hh * head_dim, (hh + 1) * head_dim) for hh in range(MEM_HEADS)]
    half = MEM_HEADS // 2
    qs = [_dot(h, wq_ref[:, c]).astype(BF16) for c in cols[:half]]
    for src, dst in zip(cast_in, cast_out):
        dst[...] = src[...].astype(dst.dtype)
    h = _normalise_next_tile(xn_ref, nw_ref, h_scr, i)
    qs += [_dot(h, wq_ref[:, c]).astype(BF16) for c in cols[half:]]
    ss = [_dot_nt(q, k_ref[:, c]) * scale for q, c in zip(qs, cols)]
    out = x_ref[...]
    for s, c in zip(ss, cols):
        e = jnp.exp(s - jnp.max(s, axis=-1, keepdims=True))
        pr = (e / jnp.sum(e, axis=-1, keepdims=True)).astype(BF16)
        out = out + _dot(_dot(pr, v_ref[:, c]).astype(BF16), wo_ref[c, :])
    o_ref[...] = out


def _cast_row_block(rows, n_steps):
    rb = pl.cdiv(pl.cdiv(rows, n_steps), BF16_ROWS) * BF16_ROWS
    while rows % rb:
        rb += BF16_ROWS
    return rb


def _attn(x2d, norm_w, wq, kv, wo, to_cast, *, seq_len, mem_len):
    m, d = x2d.shape
    tm = ATTN_ROW_TILE
    tiles_per_seq = seq_len // tm
    n_tiles = m // tm

    def cast_spec(w):
        rb = _cast_row_block(w.shape[0], n_tiles)
        last = w.shape[0] // rb - 1
        return pl.BlockSpec((rb, w.shape[1]), lambda i: (jnp.minimum(i, last), 0))

    cast_specs = [cast_spec(w) for w in to_cast]
    kern = functools.partial(_attn_kernel, head_dim=d // MEM_HEADS, n_cast=len(to_cast))
    out, *casted = pl.pallas_call(
        kern,
        grid=(n_tiles,),
        in_specs=[
            *_row_tile_specs(tm, d, n_tiles),
            _resident((1, d)),
            _resident((d, d)),
            pl.BlockSpec((None, None, mem_len, d), lambda i: (0, i // tiles_per_seq, 0, 0)),
            pl.BlockSpec((None, None, mem_len, d), lambda i: (1, i // tiles_per_seq, 0, 0)),
            _resident((d, d)),
            *cast_specs,
        ],
        out_specs=[pl.BlockSpec((tm, d), lambda i: (i, 0)), *cast_specs],
        out_shape=[jax.ShapeDtypeStruct((m, d), F32),
                   *(jax.ShapeDtypeStruct(w.shape, BF16) for w in to_cast)],
        scratch_shapes=[pltpu.VMEM((2, tm, d), BF16)],
        compiler_params=pltpu.CompilerParams(
            dimension_semantics=("arbitrary",),
            vmem_limit_bytes=ATTN_VMEM_LIMIT_BYTES),
        name="mem_cross_attn",
    )(x2d, x2d, norm_w, wq, kv, kv, wo, *to_cast)
    return out, casted


def _ffn_kernel(x_ref, nw_ref, wg_ref, wu_ref, cw_ref, cb_ref, wd_ref, fnw_ref, o_ref,
                h_scr, carry_scr, *, tiles_per_seq, n_steps, final_norm):
    i = pl.program_id(0)
    f = pl.program_id(1)
    tm = x_ref.shape[0]

    @pl.when(f == 0)
    def _():
        xv = x_ref[...]
        h_scr[...] = _rms(xv, nw_ref[...]).astype(BF16)
        o_ref[...] = xv

    @pl.when(i % tiles_per_seq == 0)
    def _():
        carry_scr[f] = jnp.zeros(carry_scr.shape[1:], F32)

    h = h_scr[...]
    a = _dot(h, wg_ref[...])
    up = _dot(h, wu_ref[...])
    prev8 = carry_scr[f]
    carry_scr[f] = a[tm - SUBLANES:]
    act = _silu(_causal_conv3(a, prev8, cw_ref[...]) + cb_ref[...]) * up
    o_ref[...] += _dot(act.astype(BF16), wd_ref[...])

    if final_norm:
        @pl.when(f == n_steps - 1)
        def _():
            o_ref[...] = _rms(o_ref[...], fnw_ref[...])


def _ffn(x2d, norm_w, w_gate, w_up, conv_w, conv_b, w_down, final_w, *, seq_len, final_norm):
    m, d = x2d.shape
    d_ff = w_gate.shape[1]
    tm, tf = FFN_ROW_TILE, FFN_TILE
    nf = d_ff // tf
    kern = functools.partial(_ffn_kernel, tiles_per_seq=seq_len // tm, n_steps=nf,
                             final_norm=final_norm)
    return pl.pallas_call(
        kern,
        grid=(m // tm, nf),
        in_specs=[
            pl.BlockSpec((tm, d), lambda i, f: (i, 0)),
            pl.BlockSpec((1, d), lambda i, f: (0, 0)),
            pl.BlockSpec((d, tf), lambda i, f: (0, f)),
            pl.BlockSpec((d, tf), lambda i, f: (0, f)),
            pl.BlockSpec((CONV_TAPS, tf), lambda i, f: (0, f)),
            pl.BlockSpec((1, tf), lambda i, f: (0, f)),
            pl.BlockSpec((tf, d), lambda i, f: (f, 0)),
            pl.BlockSpec((1, d), lambda i, f: (0, 0)),
        ],
        out_specs=pl.BlockSpec((tm, d), lambda i, f: (i, 0)),
        out_shape=jax.ShapeDtypeStruct((m, d), F32),
        scratch_shapes=[
            pltpu.VMEM((tm, d), BF16),
            pltpu.VMEM((nf, SUBLANES, tf), F32),
        ],
        compiler_params=pltpu.CompilerParams(
            dimension_semantics=("arbitrary", "arbitrary"),
            vmem_limit_bytes=FFN_VMEM_LIMIT_BYTES),
        name="conv_ffn",
    )(x2d, norm_w, w_gate, w_up, conv_w, conv_b, w_down, final_w)


def kernel(x, mem, hgrn_lb, norm1_w, w_in, hgrn_norm_w, sconv_w, w_out, norm2_w, mem_norm_w,
           wq, wk, wv, wo, norm3_w, w_gate, w_up, ffn_conv_w, ffn_conv_b, w_down, final_norm_w):
    batch, seq_len, d = x.shape
    mem_len = mem.shape[1]
    depth = norm1_w.shape[0]
    assert all(seq_len % t == 0 for t in (MIXER_ROW_TILE, ATTN_ROW_TILE, FFN_ROW_TILE))
    assert MIXER_ROW_TILE % HGRN_BLOCK == 0 and HGRN_BLOCK % HGRN_CHUNK == 0
    assert hgrn_lb.shape[1] == HGRN_HEADS * LANES and HGRN_HEADS % 2 == 0
    assert w_in.shape[2] == N_PROJ_GROUPS * HGRN_HEADS * LANES and w_gate.shape[2] % FFN_TILE == 0

    xs = x.reshape(batch * seq_len, d)
    mem2d = mem.reshape(batch * mem_len, d)
    row = lambda w: w.reshape(1, -1)
    for l in range(depth):
        mixer = functools.partial(
            _mixer, xs, row(norm1_w[l]), w_in[l].astype(BF16), w_out[l].astype(BF16), hgrn_lb,
            row(hgrn_norm_w[l]), sconv_w[l], seq_len=seq_len, layer=l)
        fast, b_min = mixer(stable=False)
        xs = lax.cond(jnp.min(b_min) < -HGRN_SAFE_LOG_DECAY,
                      lambda: mixer(stable=True)[0], lambda: fast)
        kv = _kv_proj(mem2d, row(mem_norm_w[l]), jnp.stack([wk[l], wv[l]]).astype(BF16))
        kv = kv.reshape(2, batch, mem_len, d)
        xs, (wg_b, wu_b, wd_b) = _attn(
            xs, row(norm2_w[l]), wq[l].astype(BF16), kv, wo[l].astype(BF16),
            (w_gate[l], w_up[l], w_down[l]), seq_len=seq_len, mem_len=mem_len)
        xs = _ffn(xs, row(norm3_w[l]), wg_b, wu_b, ffn_conv_w[l], row(ffn_conv_b[l]), wd_b,
                  row(final_norm_w), seq_len=seq_len, final_norm=(l == depth - 1))
    return xs.reshape(batch, seq_len, d)
```

```python
import functools

import jax
import jax.numpy as jnp
from jax import lax
from jax.experimental import pallas as pl
from jax.experimental.pallas import tpu as pltpu

F32 = jnp.float32
BF16 = jnp.bfloat16

EPS = 1e-6
HGRN_HEADS = 8
HGRN_CHUNK = 64
HGRN_SAFE_LOG_DECAY = 75.0
HGRN_BLOCK = 128
CONV_TAPS = 3
MEM_HEADS = 4
N_PROJ_GROUPS = 7
LANES = 128
SUBLANES = 8
BF16_ROWS = 16
KV_VMEM_LIMIT_BYTES = 56 * 1024 * 1024
MIXER_VMEM_LIMIT_BYTES = 60 * 1024 * 1024
FFN_VMEM_LIMIT_BYTES = 60 * 1024 * 1024
ATTN_VMEM_LIMIT_BYTES = 62 * 1024 * 1024

MIXER_ROW_TILE = 256
ATTN_ROW_TILE = 512
FFN_ROW_TILE = 1024
FFN_TILE = 512


def _rms(x, w):
    ms = jnp.mean(x * x, axis=-1, keepdims=True)
    return x * lax.rsqrt(ms + EPS) * w


def _silu(x):
    return x * jax.nn.sigmoid(x)


def _dot(a, b):
    return jnp.dot(a, b, preferred_element_type=F32)


def _dot_nt(a, b):
    return lax.dot_general(a, b, (((1,), (1,)), ((), ())), preferred_element_type=F32)


def _dot_tn(a, b):
    return lax.dot_general(a, b, (((0,), (0,)), ((), ())), preferred_element_type=F32)


def _shift_rows(cur, prev8, shift):
    rolled = pltpu.roll(cur, shift, axis=0)
    head = pltpu.roll(prev8, shift, axis=0)
    rid = lax.broadcasted_iota(jnp.int32, prev8.shape, 0)
    first = jnp.where(rid < shift, head, rolled[:SUBLANES])
    return jnp.concatenate([first, rolled[SUBLANES:]], axis=0)


def _causal_conv3(cur, prev8, w):
    return (w[0:1] * _shift_rows(cur, prev8, 2)
            + w[1:2] * _shift_rows(cur, prev8, 1)
            + w[2:3] * cur)


def _normalised_tile(x_ref, nw_ref, h_scr):
    h_scr[...] = _rms(x_ref[...], nw_ref[...]).astype(BF16)
    return h_scr[...]


def _resident(shape):
    return pl.BlockSpec(shape, lambda i: (0,) * len(shape), pipeline_mode=pl.Buffered(1))


def _kv_kernel(mem_ref, nw_ref, w_ref, o_ref):
    h = _rms(mem_ref[...], nw_ref[...]).astype(BF16)
    o_ref[...] = _dot(h, w_ref[...]).astype(o_ref.dtype)


def _kv_proj(mem2d, norm_w, wkv):
    rows, d = mem2d.shape
    tn = 1024
    return pl.pallas_call(
        _kv_kernel,
        grid=(2, d // tn),
        in_specs=[
            pl.BlockSpec((rows, d), lambda s, n: (0, 0)),
            pl.BlockSpec((1, d), lambda s, n: (0, 0)),
            pl.BlockSpec((None, d, tn), lambda s, n: (s, 0, n)),
        ],
        out_specs=pl.BlockSpec((None, rows, tn), lambda s, n: (s, 0, n)),
        out_shape=jax.ShapeDtypeStruct((2, rows, d), BF16),
        compiler_params=pltpu.CompilerParams(
            dimension_semantics=("arbitrary", "arbitrary"),
            vmem_limit_bytes=KV_VMEM_LIMIT_BYTES),
        name="kv_proj",
    )(mem2d, norm_w, wkv)


def _chunk_cumsum(x):
    pos = lax.broadcasted_iota(jnp.int32, x.shape, 0) % HGRN_CHUNK
    s = 1
    while s < HGRN_CHUNK:
        x = x + jnp.where(pos >= s, pltpu.roll(x, s, axis=0), 0.0)
        s *= 2
    return x


def _block_diag2(a, b):
    z = jnp.zeros(a.shape, a.dtype)
    return jnp.concatenate([jnp.concatenate([a, z], axis=1),
                            jnp.concatenate([z, b], axis=1)], axis=0)


def _intra_chunk_by_offset(qf, kk, b, v):
    pos = lax.broadcasted_iota(jnp.int32, qf.shape, 0) % HGRN_CHUNK
    ones = jnp.ones((LANES, LANES), BF16)
    head_sum = _block_diag2(ones, ones)

    def one_offset(d, acc):
        in_chunk = pos >= d
        decay = jnp.exp(jnp.where(in_chunk, b - pltpu.roll(b, d, axis=0), 0.0))
        w = jnp.where(in_chunk, qf * pltpu.roll(kk, d, axis=0) * decay, 0.0)
        return acc + _dot(w.astype(BF16), head_sum) * pltpu.roll(v, d, axis=0)

    return lax.fori_loop(0, HGRN_CHUNK, one_offset, jnp.zeros(qf.shape, F32))


def _mixer_kernel(x_ref, nw_ref, win_ref, wout_ref, lb_ref, hnw_ref, scw_ref,
                  o_ref, bmin_ref, h_scr, state_scr, carry_scr, *, tiles_per_seq, layer, stable):
    i = pl.program_id(0)
    tm = x_ref.shape[0]
    hd = LANES
    gw = 2 * hd
    width = lb_ref.shape[1]
    n_pairs = width // gw
    n_chunks = tm // HGRN_CHUNK
    chunks = [slice(c * HGRN_CHUNK, (c + 1) * HGRN_CHUNK) for c in range(n_chunks)]
    blocks = [slice(r0, r0 + HGRN_BLOCK) for r0 in range(0, tm, HGRN_BLOCK)]

    @pl.when(i % tiles_per_seq == 0)
    def _():
        state_scr[...] = jnp.zeros(state_scr.shape, F32)
        carry_scr[...] = jnp.zeros(carry_scr.shape, F32)

    lb_all = jnp.sum(jax.nn.softmax(lb_ref[...], axis=0)[0:layer + 1], axis=0, keepdims=True)
    hnw = hnw_ref[...]
    scw = scw_ref[...]
    if not stable:
        t_id = lax.broadcasted_iota(jnp.int32, (HGRN_BLOCK, 2 * HGRN_BLOCK), 0)
        s_id = lax.broadcasted_iota(jnp.int32, (HGRN_BLOCK, 2 * HGRN_BLOCK), 1) % HGRN_BLOCK
        keep = ((t_id // HGRN_CHUNK) == (s_id // HGRN_CHUNK)) & (s_id <= t_id)

    h = _normalised_tile(x_ref, nw_ref, h_scr)
    gated, convs, b_mins = [], [], []
    for p in range(n_pairs):
        lanes = slice(p * gw, (p + 1) * gw)

        def proj(group):
            return _dot(h, win_ref[:, group * width + p * gw:group * width + (p + 1) * gw])

        f_pre = proj(1)
        q = proj(0)
        lb = lb_all[:, lanes]
        f = lb + (1.0 - lb) * jax.nn.sigmoid(f_pre)
        logf = jnp.log(f)
        kk = 1.0 - f
        qf = _silu(q)
        b = _chunk_cumsum(logf)
        v = proj(2)
        vb = v.astype(BF16)
        g = proj(3)
        cb = proj(4)
        cc = proj(5)
        qt = (qf * jnp.exp(b)).astype(BF16)

        if not stable:
            kt = (kk * jnp.exp(-b)).astype(BF16)
            scores = [_dot_nt(qt[rows], _block_diag2(kt[rows, :hd], kt[rows, hd:]))
                      for rows in blocks]

        b_lasts = [b[rows][HGRN_CHUNK - 1:HGRN_CHUNK] for rows in chunks]
        upds = [_dot_tn(vb[rows], (kk[rows] * jnp.exp(bl - b[rows])).astype(BF16))
                for rows, bl in zip(chunks, b_lasts)]
        ch = proj(6)

        if stable:
            intra = _intra_chunk_by_offset(qf, kk, b, v)
        else:
            intra = jnp.concatenate(
                [_dot(jnp.where(keep, sc, 0.0).astype(BF16),
                      _block_diag2(vb[rows, :hd], vb[rows, hd:]))
                 for rows, sc in zip(blocks, scores)], axis=0)

        st0 = state_scr[2 * p]
        st1 = state_scr[2 * p + 1]
        starts = []
        for upd, bl in zip(upds, b_lasts):
            starts.append(_block_diag2(st0.astype(BF16), st1.astype(BF16)))
            decay = jnp.exp(bl)
            st0 = st0 * decay[:, :hd] + upd[:hd, :hd]
            st1 = st1 * decay[:, hd:] + upd[hd:, hd:]
        state_scr[2 * p] = st0
        state_scr[2 * p + 1] = st1
        inter = [_dot_nt(qt[rows], start) for rows, start in zip(chunks, starts)]

        u = cc * ch
        prev8 = carry_scr[p]
        carry_scr[p] = u[tm - SUBLANES:]
        convs.append((cb * _causal_conv3(u, prev8, scw[:, lanes])).astype(BF16))

        o = intra + jnp.concatenate(inter, axis=0)
        o = jnp.concatenate([_rms(o[:, :hd], hnw), _rms(o[:, hd:], hnw)], axis=1)
        gated.append((o * _silu(g)).astype(BF16))
        b_mins.append(functools.reduce(jnp.minimum, b_lasts))

    mixed = jnp.concatenate(gated + convs, axis=1)
    o_ref[...] = x_ref[...] + _dot(mixed, wout_ref[...])

    b_min = jnp.concatenate(b_mins, axis=1)

    @pl.when(i == 0)
    def _():
        bmin_ref[...] = b_min

    @pl.when(i > 0)
    def _():
        bmin_ref[...] = jnp.minimum(bmin_ref[...], b_min)


def _mixer(x2d, norm_w, w_in, w_out, hgrn_lb, hgrn_norm_w, sconv_w, *, seq_len, layer, stable):
    m, d = x2d.shape
    tm = MIXER_ROW_TILE
    width = hgrn_lb.shape[1]
    n_tiles = m // tm
    kern = functools.partial(_mixer_kernel, tiles_per_seq=seq_len // tm, layer=layer,
                             stable=stable)
    return pl.pallas_call(
        kern,
        grid=(n_tiles,),
        in_specs=[
            pl.BlockSpec((tm, d), lambda i: (i, 0)),
            _resident((1, d)),
            _resident(w_in.shape),
            _resident(w_out.shape),
            _resident(hgrn_lb.shape),
            _resident((1, LANES)),
            _resident(sconv_w.shape),
        ],
        out_specs=[pl.BlockSpec((tm, d), lambda i: (i, 0)),
                   pl.BlockSpec((1, width), lambda i: (0, 0))],
        out_shape=[jax.ShapeDtypeStruct((m, d), F32),
                   jax.ShapeDtypeStruct((1, width), F32)],
        scratch_shapes=[
            pltpu.VMEM((tm, d), BF16),
            pltpu.VMEM((HGRN_HEADS, LANES, LANES), F32),
            pltpu.VMEM((width // (2 * LANES), SUBLANES, 2 * LANES), F32),
        ],
        compiler_params=pltpu.CompilerParams(
            dimension_semantics=("arbitrary",),
            vmem_limit_bytes=MIXER_VMEM_LIMIT_BYTES),
        name="hybrid_mixer_stable" if stable else "hybrid_mixer",
    )(x2d, norm_w, w_in, w_out, hgrn_lb, hgrn_norm_w, sconv_w)


def _attn_kernel(x_ref, nw_ref, wq_ref, k_ref, v_ref, wo_ref, *rest, head_dim, n_cast):
    cast_in, o_ref = rest[:n_cast], rest[n_cast]
    cast_out, h_scr = rest[n_cast + 1:2 * n_cast + 1], rest[2 * n_cast + 1]
    h = _normalised_tile(x_ref, nw_ref, h_scr)
    scale = head_dim ** -0.5
    cols = [slice(hh * head_dim, (hh + 1) * head_dim) for hh in range(MEM_HEADS)]
    qs = [_dot(h, wq_ref[:, c]).astype(BF16) for c in cols]
    for src, dst in zip(cast_in, cast_out):
        dst[...] = src[...].astype(dst.dtype)
    ss = [_dot_nt(q, k_ref[:, c]) * scale for q, c in zip(qs, cols)]
    out = x_ref[...]
    for s, c in zip(ss, cols):
        e = jnp.exp(s - jnp.max(s, axis=-1, keepdims=True))
        pr = (e / jnp.sum(e, axis=-1, keepdims=True)).astype(BF16)
        out = out + _dot(_dot(pr, v_ref[:, c]).astype(BF16), wo_ref[c, :])
    o_ref[...] = out


def _cast_row_block(rows, n_steps):
    rb = pl.cdiv(pl.cdiv(rows, n_steps), BF16_ROWS) * BF16_ROWS
    while rows % rb:
        rb += BF16_ROWS
    return rb


def _attn(x2d, norm_w, wq, kv, wo, to_cast, *, seq_len, mem_len):
    m, d = x2d.shape
    tm = ATTN_ROW_TILE
    tiles_per_seq = seq_len // tm
    n_tiles = m // tm

    def cast_spec(w):
        rb = _cast_row_block(w.shape[0], n_tiles)
        last = w.shape[0] // rb - 1
        return pl.BlockSpec((rb, w.shape[1]), lambda i: (jnp.minimum(i, last), 0))

    cast_specs = [cast_spec(w) for w in to_cast]
    kern = functools.partial(_attn_kernel, head_dim=d // MEM_HEADS, n_cast=len(to_cast))
    out, *casted = pl.pallas_call(
        kern,
        grid=(n_tiles,),
        in_specs=[
            pl.BlockSpec((tm, d), lambda i: (i, 0)),
            _resident((1, d)),
            _resident((d, d)),
            pl.BlockSpec((None, None, mem_len, d), lambda i: (0, i // tiles_per_seq, 0, 0)),
            pl.BlockSpec((None, None, mem_len, d), lambda i: (1, i // tiles_per_seq, 0, 0)),
            _resident((d, d)),
            *cast_specs,
        ],
        out_specs=[pl.BlockSpec((tm, d), lambda i: (i, 0)), *cast_specs],
        out_shape=[jax.ShapeDtypeStruct((m, d), F32),
                   *(jax.ShapeDtypeStruct(w.shape, BF16) for w in to_cast)],
        scratch_shapes=[pltpu.VMEM((tm, d), BF16)],
        compiler_params=pltpu.CompilerParams(
            dimension_semantics=("arbitrary",),
            vmem_limit_bytes=ATTN_VMEM_LIMIT_BYTES),
        name="mem_cross_attn",
    )(x2d, norm_w, wq, kv, kv, wo, *to_cast)
    return out, casted


def _ffn_kernel(x_ref, nw_ref, wg_ref, wu_ref, cw_ref, cb_ref, wd_ref, fnw_ref, o_ref,
                h_scr, carry_scr, *, tiles_per_seq, n_steps, final_norm):
    i = pl.program_id(0)
    f = pl.program_id(1)
    tm = x_ref.shape[0]

    @pl.when(f == 0)
    def _():
        xv = x_ref[...]
        h_scr[...] = _rms(xv, nw_ref[...]).astype(BF16)
        o_ref[...] = xv

    @pl.when(i % tiles_per_seq == 0)
    def _():
        carry_scr[f] = jnp.zeros(carry_scr.shape[1:], F32)

    h = h_scr[...]
    a = _dot(h, wg_ref[...])
    up = _dot(h, wu_ref[...])
    prev8 = carry_scr[f]
    carry_scr[f] = a[tm - SUBLANES:]
    act = _silu(_causal_conv3(a, prev8, cw_ref[...]) + cb_ref[...]) * up
    o_ref[...] += _dot(act.astype(BF16), wd_ref[...])

    if final_norm:
        @pl.when(f == n_steps - 1)
        def _():
            o_ref[...] = _rms(o_ref[...], fnw_ref[...])


def _ffn(x2d, norm_w, w_gate, w_up, conv_w, conv_b, w_down, final_w, *, seq_len, final_norm):
    m, d = x2d.shape
    d_ff = w_gate.shape[1]
    tm, tf = FFN_ROW_TILE, FFN_TILE
    nf = d_ff // tf
    kern = functools.partial(_ffn_kernel, tiles_per_seq=seq_len // tm, n_steps=nf,
                             final_norm=final_norm)
    return pl.pallas_call(
        kern,
        grid=(m // tm, nf),
        in_specs=[
            pl.BlockSpec((tm, d), lambda i, f: (i, 0)),
            pl.BlockSpec((1, d), lambda i, f: (0, 0)),
            pl.BlockSpec((d, tf), lambda i, f: (0, f)),
            pl.BlockSpec((d, tf), lambda i, f: (0, f)),
            pl.BlockSpec((CONV_TAPS, tf), lambda i, f: (0, f)),
            pl.BlockSpec((1, tf), lambda i, f: (0, f)),
            pl.BlockSpec((tf, d), lambda i, f: (f, 0)),
            pl.BlockSpec((1, d), lambda i, f: (0, 0)),
        ],
        out_specs=pl.BlockSpec((tm, d), lambda i, f: (i, 0)),
        out_shape=jax.ShapeDtypeStruct((m, d), F32),
        scratch_shapes=[
            pltpu.VMEM((tm, d), BF16),
            pltpu.VMEM((nf, SUBLANES, tf), F32),
        ],
        compiler_params=pltpu.CompilerParams(
            dimension_semantics=("arbitrary", "arbitrary"),
            vmem_limit_bytes=FFN_VMEM_LIMIT_BYTES),
        name="conv_ffn",
    )(x2d, norm_w, w_gate, w_up, conv_w, conv_b, w_down, final_w)


def kernel(x, mem, hgrn_lb, norm1_w, w_in, hgrn_norm_w, sconv_w, w_out, norm2_w, mem_norm_w,
           wq, wk, wv, wo, norm3_w, w_gate, w_up, ffn_conv_w, ffn_conv_b, w_down, final_norm_w):
    batch, seq_len, d = x.shape
    mem_len = mem.shape[1]
    depth = norm1_w.shape[0]
    assert all(seq_len % t == 0 for t in (MIXER_ROW_TILE, ATTN_ROW_TILE, FFN_ROW_TILE))
    assert MIXER_ROW_TILE % HGRN_BLOCK == 0 and HGRN_BLOCK % HGRN_CHUNK == 0
    assert hgrn_lb.shape[1] == HGRN_HEADS * LANES and HGRN_HEADS % 2 == 0
    assert w_in.shape[2] == N_PROJ_GROUPS * HGRN_HEADS * LANES and w_gate.shape[2] % FFN_TILE == 0

    xs = x.reshape(batch * seq_len, d)
    mem2d = mem.reshape(batch * mem_len, d)
    row = lambda w: w.reshape(1, -1)
    for l in range(depth):
        mixer = functools.partial(
            _mixer, xs, row(norm1_w[l]), w_in[l].astype(BF16), w_out[l].astype(BF16), hgrn_lb,
            row(hgrn_norm_w[l]), sconv_w[l], seq_len=seq_len, layer=l)
        fast, b_min = mixer(stable=False)
        xs = lax.cond(jnp.min(b_min) < -HGRN_SAFE_LOG_DECAY,
                      lambda: mixer(stable=True)[0], lambda: fast)
        kv = _kv_proj(mem2d, row(mem_norm_w[l]), jnp.stack([wk[l], wv[l]]).astype(BF16))
        kv = kv.reshape(2, batch, mem_len, d)
        xs, (wg_b, wu_b, wd_b) = _attn(
            xs, row(norm2_w[l]), wq[l].astype(BF16), kv, wo[l].astype(BF16),
            (w_gate[l], w_up[l], w_down[l]), seq_len=seq_len, mem_len=mem_len)
        xs = _ffn(xs, row(norm3_w[l]), wg_b, wu_b, ffn_conv_w[l], row(ffn_conv_b[l]), wd_b,
                  row(final_norm_w), seq_len=seq_len, final_norm=(l == depth - 1))
    return xs.reshape(batch, seq_len, d)
```

```python
import functools

import jax
import jax.numpy as jnp
from jax import lax
from jax.experimental import pallas as pl
from jax.experimental.pallas import tpu as pltpu

F32 = jnp.float32
BF16 = jnp.bfloat16

EPS = 1e-6
HGRN_HEADS = 8
HGRN_CHUNK = 64
HGRN_SAFE_LOG_DECAY = 75.0
HGRN_BLOCK = 128
CONV_TAPS = 3
MEM_HEADS = 4
N_PROJ_GROUPS = 7
LANES = 128
SUBLANES = 8
BF16_ROWS = 16
KV_VMEM_LIMIT_BYTES = 56 * 1024 * 1024
MIXER_VMEM_LIMIT_BYTES = 60 * 1024 * 1024
FFN_VMEM_LIMIT_BYTES = 60 * 1024 * 1024
ATTN_VMEM_LIMIT_BYTES = 62 * 1024 * 1024

MIXER_ROW_TILE = 256
ATTN_ROW_TILE = 512
FFN_ROW_TILE = 1024
FFN_TILE = 512


def _rms(x, w):
    ms = jnp.mean(x * x, axis=-1, keepdims=True)
    return x * lax.rsqrt(ms + EPS) * w


def _silu(x):
    return x * jax.nn.sigmoid(x)


def _dot(a, b):
    return jnp.dot(a, b, preferred_element_type=F32)


def _dot_nt(a, b):
    return lax.dot_general(a, b, (((1,), (1,)), ((), ())), preferred_element_type=F32)


def _dot_tn(a, b):
    return lax.dot_general(a, b, (((0,), (0,)), ((), ())), preferred_element_type=F32)


def _shift_rows(cur, prev8, shift):
    rolled = pltpu.roll(cur, shift, axis=0)
    head = pltpu.roll(prev8, shift, axis=0)
    rid = lax.broadcasted_iota(jnp.int32, prev8.shape, 0)
    first = jnp.where(rid < shift, head, rolled[:SUBLANES])
    return jnp.concatenate([first, rolled[SUBLANES:]], axis=0)


def _causal_conv3(cur, prev8, w):
    return (w[0:1] * _shift_rows(cur, prev8, 2)
            + w[1:2] * _shift_rows(cur, prev8, 1)
            + w[2:3] * cur)


def _normalised_tile(x_ref, nw_ref, h_scr):
    h_scr[...] = _rms(x_ref[...], nw_ref[...]).astype(BF16)
    return h_scr[...]


def _cast_specs(to_cast, n_steps):
    def spec(w):
        rows = w.shape[0]
        rb = pl.cdiv(pl.cdiv(rows, n_steps), BF16_ROWS) * BF16_ROWS
        while rows % rb:
            rb += BF16_ROWS
        last = rows // rb - 1
        return pl.BlockSpec((rb, w.shape[1]), lambda i: (jnp.minimum(i, last), 0))

    return [spec(w) for w in to_cast]


def _cast_blocks(src_refs, dst_refs):
    for src, dst in zip(src_refs, dst_refs):
        dst[...] = src[...].astype(dst.dtype)


def _resident(shape):
    return pl.BlockSpec(shape, lambda i: (0,) * len(shape), pipeline_mode=pl.Buffered(1))


def _kv_kernel(mem_ref, nw_ref, wk_ref, wv_ref, o_ref, h_scr):
    @pl.when(pl.program_id(0) == 0)
    def _():
        h_scr[...] = _rms(mem_ref[...], nw_ref[...]).astype(BF16)

    h = h_scr[...]
    o_ref[0] = _dot(h, wk_ref[...].astype(BF16)).astype(o_ref.dtype)
    o_ref[1] = _dot(h, wv_ref[...].astype(BF16)).astype(o_ref.dtype)


def _kv_proj(mem2d, norm_w, wk, wv):
    rows, d = mem2d.shape
    tn = 512
    return pl.pallas_call(
        _kv_kernel,
        grid=(d // tn,),
        in_specs=[
            pl.BlockSpec((rows, d), lambda n: (0, 0)),
            pl.BlockSpec((1, d), lambda n: (0, 0)),
            pl.BlockSpec((d, tn), lambda n: (0, n)),
            pl.BlockSpec((d, tn), lambda n: (0, n)),
        ],
        out_specs=pl.BlockSpec((2, rows, tn), lambda n: (0, 0, n)),
        out_shape=jax.ShapeDtypeStruct((2, rows, d), BF16),
        scratch_shapes=[pltpu.VMEM((rows, d), BF16)],
        compiler_params=pltpu.CompilerParams(
            dimension_semantics=("arbitrary",),
            vmem_limit_bytes=KV_VMEM_LIMIT_BYTES),
        name="kv_proj",
    )(mem2d, norm_w, wk, wv)


def _chunk_cumsum(x):
    pos = lax.broadcasted_iota(jnp.int32, x.shape, 0) % HGRN_CHUNK
    s = 1
    while s < HGRN_CHUNK:
        x = x + jnp.where(pos >= s, pltpu.roll(x, s, axis=0), 0.0)
        s *= 2
    return x


def _block_diag2(a, b):
    z = jnp.zeros(a.shape, a.dtype)
    return jnp.concatenate([jnp.concatenate([a, z], axis=1),
                            jnp.concatenate([z, b], axis=1)], axis=0)


def _intra_chunk_by_offset(qf, kk, b, v):
    pos = lax.broadcasted_iota(jnp.int32, qf.shape, 0) % HGRN_CHUNK
    ones = jnp.ones((LANES, LANES), BF16)
    head_sum = _block_diag2(ones, ones)

    def one_offset(d, acc):
        in_chunk = pos >= d
        decay = jnp.exp(jnp.where(in_chunk, b - pltpu.roll(b, d, axis=0), 0.0))
        w = jnp.where(in_chunk, qf * pltpu.roll(kk, d, axis=0) * decay, 0.0)
        return acc + _dot(w.astype(BF16), head_sum) * pltpu.roll(v, d, axis=0)

    return lax.fori_loop(0, HGRN_CHUNK, one_offset, jnp.zeros(qf.shape, F32))


def _mixer_kernel(x_ref, nw_ref, win_ref, wout_ref, lb_ref, hnw_ref, scw_ref, *rest,
                  tiles_per_seq, layer, stable, n_cast):
    cast_in, (o_ref, bmin_ref) = rest[:n_cast], rest[n_cast:n_cast + 2]
    cast_out = rest[n_cast + 2:2 * n_cast + 2]
    h_scr, state_scr, carry_scr = rest[2 * n_cast + 2:]
    _cast_blocks(cast_in, cast_out)
    i = pl.program_id(0)
    tm = x_ref.shape[0]
    hd = LANES
    gw = 2 * hd
    width = lb_ref.shape[1]
    n_pairs = width // gw
    n_chunks = tm // HGRN_CHUNK
    chunks = [slice(c * HGRN_CHUNK, (c + 1) * HGRN_CHUNK) for c in range(n_chunks)]
    blocks = [slice(r0, r0 + HGRN_BLOCK) for r0 in range(0, tm, HGRN_BLOCK)]

    @pl.when(i % tiles_per_seq == 0)
    def _():
        state_scr[...] = jnp.zeros(state_scr.shape, F32)
        carry_scr[...] = jnp.zeros(carry_scr.shape, F32)

    lb_all = jnp.sum(jax.nn.softmax(lb_ref[...], axis=0)[0:layer + 1], axis=0, keepdims=True)
    hnw = hnw_ref[...]
    scw = scw_ref[...]
    if not stable:
        t_id = lax.broadcasted_iota(jnp.int32, (HGRN_BLOCK, 2 * HGRN_BLOCK), 0)
        s_id = lax.broadcasted_iota(jnp.int32, (HGRN_BLOCK, 2 * HGRN_BLOCK), 1) % HGRN_BLOCK
        keep = ((t_id // HGRN_CHUNK) == (s_id // HGRN_CHUNK)) & (s_id <= t_id)

    h = _normalised_tile(x_ref, nw_ref, h_scr)
    gated, convs, b_mins = [], [], []
    for p in range(n_pairs):
        lanes = slice(p * gw, (p + 1) * gw)

        def proj(group):
            return _dot(h, win_ref[:, group * width + p * gw:group * width + (p + 1) * gw])

        f_pre = proj(1)
        q = proj(0)
        lb = lb_all[:, lanes]
        f = lb + (1.0 - lb) * jax.nn.sigmoid(f_pre)
        logf = jnp.log(f)
        kk = 1.0 - f
        qf = _silu(q)
        b = _chunk_cumsum(logf)
        v = proj(2)
        vb = v.astype(BF16)
        g = proj(3)
        cb = proj(4)
        cc = proj(5)
        qt = (qf * jnp.exp(b)).astype(BF16)

        if not stable:
            kt = (kk * jnp.exp(-b)).astype(BF16)
            scores = [_dot_nt(qt[rows], _block_diag2(kt[rows, :hd], kt[rows, hd:]))
                      for rows in blocks]

        b_lasts = [b[rows][HGRN_CHUNK - 1:HGRN_CHUNK] for rows in chunks]
        upds = [_dot_tn(vb[rows], (kk[rows] * jnp.exp(bl - b[rows])).astype(BF16))
                for rows, bl in zip(chunks, b_lasts)]
        ch = proj(6)

        if stable:
            intra = _intra_chunk_by_offset(qf, kk, b, v)
        else:
            intra = jnp.concatenate(
                [_dot(jnp.where(keep, sc, 0.0).astype(BF16),
                      _block_diag2(vb[rows, :hd], vb[rows, hd:]))
                 for rows, sc in zip(blocks, scores)], axis=0)

        st0 = state_scr[2 * p]
        st1 = state_scr[2 * p + 1]
        starts = []
        for upd, bl in zip(upds, b_lasts):
            starts.append(_block_diag2(st0.astype(BF16), st1.astype(BF16)))
            decay = jnp.exp(bl)
            st0 = st0 * decay[:, :hd] + upd[:hd, :hd]
            st1 = st1 * decay[:, hd:] + upd[hd:, hd:]
        state_scr[2 * p] = st0
        state_scr[2 * p + 1] = st1
        inter = [_dot_nt(qt[rows], start) for rows, start in zip(chunks, starts)]

        u = cc * ch
        prev8 = carry_scr[p]
        carry_scr[p] = u[tm - SUBLANES:]
        convs.append((cb * _causal_conv3(u, prev8, scw[:, lanes])).astype(BF16))

        o = intra + jnp.concatenate(inter, axis=0)
        o = jnp.concatenate([_rms(o[:, :hd], hnw), _rms(o[:, hd:], hnw)], axis=1)
        gated.append((o * _silu(g)).astype(BF16))
        b_mins.append(functools.reduce(jnp.minimum, b_lasts))

    mixed = jnp.concatenate(gated + convs, axis=1)
    o_ref[...] = x_ref[...] + _dot(mixed, wout_ref[...])

    b_min = jnp.concatenate(b_mins, axis=1)

    @pl.when(i == 0)
    def _():
        bmin_ref[...] = b_min

    @pl.when(i > 0)
    def _():
        bmin_ref[...] = jnp.minimum(bmin_ref[...], b_min)


def _mixer(x2d, norm_w, w_in, w_out, hgrn_lb, hgrn_norm_w, sconv_w, to_cast=(), *,
           seq_len, layer, stable):
    m, d = x2d.shape
    tm = MIXER_ROW_TILE
    width = hgrn_lb.shape[1]
    n_tiles = m // tm
    cast_specs = _cast_specs(to_cast, n_tiles)
    kern = functools.partial(_mixer_kernel, tiles_per_seq=seq_len // tm, layer=layer,
                             stable=stable, n_cast=len(to_cast))
    out, b_min, *casted = pl.pallas_call(
        kern,
        grid=(n_tiles,),
        in_specs=[
            pl.BlockSpec((tm, d), lambda i: (i, 0)),
            _resident((1, d)),
            _resident(w_in.shape),
            _resident(w_out.shape),
            _resident(hgrn_lb.shape),
            _resident((1, LANES)),
            _resident(sconv_w.shape),
            *cast_specs,
        ],
        out_specs=[pl.BlockSpec((tm, d), lambda i: (i, 0)),
                   pl.BlockSpec((1, width), lambda i: (0, 0)),
                   *cast_specs],
        out_shape=[jax.ShapeDtypeStruct((m, d), F32),
                   jax.ShapeDtypeStruct((1, width), F32),
                   *(jax.ShapeDtypeStruct(w.shape, BF16) for w in to_cast)],
        scratch_shapes=[
            pltpu.VMEM((tm, d), BF16),
            pltpu.VMEM((HGRN_HEADS, LANES, LANES), F32),
            pltpu.VMEM((width // (2 * LANES), SUBLANES, 2 * LANES), F32),
        ],
        compiler_params=pltpu.CompilerParams(
            dimension_semantics=("arbitrary",),
            vmem_limit_bytes=MIXER_VMEM_LIMIT_BYTES),
        name="hybrid_mixer_stable" if stable else "hybrid_mixer",
    )(x2d, norm_w, w_in, w_out, hgrn_lb, hgrn_norm_w, sconv_w, *to_cast)
    return out, b_min, casted


def _attn_kernel(x_ref, nw_ref, wq_ref, k_ref, v_ref, wo_ref, *rest, head_dim, n_cast):
    cast_in, o_ref = rest[:n_cast], rest[n_cast]
    cast_out, h_scr = rest[n_cast + 1:2 * n_cast + 1], rest[2 * n_cast + 1]
    h = _normalised_tile(x_ref, nw_ref, h_scr)
    scale = head_dim ** -0.5
    cols = [slice(hh * head_dim, (hh + 1) * head_dim) for hh in range(MEM_HEADS)]
    qs = [_dot(h, wq_ref[:, c]).astype(BF16) for c in cols]
    _cast_blocks(cast_in, cast_out)
    ss = [_dot_nt(q, k_ref[:, c]) * scale for q, c in zip(qs, cols)]
    out = x_ref[...]
    for s, c in zip(ss, cols):
        e = jnp.exp(s - jnp.max(s, axis=-1, keepdims=True))
        pr = (e / jnp.sum(e, axis=-1, keepdims=True)).astype(BF16)
        out = out + _dot(_dot(pr, v_ref[:, c]).astype(BF16), wo_ref[c, :])
    o_ref[...] = out


def _attn(x2d, norm_w, wq, kv, wo, to_cast, *, seq_len, mem_len):
    m, d = x2d.shape
    tm = ATTN_ROW_TILE
    tiles_per_seq = seq_len // tm
    n_tiles = m // tm

    cast_specs = _cast_specs(to_cast, n_tiles)
    kern = functools.partial(_attn_kernel, head_dim=d // MEM_HEADS, n_cast=len(to_cast))
    out, *casted = pl.pallas_call(
        kern,
        grid=(n_tiles,),
        in_specs=[
            pl.BlockSpec((tm, d), lambda i: (i, 0)),
            _resident((1, d)),
            _resident((d, d)),
            pl.BlockSpec((None, None, mem_len, d), lambda i: (0, i // tiles_per_seq, 0, 0)),
            pl.BlockSpec((None, None, mem_len, d), lambda i: (1, i // tiles_per_seq, 0, 0)),
            _resident((d, d)),
            *cast_specs,
        ],
        out_specs=[pl.BlockSpec((tm, d), lambda i: (i, 0)), *cast_specs],
        out_shape=[jax.ShapeDtypeStruct((m, d), F32),
                   *(jax.ShapeDtypeStruct(w.shape, BF16) for w in to_cast)],
        scratch_shapes=[pltpu.VMEM((tm, d), BF16)],
        compiler_params=pltpu.CompilerParams(
            dimension_semantics=("arbitrary",),
            vmem_limit_bytes=ATTN_VMEM_LIMIT_BYTES),
        name="mem_cross_attn",
    )(x2d, norm_w, wq, kv, kv, wo, *to_cast)
    return out, casted


def _ffn_kernel(x_ref, nw_ref, wg_ref, wu_ref, cw_ref, cb_ref, wd_ref, fnw_ref, o_ref,
                h_scr, carry_scr, *, tiles_per_seq, n_steps, final_norm):
    i = pl.program_id(0)
    f = pl.program_id(1)
    tm = x_ref.shape[0]

    @pl.when(f == 0)
    def _():
        xv = x_ref[...]
        h_scr[...] = _rms(xv, nw_ref[...]).astype(BF16)
        o_ref[...] = xv

    @pl.when(i % tiles_per_seq == 0)
    def _():
        carry_scr[f] = jnp.zeros(carry_scr.shape[1:], F32)

    h = h_scr[...]
    a = _dot(h, wg_ref[...])
    up = _dot(h, wu_ref[...])
    prev8 = carry_scr[f]
    carry_scr[f] = a[tm - SUBLANES:]
    act = _silu(_causal_conv3(a, prev8, cw_ref[...]) + cb_ref[...]) * up
    o_ref[...] += _dot(act.astype(BF16), wd_ref[...])

    if final_norm:
        @pl.when(f == n_steps - 1)
        def _():
            o_ref[...] = _rms(o_ref[...], fnw_ref[...])


def _ffn(x2d, norm_w, w_gate, w_up, conv_w, conv_b, w_down, final_w, *, seq_len, final_norm):
    m, d = x2d.shape
    d_ff = w_gate.shape[1]
    tm, tf = FFN_ROW_TILE, FFN_TILE
    nf = d_ff // tf
    kern = functools.partial(_ffn_kernel, tiles_per_seq=seq_len // tm, n_steps=nf,
                             final_norm=final_norm)
    return pl.pallas_call(
        kern,
        grid=(m // tm, nf),
        in_specs=[
            pl.BlockSpec((tm, d), lambda i, f: (i, 0)),
            pl.BlockSpec((1, d), lambda i, f: (0, 0)),
            pl.BlockSpec((d, tf), lambda i, f: (0, f)),
            pl.BlockSpec((d, tf), lambda i, f: (0, f)),
            pl.BlockSpec((CONV_TAPS, tf), lambda i, f: (0, f)),
            pl.BlockSpec((1, tf), lambda i, f: (0, f)),
            pl.BlockSpec((tf, d), lambda i, f: (f, 0)),
            pl.BlockSpec((1, d), lambda i, f: (0, 0)),
        ],
        out_specs=pl.BlockSpec((tm, d), lambda i, f: (i, 0)),
        out_shape=jax.ShapeDtypeStruct((m, d), F32),
        scratch_shapes=[
            pltpu.VMEM((tm, d), BF16),
            pltpu.VMEM((nf, SUBLANES, tf), F32),
        ],
        compiler_params=pltpu.CompilerParams(
            dimension_semantics=("arbitrary", "arbitrary"),
            vmem_limit_bytes=FFN_VMEM_LIMIT_BYTES),
        name="conv_ffn",
    )(x2d, norm_w, w_gate, w_up, conv_w, conv_b, w_down, final_w)


def kernel(x, mem, hgrn_lb, norm1_w, w_in, hgrn_norm_w, sconv_w, w_out, norm2_w, mem_norm_w,
           wq, wk, wv, wo, norm3_w, w_gate, w_up, ffn_conv_w, ffn_conv_b, w_down, final_norm_w):
    batch, seq_len, d = x.shape
    mem_len = mem.shape[1]
    depth = norm1_w.shape[0]
    assert all(seq_len % t == 0 for t in (MIXER_ROW_TILE, ATTN_ROW_TILE, FFN_ROW_TILE))
    assert MIXER_ROW_TILE % HGRN_BLOCK == 0 and HGRN_BLOCK % HGRN_CHUNK == 0
    assert hgrn_lb.shape[1] == HGRN_HEADS * LANES and HGRN_HEADS % 2 == 0
    assert w_in.shape[2] == N_PROJ_GROUPS * HGRN_HEADS * LANES and w_gate.shape[2] % FFN_TILE == 0

    xs = x.reshape(batch * seq_len, d)
    mem2d = mem.reshape(batch * mem_len, d)
    row = lambda w: w.reshape(1, -1)
    for l in range(depth):
        mixer = functools.partial(
            _mixer, xs, row(norm1_w[l]), w_in[l].astype(BF16), w_out[l].astype(BF16), hgrn_lb,
            row(hgrn_norm_w[l]), sconv_w[l], seq_len=seq_len, layer=l)
        fast, b_min, (wq_b, wo_b) = mixer((wq[l], wo[l]), stable=False)
        xs = lax.cond(jnp.min(b_min) < -HGRN_SAFE_LOG_DECAY,
                      lambda: mixer(stable=True)[0], lambda: fast)
        kv = _kv_proj(mem2d, row(mem_norm_w[l]), wk[l], wv[l])
        kv = kv.reshape(2, batch, mem_len, d)
        xs, (wg_b, wu_b, wd_b) = _attn(
            xs, row(norm2_w[l]), wq_b, kv, wo_b,
            (w_gate[l], w_up[l], w_down[l]), seq_len=seq_len, mem_len=mem_len)
        xs = _ffn(xs, row(norm3_w[l]), wg_b, wu_b, ffn_conv_w[l], row(ffn_conv_b[l]), wd_b,
                  row(final_norm_w), seq_len=seq_len, final_norm=(l == depth - 1))
    return xs.reshape(batch, seq_len, d)
```

```python
import functools

import jax
import jax.numpy as jnp
from jax import lax
from jax.experimental import pallas as pl
from jax.experimental.pallas import tpu as pltpu

F32 = jnp.float32
BF16 = jnp.bfloat16

EPS = 1e-6
HGRN_HEADS = 8
HGRN_CHUNK = 64
HGRN_SAFE_LOG_DECAY = 75.0
HGRN_BLOCK = 128
CONV_TAPS = 3
MEM_HEADS = 4
N_PROJ_GROUPS = 7
LANES = 128
SUBLANES = 8
BF16_ROWS = 16
KV_VMEM_LIMIT_BYTES = 56 * 1024 * 1024
MIXER_VMEM_LIMIT_BYTES = 60 * 1024 * 1024
FFN_VMEM_LIMIT_BYTES = 60 * 1024 * 1024
ATTN_VMEM_LIMIT_BYTES = 62 * 1024 * 1024

MIXER_ROW_TILE = 256
ATTN_ROW_TILE = 512
FFN_ROW_TILE = 1024
FFN_TILE = 512


def _rms(x, w):
    ms = jnp.mean(x * x, axis=-1, keepdims=True)
    return x * lax.rsqrt(ms + EPS) * w


def _silu(x):
    return x * jax.nn.sigmoid(x)


def _dot(a, b):
    return jnp.dot(a, b, preferred_element_type=F32)


def _dot_nt(a, b):
    return lax.dot_general(a, b, (((1,), (1,)), ((), ())), preferred_element_type=F32)


def _dot_tn(a, b):
    return lax.dot_general(a, b, (((0,), (0,)), ((), ())), preferred_element_type=F32)


def _shift_rows(cur, prev8, shift):
    rolled = pltpu.roll(cur, shift, axis=0)
    head = pltpu.roll(prev8, shift, axis=0)
    rid = lax.broadcasted_iota(jnp.int32, prev8.shape, 0)
    first = jnp.where(rid < shift, head, rolled[:SUBLANES])
    return jnp.concatenate([first, rolled[SUBLANES:]], axis=0)


def _causal_conv3(cur, prev8, w):
    return (w[0:1] * _shift_rows(cur, prev8, 2)
            + w[1:2] * _shift_rows(cur, prev8, 1)
            + w[2:3] * cur)


def _normalised_tile(x_ref, nw_ref, h_scr):
    h_scr[...] = _rms(x_ref[...], nw_ref[...]).astype(BF16)
    return h_scr[...]


def _cast_specs(to_cast, n_steps):
    def spec(w):
        rows = w.shape[0]
        rb = pl.cdiv(pl.cdiv(rows, n_steps), BF16_ROWS) * BF16_ROWS
        while rows % rb:
            rb += BF16_ROWS
        last = rows // rb - 1
        return pl.BlockSpec((rb, w.shape[1]), lambda i: (jnp.minimum(i, last), 0))

    return [spec(w) for w in to_cast]


def _cast_blocks(src_refs, dst_refs):
    for src, dst in zip(src_refs, dst_refs):
        dst[...] = src[...].astype(dst.dtype)


def _resident(shape):
    return pl.BlockSpec(shape, lambda i: (0,) * len(shape), pipeline_mode=pl.Buffered(1))


def _kv_kernel(mem_ref, nw_ref, wk_ref, wv_ref, o_ref, h_scr):
    @pl.when(pl.program_id(0) == 0)
    def _():
        h_scr[...] = _rms(mem_ref[...], nw_ref[...]).astype(BF16)

    h = h_scr[...]
    o_ref[0] = _dot(h, wk_ref[...].astype(BF16)).astype(o_ref.dtype)
    o_ref[1] = _dot(h, wv_ref[...].astype(BF16)).astype(o_ref.dtype)


def _kv_proj(mem2d, norm_w, wk, wv):
    rows, d = mem2d.shape
    tn = 512
    return pl.pallas_call(
        _kv_kernel,
        grid=(d // tn,),
        in_specs=[
            pl.BlockSpec((rows, d), lambda n: (0, 0)),
            pl.BlockSpec((1, d), lambda n: (0, 0)),
            pl.BlockSpec((d, tn), lambda n: (0, n)),
            pl.BlockSpec((d, tn), lambda n: (0, n)),
        ],
        out_specs=pl.BlockSpec((2, rows, tn), lambda n: (0, 0, n)),
        out_shape=jax.ShapeDtypeStruct((2, rows, d), BF16),
        scratch_shapes=[pltpu.VMEM((rows, d), BF16)],
        compiler_params=pltpu.CompilerParams(
            dimension_semantics=("arbitrary",),
            vmem_limit_bytes=KV_VMEM_LIMIT_BYTES),
        name="kv_proj",
    )(mem2d, norm_w, wk, wv)


def _chunk_cumsum(x):
    pos = lax.broadcasted_iota(jnp.int32, x.shape, 0) % HGRN_CHUNK
    s = 1
    while s < HGRN_CHUNK:
        x = x + jnp.where(pos >= s, pltpu.roll(x, s, axis=0), 0.0)
        s *= 2
    return x


def _block_diag2(a, b):
    z = jnp.zeros(a.shape, a.dtype)
    return jnp.concatenate([jnp.concatenate([a, z], axis=1),
                            jnp.concatenate([z, b], axis=1)], axis=0)


def _intra_chunk_by_offset(qf, kk, b, v):
    pos = lax.broadcasted_iota(jnp.int32, qf.shape, 0) % HGRN_CHUNK
    ones = jnp.ones((LANES, LANES), BF16)
    head_sum = _block_diag2(ones, ones)

    def one_offset(d, acc):
        in_chunk = pos >= d
        decay = jnp.exp(jnp.where(in_chunk, b - pltpu.roll(b, d, axis=0), 0.0))
        w = jnp.where(in_chunk, qf * pltpu.roll(kk, d, axis=0) * decay, 0.0)
        return acc + _dot(w.astype(BF16), head_sum) * pltpu.roll(v, d, axis=0)

    return lax.fori_loop(0, HGRN_CHUNK, one_offset, jnp.zeros(qf.shape, F32))


def _mixer_kernel(x_ref, nw_ref, win_ref, wout_ref, lb_ref, hnw_ref, scw_ref, *rest,
                  tiles_per_seq, layer, stable, n_cast):
    cast_in, (o_ref, bmin_ref) = rest[:n_cast], rest[n_cast:n_cast + 2]
    cast_out = rest[n_cast + 2:2 * n_cast + 2]
    h_scr, state_scr, carry_scr = rest[2 * n_cast + 2:]
    _cast_blocks(cast_in, cast_out)
    i = pl.program_id(0)
    tm = x_ref.shape[0]
    hd = LANES
    gw = 2 * hd
    width = lb_ref.shape[1]
    n_pairs = width // gw
    n_chunks = tm // HGRN_CHUNK
    chunks = [slice(c * HGRN_CHUNK, (c + 1) * HGRN_CHUNK) for c in range(n_chunks)]
    blocks = [slice(r0, r0 + HGRN_BLOCK) for r0 in range(0, tm, HGRN_BLOCK)]

    @pl.when(i % tiles_per_seq == 0)
    def _():
        state_scr[...] = jnp.zeros(state_scr.shape, F32)
        carry_scr[...] = jnp.zeros(carry_scr.shape, F32)

    lb_all = jnp.sum(jax.nn.softmax(lb_ref[...], axis=0)[0:layer + 1], axis=0, keepdims=True)
    hnw = hnw_ref[...]
    scw = scw_ref[...]
    if not stable:
        t_id = lax.broadcasted_iota(jnp.int32, (HGRN_BLOCK, 2 * HGRN_BLOCK), 0)
        s_id = lax.broadcasted_iota(jnp.int32, (HGRN_BLOCK, 2 * HGRN_BLOCK), 1) % HGRN_BLOCK
        keep = ((t_id // HGRN_CHUNK) == (s_id // HGRN_CHUNK)) & (s_id <= t_id)

    h = _normalised_tile(x_ref, nw_ref, h_scr)
    gated, convs, b_mins = [], [], []
    for p in range(n_pairs):
        lanes = slice(p * gw, (p + 1) * gw)

        def proj(group):
            return _dot(h, win_ref[:, group * width + p * gw:group * width + (p + 1) * gw])

        f_pre = proj(1)
        q = proj(0)
        lb = lb_all[:, lanes]
        f = lb + (1.0 - lb) * jax.nn.sigmoid(f_pre)
        logf = jnp.log(f)
        kk = 1.0 - f
        qf = _silu(q)
        b = _chunk_cumsum(logf)
        v = proj(2)
        vb = v.astype(BF16)
        g = proj(3)
        cb = proj(4)
        cc = proj(5)
        qt = (qf * jnp.exp(b)).astype(BF16)

        if not stable:
            kt = kk * jnp.exp(-b)
            scores = [_dot(qt[rows], _block_diag2(kt[rows, :hd].T.astype(BF16),
                                                  kt[rows, hd:].T.astype(BF16)))
                      for rows in blocks]

        b_lasts = [b[rows][HGRN_CHUNK - 1:HGRN_CHUNK] for rows in chunks]
        upds = [_dot_tn((kk[rows] * jnp.exp(bl - b[rows])).astype(BF16), vb[rows])
                for rows, bl in zip(chunks, b_lasts)]
        ch = proj(6)

        if stable:
            intra = _intra_chunk_by_offset(qf, kk, b, v)
        else:
            intra = jnp.concatenate(
                [_dot(jnp.where(keep, sc, 0.0).astype(BF16),
                      _block_diag2(vb[rows, :hd], vb[rows, hd:]))
                 for rows, sc in zip(blocks, scores)], axis=0)

        st0 = state_scr[2 * p]
        st1 = state_scr[2 * p + 1]
        starts = []
        for upd, bl in zip(upds, b_lasts):
            starts.append(_block_diag2(st0.astype(BF16), st1.astype(BF16)))
            decay = jnp.exp(bl)
            st0 = st0 * jnp.broadcast_to(decay[:, :hd], (hd, hd)).T + upd[:hd, :hd]
            st1 = st1 * jnp.broadcast_to(decay[:, hd:], (hd, hd)).T + upd[hd:, hd:]
        state_scr[2 * p] = st0
        state_scr[2 * p + 1] = st1
        inter = [_dot(qt[rows], start) for rows, start in zip(chunks, starts)]

        u = cc * ch
        prev8 = carry_scr[p]
        carry_scr[p] = u[tm - SUBLANES:]
        convs.append((cb * _causal_conv3(u, prev8, scw[:, lanes])).astype(BF16))

        o = intra + jnp.concatenate(inter, axis=0)
        o = jnp.concatenate([_rms(o[:, :hd], hnw), _rms(o[:, hd:], hnw)], axis=1)
        gated.append((o * _silu(g)).astype(BF16))
        b_mins.append(functools.reduce(jnp.minimum, b_lasts))

    mixed = jnp.concatenate(gated + convs, axis=1)
    o_ref[...] = x_ref[...] + _dot(mixed, wout_ref[...])

    b_min = jnp.concatenate(b_mins, axis=1)

    @pl.when(i == 0)
    def _():
        bmin_ref[...] = b_min

    @pl.when(i > 0)
    def _():
        bmin_ref[...] = jnp.minimum(bmin_ref[...], b_min)


def _mixer(x2d, norm_w, w_in, w_out, hgrn_lb, hgrn_norm_w, sconv_w, to_cast=(), *,
           seq_len, layer, stable):
    m, d = x2d.shape
    tm = MIXER_ROW_TILE
    width = hgrn_lb.shape[1]
    n_tiles = m // tm
    cast_specs = _cast_specs(to_cast, n_tiles)
    kern = functools.partial(_mixer_kernel, tiles_per_seq=seq_len // tm, layer=layer,
                             stable=stable, n_cast=len(to_cast))
    out, b_min, *casted = pl.pallas_call(
        kern,
        grid=(n_tiles,),
        in_specs=[
            pl.BlockSpec((tm, d), lambda i: (i, 0)),
            _resident((1, d)),
            _resident(w_in.shape),
            _resident(w_out.shape),
            _resident(hgrn_lb.shape),
            _resident((1, LANES)),
            _resident(sconv_w.shape),
            *cast_specs,
        ],
        out_specs=[pl.BlockSpec((tm, d), lambda i: (i, 0)),
                   pl.BlockSpec((1, width), lambda i: (0, 0)),
                   *cast_specs],
        out_shape=[jax.ShapeDtypeStruct((m, d), F32),
                   jax.ShapeDtypeStruct((1, width), F32),
                   *(jax.ShapeDtypeStruct(w.shape, BF16) for w in to_cast)],
        scratch_shapes=[
            pltpu.VMEM((tm, d), BF16),
            pltpu.VMEM((HGRN_HEADS, LANES, LANES), F32),
            pltpu.VMEM((width // (2 * LANES), SUBLANES, 2 * LANES), F32),
        ],
        compiler_params=pltpu.CompilerParams(
            dimension_semantics=("arbitrary",),
            vmem_limit_bytes=MIXER_VMEM_LIMIT_BYTES),
        name="hybrid_mixer_stable" if stable else "hybrid_mixer",
    )(x2d, norm_w, w_in, w_out, hgrn_lb, hgrn_norm_w, sconv_w, *to_cast)
    return out, b_min, casted


def _attn_kernel(x_ref, nw_ref, wq_ref, k_ref, v_ref, wo_ref, *rest, head_dim, n_cast):
    cast_in, o_ref = rest[:n_cast], rest[n_cast]
    cast_out, h_scr = rest[n_cast + 1:2 * n_cast + 1], rest[2 * n_cast + 1]
    h = _normalised_tile(x_ref, nw_ref, h_scr)
    scale = head_dim ** -0.5
    cols = [slice(hh * head_dim, (hh + 1) * head_dim) for hh in range(MEM_HEADS)]
    qs = [_dot(h, wq_ref[:, c]).astype(BF16) for c in cols]
    _cast_blocks(cast_in, cast_out)
    ss = [_dot_nt(q, k_ref[:, c]) * scale for q, c in zip(qs, cols)]
    out = x_ref[...]
    for s, c in zip(ss, cols):
        e = jnp.exp(s - jnp.max(s, axis=-1, keepdims=True))
        pr = (e / jnp.sum(e, axis=-1, keepdims=True)).astype(BF16)
        out = out + _dot(_dot(pr, v_ref[:, c]).astype(BF16), wo_ref[c, :])
    o_ref[...] = out


def _attn(x2d, norm_w, wq, kv, wo, to_cast, *, seq_len, mem_len):
    m, d = x2d.shape
    tm = ATTN_ROW_TILE
    tiles_per_seq = seq_len // tm
    n_tiles = m // tm

    cast_specs = _cast_specs(to_cast, n_tiles)
    kern = functools.partial(_attn_kernel, head_dim=d // MEM_HEADS, n_cast=len(to_cast))
    out, *casted = pl.pallas_call(
        kern,
        grid=(n_tiles,),
        in_specs=[
            pl.BlockSpec((tm, d), lambda i: (i, 0)),
            _resident((1, d)),
            _resident((d, d)),
            pl.BlockSpec((None, None, mem_len, d), lambda i: (0, i // tiles_per_seq, 0, 0)),
            pl.BlockSpec((None, None, mem_len, d), lambda i: (1, i // tiles_per_seq, 0, 0)),
            _resident((d, d)),
            *cast_specs,
        ],
        out_specs=[pl.BlockSpec((tm, d), lambda i: (i, 0)), *cast_specs],
        out_shape=[jax.ShapeDtypeStruct((m, d), F32),
                   *(jax.ShapeDtypeStruct(w.shape, BF16) for w in to_cast)],
        scratch_shapes=[pltpu.VMEM((tm, d), BF16)],
        compiler_params=pltpu.CompilerParams(
            dimension_semantics=("arbitrary",),
            vmem_limit_bytes=ATTN_VMEM_LIMIT_BYTES),
        name="mem_cross_attn",
    )(x2d, norm_w, wq, kv, kv, wo, *to_cast)
    return out, casted


def _ffn_kernel(x_ref, nw_ref, wg_ref, wu_ref, cw_ref, cb_ref, wd_ref, fnw_ref, o_ref,
                h_scr, carry_scr, *, tiles_per_seq, n_steps, final_norm):
    i = pl.program_id(0)
    f = pl.program_id(1)
    tm = x_ref.shape[0]

    @pl.when(f == 0)
    def _():
        xv = x_ref[...]
        h_scr[...] = _rms(xv, nw_ref[...]).astype(BF16)
        o_ref[...] = xv

    @pl.when(i % tiles_per_seq == 0)
    def _():
        carry_scr[f] = jnp.zeros(carry_scr.shape[1:], F32)

    h = h_scr[...]
    a = _dot(h, wg_ref[...])
    up = _dot(h, wu_ref[...])
    prev8 = carry_scr[f]
    carry_scr[f] = a[tm - SUBLANES:]
    act = _silu(_causal_conv3(a, prev8, cw_ref[...]) + cb_ref[...]) * up
    o_ref[...] += _dot(act.astype(BF16), wd_ref[...])

    if final_norm:
        @pl.when(f == n_steps - 1)
        def _():
            o_ref[...] = _rms(o_ref[...], fnw_ref[...])


def _ffn(x2d, norm_w, w_gate, w_up, conv_w, conv_b, w_down, final_w, *, seq_len, final_norm):
    m, d = x2d.shape
    d_ff = w_gate.shape[1]
    tm, tf = FFN_ROW_TILE, FFN_TILE
    nf = d_ff // tf
    kern = functools.partial(_ffn_kernel, tiles_per_seq=seq_len // tm, n_steps=nf,
                             final_norm=final_norm)
    return pl.pallas_call(
        kern,
        grid=(m // tm, nf),
        in_specs=[
            pl.BlockSpec((tm, d), lambda i, f: (i, 0)),
            pl.BlockSpec((1, d), lambda i, f: (0, 0)),
            pl.BlockSpec((d, tf), lambda i, f: (0, f)),
            pl.BlockSpec((d, tf), lambda i, f: (0, f)),
            pl.BlockSpec((CONV_TAPS, tf), lambda i, f: (0, f)),
            pl.BlockSpec((1, tf), lambda i, f: (0, f)),
            pl.BlockSpec((tf, d), lambda i, f: (f, 0)),
            pl.BlockSpec((1, d), lambda i, f: (0, 0)),
        ],
        out_specs=pl.BlockSpec((tm, d), lambda i, f: (i, 0)),
        out_shape=jax.ShapeDtypeStruct((m, d), F32),
        scratch_shapes=[
            pltpu.VMEM((tm, d), BF16),
            pltpu.VMEM((nf, SUBLANES, tf), F32),
        ],
        compiler_params=pltpu.CompilerParams(
            dimension_semantics=("arbitrary", "arbitrary"),
            vmem_limit_bytes=FFN_VMEM_LIMIT_BYTES),
        name="conv_ffn",
    )(x2d, norm_w, w_gate, w_up, conv_w, conv_b, w_down, final_w)


def kernel(x, mem, hgrn_lb, norm1_w, w_in, hgrn_norm_w, sconv_w, w_out, norm2_w, mem_norm_w,
           wq, wk, wv, wo, norm3_w, w_gate, w_up, ffn_conv_w, ffn_conv_b, w_down, final_norm_w):
    batch, seq_len, d = x.shape
    mem_len = mem.shape[1]
    depth = norm1_w.shape[0]
    assert all(seq_len % t == 0 for t in (MIXER_ROW_TILE, ATTN_ROW_TILE, FFN_ROW_TILE))
    assert MIXER_ROW_TILE % HGRN_BLOCK == 0 and HGRN_BLOCK % HGRN_CHUNK == 0
    assert hgrn_lb.shape[1] == HGRN_HEADS * LANES and HGRN_HEADS % 2 == 0
    assert w_in.shape[2] == N_PROJ_GROUPS * HGRN_HEADS * LANES and w_gate.shape[2] % FFN_TILE == 0

    xs = x.reshape(batch * seq_len, d)
    mem2d = mem.reshape(batch * mem_len, d)
    row = lambda w: w.reshape(1, -1)
    for l in range(depth):
        mixer = functools.partial(
            _mixer, xs, row(norm1_w[l]), w_in[l].astype(BF16), w_out[l].astype(BF16), hgrn_lb,
            row(hgrn_norm_w[l]), sconv_w[l], seq_len=seq_len, layer=l)
        fast, b_min, (wq_b, wo_b) = mixer((wq[l], wo[l]), stable=False)
        xs = lax.cond(jnp.min(b_min) < -HGRN_SAFE_LOG_DECAY,
                      lambda: mixer(stable=True)[0], lambda: fast)
        kv = _kv_proj(mem2d, row(mem_norm_w[l]), wk[l], wv[l])
        kv = kv.reshape(2, batch, mem_len, d)
        xs, (wg_b, wu_b, wd_b) = _attn(
            xs, row(norm2_w[l]), wq_b, kv, wo_b,
            (w_gate[l], w_up[l], w_down[l]), seq_len=seq_len, mem_len=mem_len)
        xs = _ffn(xs, row(norm3_w[l]), wg_b, wu_b, ffn_conv_w[l], row(ffn_conv_b[l]), wd_b,
                  row(final_norm_w), seq_len=seq_len, final_norm=(l == depth - 1))
    return xs.reshape(batch, seq_len, d)
```

```python
import functools

import jax
import jax.numpy as jnp
from jax import lax
from jax.experimental import pallas as pl
from jax.experimental.pallas import tpu as pltpu

F32 = jnp.float32
BF16 = jnp.bfloat16

EPS = 1e-6
HGRN_HEADS = 8
HGRN_CHUNK = 64
HGRN_SAFE_LOG_DECAY = 75.0
HGRN_BLOCK = 128
CONV_TAPS = 3
MEM_HEADS = 4
N_PROJ_GROUPS = 7
LANES = 128
SUBLANES = 8
BF16_ROWS = 16
KV_VMEM_LIMIT_BYTES = 56 * 1024 * 1024
MIXER_VMEM_LIMIT_BYTES = 60 * 1024 * 1024
FFN_VMEM_LIMIT_BYTES = 60 * 1024 * 1024
ATTN_VMEM_LIMIT_BYTES = 62 * 1024 * 1024

MIXER_ROW_TILE = 256
ATTN_ROW_TILE = 512
FFN_ROW_TILE = 1024
FFN_TILE = 512


def _rms(x, w):
    ms = jnp.mean(x * x, axis=-1, keepdims=True)
    return x * lax.rsqrt(ms + EPS) * w


def _silu(x):
    return x * jax.nn.sigmoid(x)


def _dot(a, b):
    return jnp.dot(a, b, preferred_element_type=F32)


def _dot_nt(a, b):
    return lax.dot_general(a, b, (((1,), (1,)), ((), ())), preferred_element_type=F32)


def _dot_tn(a, b):
    return lax.dot_general(a, b, (((0,), (0,)), ((), ())), preferred_element_type=F32)


def _shift_rows(cur, prev8, shift):
    rolled = pltpu.roll(cur, shift, axis=0)
    head = pltpu.roll(prev8, shift, axis=0)
    rid = lax.broadcasted_iota(jnp.int32, prev8.shape, 0)
    first = jnp.where(rid < shift, head, rolled[:SUBLANES])
    return jnp.concatenate([first, rolled[SUBLANES:]], axis=0)


def _causal_conv3(cur, prev8, w):
    return (w[0:1] * _shift_rows(cur, prev8, 2)
            + w[1:2] * _shift_rows(cur, prev8, 1)
            + w[2:3] * cur)


def _normalised_tile(x_ref, nw_ref, h_scr):
    h_scr[...] = _rms(x_ref[...], nw_ref[...]).astype(BF16)
    return h_scr[...]


def _cast_specs(to_cast, n_steps):
    def spec(w):
        rows = w.shape[0]
        rb = pl.cdiv(pl.cdiv(rows, n_steps), BF16_ROWS) * BF16_ROWS
        while rows % rb:
            rb += BF16_ROWS
        last = rows // rb - 1
        return pl.BlockSpec((rb, w.shape[1]), lambda i: (jnp.minimum(i, last), 0))

    return [spec(w) for w in to_cast]


def _cast_blocks(src_refs, dst_refs):
    for src, dst in zip(src_refs, dst_refs):
        dst[...] = src[...].astype(dst.dtype)


def _resident(shape):
    return pl.BlockSpec(shape, lambda i: (0,) * len(shape), pipeline_mode=pl.Buffered(1))


def _kv_kernel(mem_ref, nw_ref, wk_ref, wv_ref, o_ref, h_scr):
    @pl.when(pl.program_id(0) == 0)
    def _():
        h_scr[...] = _rms(mem_ref[...], nw_ref[...]).astype(BF16)

    h = h_scr[...]
    o_ref[0] = _dot(h, wk_ref[...].astype(BF16)).astype(o_ref.dtype)
    o_ref[1] = _dot(h, wv_ref[...].astype(BF16)).astype(o_ref.dtype)


def _kv_proj(mem2d, norm_w, wk, wv):
    rows, d = mem2d.shape
    tn = 512
    return pl.pallas_call(
        _kv_kernel,
        grid=(d // tn,),
        in_specs=[
            pl.BlockSpec((rows, d), lambda n: (0, 0)),
            pl.BlockSpec((1, d), lambda n: (0, 0)),
            pl.BlockSpec((d, tn), lambda n: (0, n)),
            pl.BlockSpec((d, tn), lambda n: (0, n)),
        ],
        out_specs=pl.BlockSpec((2, rows, tn), lambda n: (0, 0, n)),
        out_shape=jax.ShapeDtypeStruct((2, rows, d), BF16),
        scratch_shapes=[pltpu.VMEM((rows, d), BF16)],
        compiler_params=pltpu.CompilerParams(
            dimension_semantics=("arbitrary",),
            vmem_limit_bytes=KV_VMEM_LIMIT_BYTES),
        name="kv_proj",
    )(mem2d, norm_w, wk, wv)


def _chunk_cumsum(x):
    pos = lax.broadcasted_iota(jnp.int32, x.shape, 0) % HGRN_CHUNK
    s = 1
    while s < HGRN_CHUNK:
        x = x + jnp.where(pos >= s, pltpu.roll(x, s, axis=0), 0.0)
        s *= 2
    return x


def _block_diag2(a, b):
    z = jnp.zeros(a.shape, a.dtype)
    return jnp.concatenate([jnp.concatenate([a, z], axis=1),
                            jnp.concatenate([z, b], axis=1)], axis=0)


def _intra_chunk_by_offset(qf, kk, b, v):
    pos = lax.broadcasted_iota(jnp.int32, qf.shape, 0) % HGRN_CHUNK
    ones = jnp.ones((LANES, LANES), BF16)
    head_sum = _block_diag2(ones, ones)

    def one_offset(d, acc):
        in_chunk = pos >= d
        decay = jnp.exp(jnp.where(in_chunk, b - pltpu.roll(b, d, axis=0), 0.0))
        w = jnp.where(in_chunk, qf * pltpu.roll(kk, d, axis=0) * decay, 0.0)
        return acc + _dot(w.astype(BF16), head_sum) * pltpu.roll(v, d, axis=0)

    return lax.fori_loop(0, HGRN_CHUNK, one_offset, jnp.zeros(qf.shape, F32))


def _mixer_kernel(x_ref, nw_ref, win_ref, wout_ref, lb_ref, hnw_ref, scw_ref, *rest,
                  tiles_per_seq, layer, stable, n_cast):
    cast_in, (o_ref, bmin_ref) = rest[:n_cast], rest[n_cast:n_cast + 2]
    cast_out = rest[n_cast + 2:2 * n_cast + 2]
    h_scr, state_scr, carry_scr = rest[2 * n_cast + 2:]
    _cast_blocks(cast_in, cast_out)
    i = pl.program_id(0)
    tm = x_ref.shape[0]
    hd = LANES
    gw = 2 * hd
    width = lb_ref.shape[1]
    n_pairs = width // gw
    n_chunks = tm // HGRN_CHUNK
    chunks = [slice(c * HGRN_CHUNK, (c + 1) * HGRN_CHUNK) for c in range(n_chunks)]
    blocks = [slice(r0, r0 + HGRN_BLOCK) for r0 in range(0, tm, HGRN_BLOCK)]

    @pl.when(i % tiles_per_seq == 0)
    def _():
        state_scr[...] = jnp.zeros(state_scr.shape, F32)
        carry_scr[...] = jnp.zeros(carry_scr.shape, F32)

    lb_all = jnp.sum(jax.nn.softmax(lb_ref[...], axis=0)[0:layer + 1], axis=0, keepdims=True)
    hnw = hnw_ref[...]
    scw = scw_ref[...]
    if not stable:
        t_id = lax.broadcasted_iota(jnp.int32, (HGRN_BLOCK, 2 * HGRN_BLOCK), 0)
        s_id = lax.broadcasted_iota(jnp.int32, (HGRN_BLOCK, 2 * HGRN_BLOCK), 1) % HGRN_BLOCK
        keep = ((t_id // HGRN_CHUNK) == (s_id // HGRN_CHUNK)) & (s_id <= t_id)

    h = _normalised_tile(x_ref, nw_ref, h_scr)
    gated, convs, b_mins = [], [], []
    for p in range(n_pairs):
        lanes = slice(p * gw, (p + 1) * gw)

        def proj(group):
            return _dot(h, win_ref[:, group * width + p * gw:group * width + (p + 1) * gw])

        f_pre = proj(1)
        q = proj(0)
        lb = lb_all[:, lanes]
        f = lb + (1.0 - lb) * jax.nn.sigmoid(f_pre)
        logf = jnp.log(f)
        kk = 1.0 - f
        qf = _silu(q)
        b = _chunk_cumsum(logf)
        v = proj(2)
        vb = v.astype(BF16)
        g = proj(3)
        cb = proj(4)
        cc = proj(5)
        qt = (qf * jnp.exp(b)).astype(BF16)

        if not stable:
            kt = (kk * jnp.exp(-b)).astype(BF16)
            scores = [_dot_nt(qt[rows], _block_diag2(kt[rows, :hd], kt[rows, hd:]))
                      for rows in blocks]

        b_lasts = [b[rows][HGRN_CHUNK - 1:HGRN_CHUNK] for rows in chunks]
        upds = [_dot_tn(vb[rows], (kk[rows] * jnp.exp(bl - b[rows])).astype(BF16))
                for rows, bl in zip(chunks, b_lasts)]
        ch = proj(6)

        if stable:
            intra = _intra_chunk_by_offset(qf, kk, b, v)
        else:
            intra = jnp.concatenate(
                [_dot(jnp.where(keep, sc, 0.0).astype(BF16),
                      _block_diag2(vb[rows, :hd], vb[rows, hd:]))
                 for rows, sc in zip(blocks, scores)], axis=0)

        st0 = state_scr[2 * p]
        st1 = state_scr[2 * p + 1]
        starts = []
        for upd, bl in zip(upds, b_lasts):
            starts.append(_block_diag2(st0.astype(BF16), st1.astype(BF16)))
            decay = jnp.exp(bl)
            st0 = st0 * decay[:, :hd] + upd[:hd, :hd]
            st1 = st1 * decay[:, hd:] + upd[hd:, hd:]
        state_scr[2 * p] = st0
        state_scr[2 * p + 1] = st1
        inter = [_dot_nt(qt[rows], start) for rows, start in zip(chunks, starts)]

        u = cc * ch
        prev8 = carry_scr[p]
        carry_scr[p] = u[tm - SUBLANES:]
        convs.append((cb * _causal_conv3(u, prev8, scw[:, lanes])).astype(BF16))

        o = intra + jnp.concatenate(inter, axis=0)
        o = jnp.concatenate([_rms(o[:, :hd], hnw), _rms(o[:, hd:], hnw)], axis=1)
        gated.append((o * _silu(g)).astype(BF16))
        b_mins.append(functools.reduce(jnp.minimum, b_lasts))

    mixed = jnp.concatenate(gated + convs, axis=1)
    o_ref[...] = x_ref[...] + _dot(mixed, wout_ref[...])

    b_min = jnp.concatenate(b_mins, axis=1)

    @pl.when(i == 0)
    def _():
        bmin_ref[...] = b_min

    @pl.when(i > 0)
    def _():
        bmin_ref[...] = jnp.minimum(bmin_ref[...], b_min)


def _mixer(x2d, norm_w, w_in, w_out, hgrn_lb, hgrn_norm_w, sconv_w, to_cast=(), *,
           seq_len, layer, stable):
    m, d = x2d.shape
    tm = MIXER_ROW_TILE
    width = hgrn_lb.shape[1]
    n_tiles = m // tm
    cast_specs = _cast_specs(to_cast, n_tiles)
    kern = functools.partial(_mixer_kernel, tiles_per_seq=seq_len // tm, layer=layer,
                             stable=stable, n_cast=len(to_cast))
    out, b_min, *casted = pl.pallas_call(
        kern,
        grid=(n_tiles,),
        in_specs=[
            pl.BlockSpec((tm, d), lambda i: (i, 0)),
            _resident((1, d)),
            _resident(w_in.shape),
            _resident(w_out.shape),
            _resident(hgrn_lb.shape),
            _resident((1, LANES)),
            _resident(sconv_w.shape),
            *cast_specs,
        ],
        out_specs=[pl.BlockSpec((tm, d), lambda i: (i, 0)),
                   pl.BlockSpec((1, width), lambda i: (0, 0)),
                   *cast_specs],
        out_shape=[jax.ShapeDtypeStruct((m, d), F32),
                   jax.ShapeDtypeStruct((1, width), F32),
                   *(jax.ShapeDtypeStruct(w.shape, BF16) for w in to_cast)],
        scratch_shapes=[
            pltpu.VMEM((tm, d), BF16),
            pltpu.VMEM((HGRN_HEADS, LANES, LANES), F32),
            pltpu.VMEM((width // (2 * LANES), SUBLANES, 2 * LANES), F32),
        ],
        compiler_params=pltpu.CompilerParams(
            dimension_semantics=("arbitrary",),
            vmem_limit_bytes=MIXER_VMEM_LIMIT_BYTES),
        name="hybrid_mixer_stable" if stable else "hybrid_mixer",
    )(x2d, norm_w, w_in, w_out, hgrn_lb, hgrn_norm_w, sconv_w, *to_cast)
    return out, b_min, casted


def _attn_kernel(x_ref, nw_ref, wq_ref, k_ref, v_ref, wo_ref, *rest, head_dim, n_cast):
    cast_in, o_ref = rest[:n_cast], rest[n_cast]
    cast_out, h_scr = rest[n_cast + 1:2 * n_cast + 1], rest[2 * n_cast + 1]
    h = _normalised_tile(x_ref, nw_ref, h_scr)
    scale = head_dim ** -0.5
    cols = [slice(hh * head_dim, (hh + 1) * head_dim) for hh in range(MEM_HEADS)]
    qs = [_dot(h, wq_ref[:, c]).astype(BF16) for c in cols]
    _cast_blocks(cast_in, cast_out)
    ss = [_dot_nt(q, k_ref[:, c]) * scale for q, c in zip(qs, cols)]
    out = x_ref[...]
    for s, c in zip(ss, cols):
        e = jnp.exp(s - jnp.max(s, axis=-1, keepdims=True))
        pr = (e / jnp.sum(e, axis=-1, keepdims=True)).astype(BF16)
        out = out + _dot(_dot(pr, v_ref[:, c]).astype(BF16), wo_ref[c, :])
    o_ref[...] = out


def _attn(x2d, norm_w, wq, kv, wo, to_cast, *, seq_len, mem_len):
    m, d = x2d.shape
    tm = ATTN_ROW_TILE
    tiles_per_seq = seq_len // tm
    n_tiles = m // tm

    cast_specs = _cast_specs(to_cast, n_tiles)
    kern = functools.partial(_attn_kernel, head_dim=d // MEM_HEADS, n_cast=len(to_cast))
    out, *casted = pl.pallas_call(
        kern,
        grid=(n_tiles,),
        in_specs=[
            pl.BlockSpec((tm, d), lambda i: (i, 0)),
            _resident((1, d)),
            _resident((d, d)),
            pl.BlockSpec((None, None, mem_len, d), lambda i: (0, i // tiles_per_seq, 0, 0)),
            pl.BlockSpec((None, None, mem_len, d), lambda i: (1, i // tiles_per_seq, 0, 0)),
            _resident((d, d)),
            *cast_specs,
        ],
        out_specs=[pl.BlockSpec((tm, d), lambda i: (i, 0)), *cast_specs],
        out_shape=[jax.ShapeDtypeStruct((m, d), F32),
                   *(jax.ShapeDtypeStruct(w.shape, BF16) for w in to_cast)],
        scratch_shapes=[pltpu.VMEM((tm, d), BF16)],
        compiler_params=pltpu.CompilerParams(
            dimension_semantics=("arbitrary",),
            vmem_limit_bytes=ATTN_VMEM_LIMIT_BYTES),
        name="mem_cross_attn",
    )(x2d, norm_w, wq, kv, kv, wo, *to_cast)
    return out, casted


def _ffn_kernel(x_ref, nw_ref, wg_ref, wu_ref, cw_ref, cb_ref, wd_ref, fnw_ref, o_ref,
                h_scr, carry_scr, *, tiles_per_seq, n_steps, final_norm):
    i = pl.program_id(0)
    f = pl.program_id(1)
    tm = x_ref.shape[0]

    @pl.when(f == 0)
    def _():
        xv = x_ref[...]
        h_scr[...] = _rms(xv, nw_ref[...]).astype(BF16)
        o_ref[...] = xv

    @pl.when(i % tiles_per_seq == 0)
    def _():
        carry_scr[f] = jnp.zeros(carry_scr.shape[1:], F32)

    h = h_scr[...]
    a = _dot(h, wg_ref[...])
    up = _dot(h, wu_ref[...])
    prev8 = carry_scr[f]
    carry_scr[f] = a[tm - SUBLANES:]
    act = _silu(_causal_conv3(a, prev8, cw_ref[...]) + cb_ref[...]) * up
    o_ref[...] += _dot(act.astype(BF16), wd_ref[...])

    if final_norm:
        @pl.when(f == n_steps - 1)
        def _():
            o_ref[...] = _rms(o_ref[...], fnw_ref[...])


def _ffn(x2d, norm_w, w_gate, w_up, conv_w, conv_b, w_down, final_w, *, seq_len, final_norm):
    m, d = x2d.shape
    d_ff = w_gate.shape[1]
    tm, tf = FFN_ROW_TILE, FFN_TILE
    nf = d_ff // tf
    n_tiles = m // tm
    kern = functools.partial(_ffn_kernel, tiles_per_seq=seq_len // tm, n_steps=nf,
                             final_norm=final_norm)

    def x_index(i, f):
        return jnp.minimum(i + (f >= nf // 2).astype(jnp.int32), n_tiles - 1), 0

    return pl.pallas_call(
        kern,
        grid=(n_tiles, nf),
        in_specs=[
            pl.BlockSpec((tm, d), x_index),
            pl.BlockSpec((1, d), lambda i, f: (0, 0)),
            pl.BlockSpec((d, tf), lambda i, f: (0, f)),
            pl.BlockSpec((d, tf), lambda i, f: (0, f)),
            pl.BlockSpec((CONV_TAPS, tf), lambda i, f: (0, f)),
            pl.BlockSpec((1, tf), lambda i, f: (0, f)),
            pl.BlockSpec((tf, d), lambda i, f: (f, 0)),
            pl.BlockSpec((1, d), lambda i, f: (0, 0)),
        ],
        out_specs=pl.BlockSpec((tm, d), lambda i, f: (i, 0)),
        out_shape=jax.ShapeDtypeStruct((m, d), F32),
        scratch_shapes=[
            pltpu.VMEM((tm, d), BF16),
            pltpu.VMEM((nf, SUBLANES, tf), F32),
        ],
        compiler_params=pltpu.CompilerParams(
            dimension_semantics=("arbitrary", "arbitrary"),
            vmem_limit_bytes=FFN_VMEM_LIMIT_BYTES),
        name="conv_ffn",
    )(x2d, norm_w, w_gate, w_up, conv_w, conv_b, w_down, final_w)


def kernel(x, mem, hgrn_lb, norm1_w, w_in, hgrn_norm_w, sconv_w, w_out, norm2_w, mem_norm_w,
           wq, wk, wv, wo, norm3_w, w_gate, w_up, ffn_conv_w, ffn_conv_b, w_down, final_norm_w):
    batch, seq_len, d = x.shape
    mem_len = mem.shape[1]
    depth = norm1_w.shape[0]
    assert all(seq_len % t == 0 for t in (MIXER_ROW_TILE, ATTN_ROW_TILE, FFN_ROW_TILE))
    assert MIXER_ROW_TILE % HGRN_BLOCK == 0 and HGRN_BLOCK % HGRN_CHUNK == 0
    assert hgrn_lb.shape[1] == HGRN_HEADS * LANES and HGRN_HEADS % 2 == 0
    assert w_in.shape[2] == N_PROJ_GROUPS * HGRN_HEADS * LANES and w_gate.shape[2] % FFN_TILE == 0

    xs = x.reshape(batch * seq_len, d)
    mem2d = mem.reshape(batch * mem_len, d)
    row = lambda w: w.reshape(1, -1)
    for l in range(depth):
        mixer = functools.partial(
            _mixer, xs, row(norm1_w[l]), w_in[l].astype(BF16), w_out[l].astype(BF16), hgrn_lb,
            row(hgrn_norm_w[l]), sconv_w[l], seq_len=seq_len, layer=l)
        fast, b_min, (wq_b, wo_b) = mixer((wq[l], wo[l]), stable=False)
        xs = lax.cond(jnp.min(b_min) < -HGRN_SAFE_LOG_DECAY,
                      lambda: mixer(stable=True)[0], lambda: fast)
        kv = _kv_proj(mem2d, row(mem_norm_w[l]), wk[l], wv[l])
        kv = kv.reshape(2, batch, mem_len, d)
        xs, (wg_b, wu_b, wd_b) = _attn(
            xs, row(norm2_w[l]), wq_b, kv, wo_b,
            (w_gate[l], w_up[l], w_down[l]), seq_len=seq_len, mem_len=mem_len)
        xs = _ffn(xs, row(norm3_w[l]), wg_b, wu_b, ffn_conv_w[l], row(ffn_conv_b[l]), wd_b,
                  row(final_norm_w), seq_len=seq_len, final_norm=(l == depth - 1))
    return xs.reshape(batch, seq_len, d)
```

```python
import functools

import jax
import jax.numpy as jnp
from jax import lax
from jax.experimental import pallas as pl
from jax.experimental.pallas import tpu as pltpu

F32 = jnp.float32
BF16 = jnp.bfloat16

EPS = 1e-6
HGRN_HEADS = 8
HGRN_CHUNK = 64
HGRN_SAFE_LOG_DECAY = 75.0
HGRN_BLOCK = 128
CONV_TAPS = 3
MEM_HEADS = 4
N_PROJ_GROUPS = 7
LANES = 128
SUBLANES = 8
BF16_ROWS = 16
KV_VMEM_LIMIT_BYTES = 56 * 1024 * 1024
MIXER_VMEM_LIMIT_BYTES = 60 * 1024 * 1024
FFN_VMEM_LIMIT_BYTES = 60 * 1024 * 1024
ATTN_VMEM_LIMIT_BYTES = 62 * 1024 * 1024

MIXER_ROW_TILE = 256
ATTN_ROW_TILE = 512
FFN_ROW_TILE = 1024
FFN_TILE = 512
FFN_PREP_ROWS = 128


def _rms(x, w):
    ms = jnp.mean(x * x, axis=-1, keepdims=True)
    return x * lax.rsqrt(ms + EPS) * w


def _silu(x):
    return x * jax.nn.sigmoid(x)


def _dot(a, b):
    return jnp.dot(a, b, preferred_element_type=F32)


def _dot_nt(a, b):
    return lax.dot_general(a, b, (((1,), (1,)), ((), ())), preferred_element_type=F32)


def _dot_tn(a, b):
    return lax.dot_general(a, b, (((0,), (0,)), ((), ())), preferred_element_type=F32)


def _shift_rows(cur, prev8, shift):
    rolled = pltpu.roll(cur, shift, axis=0)
    head = pltpu.roll(prev8, shift, axis=0)
    rid = lax.broadcasted_iota(jnp.int32, prev8.shape, 0)
    first = jnp.where(rid < shift, head, rolled[:SUBLANES])
    return jnp.concatenate([first, rolled[SUBLANES:]], axis=0)


def _causal_conv3(cur, prev8, w):
    return (w[0:1] * _shift_rows(cur, prev8, 2)
            + w[1:2] * _shift_rows(cur, prev8, 1)
            + w[2:3] * cur)


def _normalised_tile(x_ref, nw_ref, h_scr):
    h_scr[...] = _rms(x_ref[...], nw_ref[...]).astype(BF16)
    return h_scr[...]


def _cast_specs(to_cast, n_steps):
    def spec(w):
        rows = w.shape[0]
        rb = pl.cdiv(pl.cdiv(rows, n_steps), BF16_ROWS) * BF16_ROWS
        while rows % rb:
            rb += BF16_ROWS
        last = rows // rb - 1
        return pl.BlockSpec((rb, w.shape[1]), lambda i: (jnp.minimum(i, last), 0))

    return [spec(w) for w in to_cast]


def _cast_blocks(src_refs, dst_refs):
    for src, dst in zip(src_refs, dst_refs):
        dst[...] = src[...].astype(dst.dtype)


def _resident(shape):
    return pl.BlockSpec(shape, lambda i: (0,) * len(shape), pipeline_mode=pl.Buffered(1))


def _kv_kernel(mem_ref, nw_ref, wk_ref, wv_ref, o_ref, h_scr):
    @pl.when(pl.program_id(0) == 0)
    def _():
        h_scr[...] = _rms(mem_ref[...], nw_ref[...]).astype(BF16)

    h = h_scr[...]
    o_ref[0] = _dot(h, wk_ref[...].astype(BF16)).astype(o_ref.dtype)
    o_ref[1] = _dot(h, wv_ref[...].astype(BF16)).astype(o_ref.dtype)


def _kv_proj(mem2d, norm_w, wk, wv):
    rows, d = mem2d.shape
    tn = 512
    return pl.pallas_call(
        _kv_kernel,
        grid=(d // tn,),
        in_specs=[
            pl.BlockSpec((rows, d), lambda n: (0, 0)),
            pl.BlockSpec((1, d), lambda n: (0, 0)),
            pl.BlockSpec((d, tn), lambda n: (0, n)),
            pl.BlockSpec((d, tn), lambda n: (0, n)),
        ],
        out_specs=pl.BlockSpec((2, rows, tn), lambda n: (0, 0, n)),
        out_shape=jax.ShapeDtypeStruct((2, rows, d), BF16),
        scratch_shapes=[pltpu.VMEM((rows, d), BF16)],
        compiler_params=pltpu.CompilerParams(
            dimension_semantics=("arbitrary",),
            vmem_limit_bytes=KV_VMEM_LIMIT_BYTES),
        name="kv_proj",
    )(mem2d, norm_w, wk, wv)


def _chunk_cumsum(x):
    pos = lax.broadcasted_iota(jnp.int32, x.shape, 0) % HGRN_CHUNK
    s = 1
    while s < HGRN_CHUNK:
        x = x + jnp.where(pos >= s, pltpu.roll(x, s, axis=0), 0.0)
        s *= 2
    return x


def _block_diag2(a, b):
    z = jnp.zeros(a.shape, a.dtype)
    return jnp.concatenate([jnp.concatenate([a, z], axis=1),
                            jnp.concatenate([z, b], axis=1)], axis=0)


def _intra_chunk_by_offset(qf, kk, b, v):
    pos = lax.broadcasted_iota(jnp.int32, qf.shape, 0) % HGRN_CHUNK
    ones = jnp.ones((LANES, LANES), BF16)
    head_sum = _block_diag2(ones, ones)

    def one_offset(d, acc):
        in_chunk = pos >= d
        decay = jnp.exp(jnp.where(in_chunk, b - pltpu.roll(b, d, axis=0), 0.0))
        w = jnp.where(in_chunk, qf * pltpu.roll(kk, d, axis=0) * decay, 0.0)
        return acc + _dot(w.astype(BF16), head_sum) * pltpu.roll(v, d, axis=0)

    return lax.fori_loop(0, HGRN_CHUNK, one_offset, jnp.zeros(qf.shape, F32))


def _mixer_kernel(x_ref, nw_ref, win_ref, wout_ref, lb_ref, hnw_ref, scw_ref, *rest,
                  tiles_per_seq, layer, stable, n_cast):
    cast_in, (o_ref, bmin_ref) = rest[:n_cast], rest[n_cast:n_cast + 2]
    cast_out = rest[n_cast + 2:2 * n_cast + 2]
    h_scr, state_scr, carry_scr = rest[2 * n_cast + 2:]
    _cast_blocks(cast_in, cast_out)
    i = pl.program_id(0)
    tm = x_ref.shape[0]
    hd = LANES
    gw = 2 * hd
    width = lb_ref.shape[1]
    n_pairs = width // gw
    n_chunks = tm // HGRN_CHUNK
    chunks = [slice(c * HGRN_CHUNK, (c + 1) * HGRN_CHUNK) for c in range(n_chunks)]
    blocks = [slice(r0, r0 + HGRN_BLOCK) for r0 in range(0, tm, HGRN_BLOCK)]

    @pl.when(i % tiles_per_seq == 0)
    def _():
        state_scr[...] = jnp.zeros(state_scr.shape, F32)
        carry_scr[...] = jnp.zeros(carry_scr.shape, F32)

    lb_all = jnp.sum(jax.nn.softmax(lb_ref[...], axis=0)[0:layer + 1], axis=0, keepdims=True)
    hnw = hnw_ref[...]
    scw = scw_ref[...]
    if not stable:
        t_id = lax.broadcasted_iota(jnp.int32, (HGRN_BLOCK, 2 * HGRN_BLOCK), 0)
        s_id = lax.broadcasted_iota(jnp.int32, (HGRN_BLOCK, 2 * HGRN_BLOCK), 1) % HGRN_BLOCK
        keep = ((t_id // HGRN_CHUNK) == (s_id // HGRN_CHUNK)) & (s_id <= t_id)

    h = _normalised_tile(x_ref, nw_ref, h_scr)
    gated, convs, b_mins = [], [], []
    for p in range(n_pairs):
        lanes = slice(p * gw, (p + 1) * gw)

        def proj(group):
            return _dot(h, win_ref[:, group * width + p * gw:group * width + (p + 1) * gw])

        f_pre = proj(1)
        q = proj(0)
        lb = lb_all[:, lanes]
        f = lb + (1.0 - lb) * jax.nn.sigmoid(f_pre)
        logf = jnp.log(f)
        kk = 1.0 - f
        qf = _silu(q)
        b = _chunk_cumsum(logf)
        v = proj(2)
        vb = v.astype(BF16)
        g = proj(3)
        cb = proj(4)
        cc = proj(5)
        qt = (qf * jnp.exp(b)).astype(BF16)

        if not stable:
            kt = (kk * jnp.exp(-b)).astype(BF16)
            scores = [_dot_nt(qt[rows], _block_diag2(kt[rows, :hd], kt[rows, hd:]))
                      for rows in blocks]

        b_lasts = [b[rows][HGRN_CHUNK - 1:HGRN_CHUNK] for rows in chunks]
        upds = [_dot_tn(vb[rows], (kk[rows] * jnp.exp(bl - b[rows])).astype(BF16))
                for rows, bl in zip(chunks, b_lasts)]
        ch = proj(6)

        if stable:
            intra = _intra_chunk_by_offset(qf, kk, b, v)
        else:
            intra = jnp.concatenate(
                [_dot(jnp.where(keep, sc, 0.0).astype(BF16),
                      _block_diag2(vb[rows, :hd], vb[rows, hd:]))
                 for rows, sc in zip(blocks, scores)], axis=0)

        st0 = state_scr[2 * p]
        st1 = state_scr[2 * p + 1]
        starts = []
        for upd, bl in zip(upds, b_lasts):
            starts.append(_block_diag2(st0.astype(BF16), st1.astype(BF16)))
            decay = jnp.exp(bl)
            st0 = st0 * decay[:, :hd] + upd[:hd, :hd]
            st1 = st1 * decay[:, hd:] + upd[hd:, hd:]
        state_scr[2 * p] = st0
        state_scr[2 * p + 1] = st1
        inter = [_dot_nt(qt[rows], start) for rows, start in zip(chunks, starts)]

        u = cc * ch
        prev8 = carry_scr[p]
        carry_scr[p] = u[tm - SUBLANES:]
        convs.append((cb * _causal_conv3(u, prev8, scw[:, lanes])).astype(BF16))

        o = intra + jnp.concatenate(inter, axis=0)
        o = jnp.concatenate([_rms(o[:, :hd], hnw), _rms(o[:, hd:], hnw)], axis=1)
        gated.append((o * _silu(g)).astype(BF16))
        b_mins.append(functools.reduce(jnp.minimum, b_lasts))

    mixed = jnp.concatenate(gated + convs, axis=1)
    o_ref[...] = x_ref[...] + _dot(mixed, wout_ref[...])

    b_min = jnp.concatenate(b_mins, axis=1)

    @pl.when(i == 0)
    def _():
        bmin_ref[...] = b_min

    @pl.when(i > 0)
    def _():
        bmin_ref[...] = jnp.minimum(bmin_ref[...], b_min)


def _mixer(x2d, norm_w, w_in, w_out, hgrn_lb, hgrn_norm_w, sconv_w, to_cast=(), *,
           seq_len, layer, stable):
    m, d = x2d.shape
    tm = MIXER_ROW_TILE
    width = hgrn_lb.shape[1]
    n_tiles = m // tm
    cast_specs = _cast_specs(to_cast, n_tiles)
    kern = functools.partial(_mixer_kernel, tiles_per_seq=seq_len // tm, layer=layer,
                             stable=stable, n_cast=len(to_cast))
    out, b_min, *casted = pl.pallas_call(
        kern,
        grid=(n_tiles,),
        in_specs=[
            pl.BlockSpec((tm, d), lambda i: (i, 0)),
            _resident((1, d)),
            _resident(w_in.shape),
            _resident(w_out.shape),
            _resident(hgrn_lb.shape),
            _resident((1, LANES)),
            _resident(sconv_w.shape),
            *cast_specs,
        ],
        out_specs=[pl.BlockSpec((tm, d), lambda i: (i, 0)),
                   pl.BlockSpec((1, width), lambda i: (0, 0)),
                   *cast_specs],
        out_shape=[jax.ShapeDtypeStruct((m, d), F32),
                   jax.ShapeDtypeStruct((1, width), F32),
                   *(jax.ShapeDtypeStruct(w.shape, BF16) for w in to_cast)],
        scratch_shapes=[
            pltpu.VMEM((tm, d), BF16),
            pltpu.VMEM((HGRN_HEADS, LANES, LANES), F32),
            pltpu.VMEM((width // (2 * LANES), SUBLANES, 2 * LANES), F32),
        ],
        compiler_params=pltpu.CompilerParams(
            dimension_semantics=("arbitrary",),
            vmem_limit_bytes=MIXER_VMEM_LIMIT_BYTES),
        name="hybrid_mixer_stable" if stable else "hybrid_mixer",
    )(x2d, norm_w, w_in, w_out, hgrn_lb, hgrn_norm_w, sconv_w, *to_cast)
    return out, b_min, casted


def _attn_kernel(x_ref, nw_ref, wq_ref, k_ref, v_ref, wo_ref, *rest, head_dim, n_cast):
    cast_in, o_ref = rest[:n_cast], rest[n_cast]
    cast_out, h_scr = rest[n_cast + 1:2 * n_cast + 1], rest[2 * n_cast + 1]
    h = _normalised_tile(x_ref, nw_ref, h_scr)
    scale = head_dim ** -0.5
    cols = [slice(hh * head_dim, (hh + 1) * head_dim) for hh in range(MEM_HEADS)]
    qs = [_dot(h, wq_ref[:, c]).astype(BF16) for c in cols]
    _cast_blocks(cast_in, cast_out)
    ss = [_dot_nt(q, k_ref[:, c]) * scale for q, c in zip(qs, cols)]
    out = x_ref[...]
    for s, c in zip(ss, cols):
        e = jnp.exp(s - jnp.max(s, axis=-1, keepdims=True))
        pr = (e / jnp.sum(e, axis=-1, keepdims=True)).astype(BF16)
        out = out + _dot(_dot(pr, v_ref[:, c]).astype(BF16), wo_ref[c, :])
    o_ref[...] = out


def _attn(x2d, norm_w, wq, kv, wo, to_cast, *, seq_len, mem_len):
    m, d = x2d.shape
    tm = ATTN_ROW_TILE
    tiles_per_seq = seq_len // tm
    n_tiles = m // tm

    cast_specs = _cast_specs(to_cast, n_tiles)
    kern = functools.partial(_attn_kernel, head_dim=d // MEM_HEADS, n_cast=len(to_cast))
    out, *casted = pl.pallas_call(
        kern,
        grid=(n_tiles,),
        in_specs=[
            pl.BlockSpec((tm, d), lambda i: (i, 0)),
            _resident((1, d)),
            _resident((d, d)),
            pl.BlockSpec((None, None, mem_len, d), lambda i: (0, i // tiles_per_seq, 0, 0)),
            pl.BlockSpec((None, None, mem_len, d), lambda i: (1, i // tiles_per_seq, 0, 0)),
            _resident((d, d)),
            *cast_specs,
        ],
        out_specs=[pl.BlockSpec((tm, d), lambda i: (i, 0)), *cast_specs],
        out_shape=[jax.ShapeDtypeStruct((m, d), F32),
                   *(jax.ShapeDtypeStruct(w.shape, BF16) for w in to_cast)],
        scratch_shapes=[pltpu.VMEM((tm, d), BF16)],
        compiler_params=pltpu.CompilerParams(
            dimension_semantics=("arbitrary",),
            vmem_limit_bytes=ATTN_VMEM_LIMIT_BYTES),
        name="mem_cross_attn",
    )(x2d, norm_w, wq, kv, kv, wo, *to_cast)
    return out, casted


def _ffn_kernel(x_ref, nw_ref, wg_hbm, wu_hbm, cw_ref, cb_ref, wd_hbm, fnw_ref, o_ref,
                h_scr, wg_buf, wu_buf, wd_buf, sems, carry_scr, *,
                tiles_per_seq, n_tiles, final_norm):
    i = pl.program_id(0)
    tm = x_ref.shape[0]
    nf, _, tf = cw_ref.shape

    def weight_copies(f, slot):
        cols = pl.ds(f * tf if isinstance(f, int) else pl.multiple_of(f * tf, tf), tf)
        return (pltpu.make_async_copy(wg_hbm.at[:, cols], wg_buf.at[slot], sems.at[0, slot]),
                pltpu.make_async_copy(wu_hbm.at[:, cols], wu_buf.at[slot], sems.at[1, slot]),
                pltpu.make_async_copy(wd_hbm.at[cols, :], wd_buf.at[slot], sems.at[2, slot]))

    @pl.when(i == 0)
    def _():
        for copy in weight_copies(0, 0):
            copy.start()

    @pl.when(i % tiles_per_seq == 0)
    def _():
        carry_scr[...] = jnp.zeros(carry_scr.shape, F32)

    def prepare_rows(r, carry):
        rows = pl.ds(pl.multiple_of(r * FFN_PREP_ROWS, FFN_PREP_ROWS), FFN_PREP_ROWS)
        xv = x_ref[rows, :]
        h_scr[rows, :] = _rms(xv, nw_ref[...]).astype(BF16)
        o_ref[rows, :] = xv
        return carry

    lax.fori_loop(0, tm // FFN_PREP_ROWS, prepare_rows, 0)

    def column_tile(f, carry):
        g = i * nf + f
        slot = g % 2
        last_f = f == nf - 1

        @pl.when(jnp.logical_not(last_f & (i == n_tiles - 1)))
        def _():
            for copy in weight_copies(jnp.where(last_f, 0, f + 1), 1 - slot):
                copy.start()

        for copy in weight_copies(f, slot):
            copy.wait()
        h = h_scr[...]
        a = _dot(h, wg_buf[slot])
        up = _dot(h, wu_buf[slot])
        prev8 = carry_scr[f]
        carry_scr[f] = a[tm - SUBLANES:]
        act = _silu(_causal_conv3(a, prev8, cw_ref[f]) + cb_ref[f]) * up
        o_ref[...] += _dot(act.astype(BF16), wd_buf[slot])
        return carry

    lax.fori_loop(0, nf, column_tile, 0)
    if final_norm:
        def normalise_rows(r, carry):
            rows = pl.ds(pl.multiple_of(r * FFN_PREP_ROWS, FFN_PREP_ROWS), FFN_PREP_ROWS)
            o_ref[rows, :] = _rms(o_ref[rows, :], fnw_ref[...])
            return carry

        lax.fori_loop(0, tm // FFN_PREP_ROWS, normalise_rows, 0)


def _ffn(x2d, norm_w, w_gate, w_up, conv_w, conv_b, w_down, final_w, *, seq_len, final_norm):
    m, d = x2d.shape
    d_ff = w_gate.shape[1]
    tm, tf = FFN_ROW_TILE, FFN_TILE
    nf = d_ff // tf
    n_tiles = m // tm
    conv_w = conv_w.reshape(CONV_TAPS, nf, tf).transpose(1, 0, 2)
    conv_b = conv_b.reshape(nf, 1, tf)
    kern = functools.partial(_ffn_kernel, tiles_per_seq=seq_len // tm, n_tiles=n_tiles,
                             final_norm=final_norm)
    hbm = pl.BlockSpec(memory_space=pl.ANY)
    return pl.pallas_call(
        kern,
        grid=(n_tiles,),
        in_specs=[
            pl.BlockSpec((tm, d), lambda i: (i, 0)),
            _resident((1, d)),
            hbm, hbm,
            _resident(conv_w.shape),
            _resident(conv_b.shape),
            hbm,
            _resident((1, d)),
        ],
        out_specs=pl.BlockSpec((tm, d), lambda i: (i, 0)),
        out_shape=jax.ShapeDtypeStruct((m, d), F32),
        scratch_shapes=[
            pltpu.VMEM((tm, d), BF16),
            pltpu.VMEM((2, d, tf), BF16),
            pltpu.VMEM((2, d, tf), BF16),
            pltpu.VMEM((2, tf, d), BF16),
            pltpu.SemaphoreType.DMA((3, 2)),
            pltpu.VMEM((nf, SUBLANES, tf), F32),
        ],
        compiler_params=pltpu.CompilerParams(
            dimension_semantics=("arbitrary",),
            vmem_limit_bytes=FFN_VMEM_LIMIT_BYTES),
        name="conv_ffn",
    )(x2d, norm_w, w_gate, w_up, conv_w, conv_b, w_down, final_w)


def kernel(x, mem, hgrn_lb, norm1_w, w_in, hgrn_norm_w, sconv_w, w_out, norm2_w, mem_norm_w,
           wq, wk, wv, wo, norm3_w, w_gate, w_up, ffn_conv_w, ffn_conv_b, w_down, final_norm_w):
    batch, seq_len, d = x.shape
    mem_len = mem.shape[1]
    depth = norm1_w.shape[0]
    assert all(seq_len % t == 0 for t in (MIXER_ROW_TILE, ATTN_ROW_TILE, FFN_ROW_TILE))
    assert MIXER_ROW_TILE % HGRN_BLOCK == 0 and HGRN_BLOCK % HGRN_CHUNK == 0
    assert hgrn_lb.shape[1] == HGRN_HEADS * LANES and HGRN_HEADS % 2 == 0
    assert w_in.shape[2] == N_PROJ_GROUPS * HGRN_HEADS * LANES and w_gate.shape[2] % FFN_TILE == 0

    xs = x.reshape(batch * seq_len, d)
    mem2d = mem.reshape(batch * mem_len, d)
    row = lambda w: w.reshape(1, -1)
    for l in range(depth):
        mixer = functools.partial(
            _mixer, xs, row(norm1_w[l]), w_in[l].astype(BF16), w_out[l].astype(BF16), hgrn_lb,
            row(hgrn_norm_w[l]), sconv_w[l], seq_len=seq_len, layer=l)
        fast, b_min, (wq_b, wo_b) = mixer((wq[l], wo[l]), stable=False)
        xs = lax.cond(jnp.min(b_min) < -HGRN_SAFE_LOG_DECAY,
                      lambda: mixer(stable=True)[0], lambda: fast)
        kv = _kv_proj(mem2d, row(mem_norm_w[l]), wk[l], wv[l])
        kv = kv.reshape(2, batch, mem_len, d)
        xs, (wg_b, wu_b, wd_b) = _attn(
            xs, row(norm2_w[l]), wq_b, kv, wo_b,
            (w_gate[l], w_up[l], w_down[l]), seq_len=seq_len, mem_len=mem_len)
        xs = _ffn(xs, row(norm3_w[l]), wg_b, wu_b, ffn_conv_w[l], row(ffn_conv_b[l]), wd_b,
                  row(final_norm_w), seq_len=seq_len, final_norm=(l == depth - 1))
    return xs.reshape(batch, seq_len, d)
```

```python
import functools

import jax
import jax.numpy as jnp
from jax import lax
from jax.experimental import pallas as pl
from jax.experimental.pallas import tpu as pltpu

F32 = jnp.float32
BF16 = jnp.bfloat16

EPS = 1e-6
HGRN_HEADS = 8
HGRN_CHUNK = 64
HGRN_SAFE_LOG_DECAY = 75.0
HGRN_BLOCK = 128
CONV_TAPS = 3
MEM_HEADS = 4
N_PROJ_GROUPS = 7
LANES = 128
SUBLANES = 8
BF16_ROWS = 16
KV_VMEM_LIMIT_BYTES = 56 * 1024 * 1024
MIXER_VMEM_LIMIT_BYTES = 60 * 1024 * 1024
FFN_VMEM_LIMIT_BYTES = 60 * 1024 * 1024
ATTN_VMEM_LIMIT_BYTES = 62 * 1024 * 1024

MIXER_ROW_TILE = 256
ATTN_ROW_TILE = 512
FFN_ROW_TILE = 1024
FFN_TILE = 512
FFN_PREP_ROWS = 128


def _rms(x, w):
    ms = jnp.mean(x * x, axis=-1, keepdims=True)
    return x * lax.rsqrt(ms + EPS) * w


def _silu(x):
    return x * jax.nn.sigmoid(x)


def _dot(a, b):
    return jnp.dot(a, b, preferred_element_type=F32)


def _dot_nt(a, b):
    return lax.dot_general(a, b, (((1,), (1,)), ((), ())), preferred_element_type=F32)


def _dot_tn(a, b):
    return lax.dot_general(a, b, (((0,), (0,)), ((), ())), preferred_element_type=F32)


def _shift_rows(cur, prev8, shift):
    rolled = pltpu.roll(cur, shift, axis=0)
    head = pltpu.roll(prev8, shift, axis=0)
    rid = lax.broadcasted_iota(jnp.int32, prev8.shape, 0)
    first = jnp.where(rid < shift, head, rolled[:SUBLANES])
    return jnp.concatenate([first, rolled[SUBLANES:]], axis=0)


def _causal_conv3(cur, prev8, w):
    return (w[0:1] * _shift_rows(cur, prev8, 2)
            + w[1:2] * _shift_rows(cur, prev8, 1)
            + w[2:3] * cur)


def _normalised_tile(x_ref, nw_ref, h_scr):
    h_scr[...] = _rms(x_ref[...], nw_ref[...]).astype(BF16)
    return h_scr[...]


def _cast_specs(to_cast, n_steps):
    def spec(w):
        rows = w.shape[0]
        rb = pl.cdiv(pl.cdiv(rows, n_steps), BF16_ROWS) * BF16_ROWS
        while rows % rb:
            rb += BF16_ROWS
        last = rows // rb - 1
        return pl.BlockSpec((rb, w.shape[1]), lambda i: (jnp.minimum(i, last), 0))

    return [spec(w) for w in to_cast]


def _cast_blocks(src_refs, dst_refs):
    for src, dst in zip(src_refs, dst_refs):
        dst[...] = src[...].astype(dst.dtype)


def _resident(shape):
    return pl.BlockSpec(shape, lambda i: (0,) * len(shape), pipeline_mode=pl.Buffered(1))


def _kv_kernel(mem_ref, nw_ref, wk_ref, wv_ref, o_ref, h_scr):
    @pl.when(pl.program_id(0) == 0)
    def _():
        h_scr[...] = _rms(mem_ref[...], nw_ref[...]).astype(BF16)

    h = h_scr[...]
    o_ref[0] = _dot(h, wk_ref[...].astype(BF16)).astype(o_ref.dtype)
    o_ref[1] = _dot(h, wv_ref[...].astype(BF16)).astype(o_ref.dtype)


def _kv_proj(mem2d, norm_w, wk, wv):
    rows, d = mem2d.shape
    tn = 512
    return pl.pallas_call(
        _kv_kernel,
        grid=(d // tn,),
        in_specs=[
            pl.BlockSpec((rows, d), lambda n: (0, 0)),
            pl.BlockSpec((1, d), lambda n: (0, 0)),
            pl.BlockSpec((d, tn), lambda n: (0, n)),
            pl.BlockSpec((d, tn), lambda n: (0, n)),
        ],
        out_specs=pl.BlockSpec((2, rows, tn), lambda n: (0, 0, n)),
        out_shape=jax.ShapeDtypeStruct((2, rows, d), BF16),
        scratch_shapes=[pltpu.VMEM((rows, d), BF16)],
        compiler_params=pltpu.CompilerParams(
            dimension_semantics=("arbitrary",),
            vmem_limit_bytes=KV_VMEM_LIMIT_BYTES),
        name="kv_proj",
    )(mem2d, norm_w, wk, wv)


def _chunk_cumsum(x):
    pos = lax.broadcasted_iota(jnp.int32, x.shape, 0) % HGRN_CHUNK
    s = 1
    while s < HGRN_CHUNK:
        x = x + jnp.where(pos >= s, pltpu.roll(x, s, axis=0), 0.0)
        s *= 2
    return x


def _block_diag2(a, b):
    z = jnp.zeros(a.shape, a.dtype)
    return jnp.concatenate([jnp.concatenate([a, z], axis=1),
                            jnp.concatenate([z, b], axis=1)], axis=0)


def _intra_chunk_by_offset(qf, kk, b, v):
    pos = lax.broadcasted_iota(jnp.int32, qf.shape, 0) % HGRN_CHUNK
    ones = jnp.ones((LANES, LANES), BF16)
    head_sum = _block_diag2(ones, ones)

    def one_offset(d, acc):
        in_chunk = pos >= d
        decay = jnp.exp(jnp.where(in_chunk, b - pltpu.roll(b, d, axis=0), 0.0))
        w = jnp.where(in_chunk, qf * pltpu.roll(kk, d, axis=0) * decay, 0.0)
        return acc + _dot(w.astype(BF16), head_sum) * pltpu.roll(v, d, axis=0)

    return lax.fori_loop(0, HGRN_CHUNK, one_offset, jnp.zeros(qf.shape, F32))


def _mixer_kernel(x_ref, nw_ref, win_ref, wout_ref, lb_ref, hnw_ref, scw_ref, *rest,
                  tiles_per_seq, layer, stable, n_cast):
    cast_in, (o_ref, bmin_ref) = rest[:n_cast], rest[n_cast:n_cast + 2]
    cast_out = rest[n_cast + 2:2 * n_cast + 2]
    h_scr, state_scr, carry_scr = rest[2 * n_cast + 2:]
    _cast_blocks(cast_in, cast_out)
    i = pl.program_id(0)
    tm = x_ref.shape[0]
    hd = LANES
    gw = 2 * hd
    width = lb_ref.shape[1]
    n_pairs = width // gw
    n_chunks = tm // HGRN_CHUNK
    chunks = [slice(c * HGRN_CHUNK, (c + 1) * HGRN_CHUNK) for c in range(n_chunks)]
    blocks = [slice(r0, r0 + HGRN_BLOCK) for r0 in range(0, tm, HGRN_BLOCK)]

    @pl.when(i % tiles_per_seq == 0)
    def _():
        state_scr[...] = jnp.zeros(state_scr.shape, F32)
        carry_scr[...] = jnp.zeros(carry_scr.shape, F32)

    lb_all = jnp.sum(jax.nn.softmax(lb_ref[...], axis=0)[0:layer + 1], axis=0, keepdims=True)
    hnw = hnw_ref[...]
    scw = scw_ref[...]
    if not stable:
        t_id = lax.broadcasted_iota(jnp.int32, (HGRN_BLOCK, 2 * HGRN_BLOCK), 0)
        s_id = lax.broadcasted_iota(jnp.int32, (HGRN_BLOCK, 2 * HGRN_BLOCK), 1) % HGRN_BLOCK
        keep = ((t_id // HGRN_CHUNK) == (s_id // HGRN_CHUNK)) & (s_id <= t_id)

    h = _normalised_tile(x_ref, nw_ref, h_scr)
    gated, convs, b_mins = [], [], []
    for p in range(n_pairs):
        lanes = slice(p * gw, (p + 1) * gw)

        def proj(group):
            return _dot(h, win_ref[:, group * width + p * gw:group * width + (p + 1) * gw])

        f_pre = proj(1)
        q = proj(0)
        lb = lb_all[:, lanes]
        f = lb + (1.0 - lb) * jax.nn.sigmoid(f_pre)
        logf = jnp.log(f)
        kk = 1.0 - f
        qf = _silu(q)
        b = _chunk_cumsum(logf)
        v = proj(2)
        vb = v.astype(BF16)
        g = proj(3)
        cb = proj(4)
        cc = proj(5)
        qt = (qf * jnp.exp(b)).astype(BF16)

        if not stable:
            kt = (kk * jnp.exp(-b)).astype(BF16)
            scores = [_dot_nt(qt[rows], _block_diag2(kt[rows, :hd], kt[rows, hd:]))
                      for rows in blocks]

        b_lasts = [b[rows][HGRN_CHUNK - 1:HGRN_CHUNK] for rows in chunks]
        upds = [_dot_tn(vb[rows], (kk[rows] * jnp.exp(bl - b[rows])).astype(BF16))
                for rows, bl in zip(chunks, b_lasts)]
        ch = proj(6)

        if stable:
            intra = _intra_chunk_by_offset(qf, kk, b, v)
        else:
            intra = jnp.concatenate(
                [_dot(jnp.where(keep, sc, 0.0).astype(BF16),
                      _block_diag2(vb[rows, :hd], vb[rows, hd:]))
                 for rows, sc in zip(blocks, scores)], axis=0)

        st0 = state_scr[2 * p]
        st1 = state_scr[2 * p + 1]
        starts = []
        for upd, bl in zip(upds, b_lasts):
            starts.append(_block_diag2(st0.astype(BF16), st1.astype(BF16)))
            decay = jnp.exp(bl)
            st0 = st0 * decay[:, :hd] + upd[:hd, :hd]
            st1 = st1 * decay[:, hd:] + upd[hd:, hd:]
        state_scr[2 * p] = st0
        state_scr[2 * p + 1] = st1
        inter = [_dot_nt(qt[rows], start) for rows, start in zip(chunks, starts)]

        u = cc * ch
        prev8 = carry_scr[p]
        carry_scr[p] = u[tm - SUBLANES:]
        convs.append((cb * _causal_conv3(u, prev8, scw[:, lanes])).astype(BF16))

        o = intra + jnp.concatenate(inter, axis=0)
        o = jnp.concatenate([_rms(o[:, :hd], hnw), _rms(o[:, hd:], hnw)], axis=1)
        gated.append((o * _silu(g)).astype(BF16))
        b_mins.append(functools.reduce(jnp.minimum, b_lasts))

    mixed = jnp.concatenate(gated + convs, axis=1)
    o_ref[...] = x_ref[...] + _dot(mixed, wout_ref[...])

    b_min = jnp.concatenate(b_mins, axis=1)

    @pl.when(i == 0)
    def _():
        bmin_ref[...] = b_min

    @pl.when(i > 0)
    def _():
        bmin_ref[...] = jnp.minimum(bmin_ref[...], b_min)


def _mixer(x2d, norm_w, w_in, w_out, hgrn_lb, hgrn_norm_w, sconv_w, to_cast=(), *,
           seq_len, layer, stable):
    m, d = x2d.shape
    tm = MIXER_ROW_TILE
    width = hgrn_lb.shape[1]
    n_tiles = m // tm
    cast_specs = _cast_specs(to_cast, n_tiles)
    kern = functools.partial(_mixer_kernel, tiles_per_seq=seq_len // tm, layer=layer,
                             stable=stable, n_cast=len(to_cast))
    out, b_min, *casted = pl.pallas_call(
        kern,
        grid=(n_tiles,),
        in_specs=[
            pl.BlockSpec((tm, d), lambda i: (i, 0)),
            _resident((1, d)),
            _resident(w_in.shape),
            _resident(w_out.shape),
            _resident(hgrn_lb.shape),
            _resident((1, LANES)),
            _resident(sconv_w.shape),
            *cast_specs,
        ],
        out_specs=[pl.BlockSpec((tm, d), lambda i: (i, 0)),
                   pl.BlockSpec((1, width), lambda i: (0, 0)),
                   *cast_specs],
        out_shape=[jax.ShapeDtypeStruct((m, d), F32),
                   jax.ShapeDtypeStruct((1, width), F32),
                   *(jax.ShapeDtypeStruct(w.shape, BF16) for w in to_cast)],
        scratch_shapes=[
            pltpu.VMEM((tm, d), BF16),
            pltpu.VMEM((HGRN_HEADS, LANES, LANES), F32),
            pltpu.VMEM((width // (2 * LANES), SUBLANES, 2 * LANES), F32),
        ],
        compiler_params=pltpu.CompilerParams(
            dimension_semantics=("arbitrary",),
            vmem_limit_bytes=MIXER_VMEM_LIMIT_BYTES),
        name="hybrid_mixer_stable" if stable else "hybrid_mixer",
    )(x2d, norm_w, w_in, w_out, hgrn_lb, hgrn_norm_w, sconv_w, *to_cast)
    return out, b_min, casted


def _attn_kernel(x_ref, nw_ref, wq_ref, k_ref, v_ref, wo_ref, *rest, head_dim, n_cast):
    cast_in, o_ref = rest[:n_cast], rest[n_cast]
    cast_out, h_scr = rest[n_cast + 1:2 * n_cast + 1], rest[2 * n_cast + 1]
    h = _normalised_tile(x_ref, nw_ref, h_scr)
    scale = head_dim ** -0.5
    cols = [slice(hh * head_dim, (hh + 1) * head_dim) for hh in range(MEM_HEADS)]
    qs = [_dot(h, wq_ref[:, c]).astype(BF16) for c in cols]
    _cast_blocks(cast_in, cast_out)
    ss = [_dot_nt(q, k_ref[:, c]) * scale for q, c in zip(qs, cols)]
    out = x_ref[...]
    for s, c in zip(ss, cols):
        e = jnp.exp(s - jnp.max(s, axis=-1, keepdims=True))
        pr = (e / jnp.sum(e, axis=-1, keepdims=True)).astype(BF16)
        out = out + _dot(_dot(pr, v_ref[:, c]).astype(BF16), wo_ref[c, :])
    o_ref[...] = out


def _attn(x2d, norm_w, wq, kv, wo, to_cast, *, seq_len, mem_len):
    m, d = x2d.shape
    tm = ATTN_ROW_TILE
    tiles_per_seq = seq_len // tm
    n_tiles = m // tm

    cast_specs = _cast_specs(to_cast, n_tiles)
    kern = functools.partial(_attn_kernel, head_dim=d // MEM_HEADS, n_cast=len(to_cast))
    out, *casted = pl.pallas_call(
        kern,
        grid=(n_tiles,),
        in_specs=[
            pl.BlockSpec((tm, d), lambda i: (i, 0)),
            _resident((1, d)),
            _resident((d, d)),
            pl.BlockSpec((None, None, mem_len, d), lambda i: (0, i // tiles_per_seq, 0, 0)),
            pl.BlockSpec((None, None, mem_len, d), lambda i: (1, i // tiles_per_seq, 0, 0)),
            _resident((d, d)),
            *cast_specs,
        ],
        out_specs=[pl.BlockSpec((tm, d), lambda i: (i, 0)), *cast_specs],
        out_shape=[jax.ShapeDtypeStruct((m, d), F32),
                   *(jax.ShapeDtypeStruct(w.shape, BF16) for w in to_cast)],
        scratch_shapes=[pltpu.VMEM((tm, d), BF16)],
        compiler_params=pltpu.CompilerParams(
            dimension_semantics=("arbitrary",),
            vmem_limit_bytes=ATTN_VMEM_LIMIT_BYTES),
        name="mem_cross_attn",
    )(x2d, norm_w, wq, kv, kv, wo, *to_cast)
    return out, casted


def _ffn_kernel(x_ref, nw_ref, wg_hbm, wu_hbm, cw_ref, cb_ref, wd_hbm, fnw_ref, o_ref,
                h_scr, wg_buf, wu_buf, wd_buf, sems, carry_scr, *,
                tiles_per_seq, n_tiles, final_norm):
    i = pl.program_id(0)
    tm = x_ref.shape[0]
    nf, _, tf = cw_ref.shape

    def weight_copies(f, slot):
        cols = pl.ds(f * tf if isinstance(f, int) else pl.multiple_of(f * tf, tf), tf)
        return (pltpu.make_async_copy(wg_hbm.at[:, cols], wg_buf.at[slot], sems.at[0, slot]),
                pltpu.make_async_copy(wu_hbm.at[:, cols], wu_buf.at[slot], sems.at[1, slot]),
                pltpu.make_async_copy(wd_hbm.at[cols, :], wd_buf.at[slot], sems.at[2, slot]))

    @pl.when(i == 0)
    def _():
        for copy in weight_copies(0, 0):
            copy.start()

    @pl.when(i % tiles_per_seq == 0)
    def _():
        carry_scr[...] = jnp.zeros(carry_scr.shape, F32)

    def compute_tile(f, slot):
        h = h_scr[...]
        a = _dot(h, wg_buf[slot])
        up = _dot(h, wu_buf[slot])
        prev8 = carry_scr[f]
        carry_scr[f] = a[tm - SUBLANES:]
        act = _silu(_causal_conv3(a, prev8, cw_ref[f]) + cb_ref[f]) * up
        o_ref[...] += _dot(act.astype(BF16), wd_buf[slot])

    slot0 = (i * nf) % 2
    for copy in weight_copies(1, 1 - slot0):
        copy.start()
    for copy in weight_copies(0, slot0):
        copy.wait()
    for r0 in range(0, tm, FFN_PREP_ROWS):
        xv = x_ref[r0:r0 + FFN_PREP_ROWS, :]
        h_scr[r0:r0 + FFN_PREP_ROWS, :] = _rms(xv, nw_ref[...]).astype(BF16)
        o_ref[r0:r0 + FFN_PREP_ROWS, :] = xv
    compute_tile(0, slot0)

    def column_tile(f, carry):
        slot = (i * nf + f) % 2
        last_f = f == nf - 1

        @pl.when(jnp.logical_not(last_f & (i == n_tiles - 1)))
        def _():
            for copy in weight_copies(jnp.where(last_f, 0, f + 1), 1 - slot):
                copy.start()

        for copy in weight_copies(f, slot):
            copy.wait()
        compute_tile(f, slot)
        return carry

    lax.fori_loop(1, nf, column_tile, 0)
    if final_norm:
        def normalise_rows(r, carry):
            rows = pl.ds(pl.multiple_of(r * FFN_PREP_ROWS, FFN_PREP_ROWS), FFN_PREP_ROWS)
            o_ref[rows, :] = _rms(o_ref[rows, :], fnw_ref[...])
            return carry

        lax.fori_loop(0, tm // FFN_PREP_ROWS, normalise_rows, 0)


def _ffn(x2d, norm_w, w_gate, w_up, conv_w, conv_b, w_down, final_w, *, seq_len, final_norm):
    m, d = x2d.shape
    d_ff = w_gate.shape[1]
    tm, tf = FFN_ROW_TILE, FFN_TILE
    nf = d_ff // tf
    n_tiles = m // tm
    conv_w = conv_w.reshape(CONV_TAPS, nf, tf).transpose(1, 0, 2)
    conv_b = conv_b.reshape(nf, 1, tf)
    kern = functools.partial(_ffn_kernel, tiles_per_seq=seq_len // tm, n_tiles=n_tiles,
                             final_norm=final_norm)
    hbm = pl.BlockSpec(memory_space=pl.ANY)
    return pl.pallas_call(
        kern,
        grid=(n_tiles,),
        in_specs=[
            pl.BlockSpec((tm, d), lambda i: (i, 0)),
            _resident((1, d)),
            hbm, hbm,
            _resident(conv_w.shape),
            _resident(conv_b.shape),
            hbm,
            _resident((1, d)),
        ],
        out_specs=pl.BlockSpec((tm, d), lambda i: (i, 0)),
        out_shape=jax.ShapeDtypeStruct((m, d), F32),
        scratch_shapes=[
            pltpu.VMEM((tm, d), BF16),
            pltpu.VMEM((2, d, tf), BF16),
            pltpu.VMEM((2, d, tf), BF16),
            pltpu.VMEM((2, tf, d), BF16),
            pltpu.SemaphoreType.DMA((3, 2)),
            pltpu.VMEM((nf, SUBLANES, tf), F32),
        ],
        compiler_params=pltpu.CompilerParams(
            dimension_semantics=("arbitrary",),
            vmem_limit_bytes=FFN_VMEM_LIMIT_BYTES),
        name="conv_ffn",
    )(x2d, norm_w, w_gate, w_up, conv_w, conv_b, w_down, final_w)


def kernel(x, mem, hgrn_lb, norm1_w, w_in, hgrn_norm_w, sconv_w, w_out, norm2_w, mem_norm_w,
           wq, wk, wv, wo, norm3_w, w_gate, w_up, ffn_conv_w, ffn_conv_b, w_down, final_norm_w):
    batch, seq_len, d = x.shape
    mem_len = mem.shape[1]
    depth = norm1_w.shape[0]
    assert all(seq_len % t == 0 for t in (MIXER_ROW_TILE, ATTN_ROW_TILE, FFN_ROW_TILE))
    assert MIXER_ROW_TILE % HGRN_BLOCK == 0 and HGRN_BLOCK % HGRN_CHUNK == 0
    assert hgrn_lb.shape[1] == HGRN_HEADS * LANES and HGRN_HEADS % 2 == 0
    assert w_in.shape[2] == N_PROJ_GROUPS * HGRN_HEADS * LANES and w_gate.shape[2] % FFN_TILE == 0

    xs = x.reshape(batch * seq_len, d)
    mem2d = mem.reshape(batch * mem_len, d)
    row = lambda w: w.reshape(1, -1)
    for l in range(depth):
        mixer = functools.partial(
            _mixer, xs, row(norm1_w[l]), w_in[l].astype(BF16), w_out[l].astype(BF16), hgrn_lb,
            row(hgrn_norm_w[l]), sconv_w[l], seq_len=seq_len, layer=l)
        fast, b_min, (wq_b, wo_b) = mixer((wq[l], wo[l]), stable=False)
        xs = lax.cond(jnp.min(b_min) < -HGRN_SAFE_LOG_DECAY,
                      lambda: mixer(stable=True)[0], lambda: fast)
        kv = _kv_proj(mem2d, row(mem_norm_w[l]), wk[l], wv[l])
        kv = kv.reshape(2, batch, mem_len, d)
        xs, (wg_b, wu_b, wd_b) = _attn(
            xs, row(norm2_w[l]), wq_b, kv, wo_b,
            (w_gate[l], w_up[l], w_down[l]), seq_len=seq_len, mem_len=mem_len)
        xs = _ffn(xs, row(norm3_w[l]), wg_b, wu_b, ffn_conv_w[l], row(ffn_conv_b[l]), wd_b,
                  row(final_norm_w), seq_len=seq_len, final_norm=(l == depth - 1))
    return xs.reshape(batch, seq_len, d)
```

```python
import functools

import jax
import jax.numpy as jnp
from jax import lax
from jax.experimental import pallas as pl
from jax.experimental.pallas import tpu as pltpu

F32 = jnp.float32
BF16 = jnp.bfloat16

EPS = 1e-6
HGRN_HEADS = 8
HGRN_CHUNK = 64
HGRN_SAFE_LOG_DECAY = 75.0
HGRN_BLOCK = 128
CONV_TAPS = 3
MEM_HEADS = 4
N_PROJ_GROUPS = 7
LANES = 128
SUBLANES = 8
BF16_ROWS = 16
KV_VMEM_LIMIT_BYTES = 56 * 1024 * 1024
MIXER_VMEM_LIMIT_BYTES = 60 * 1024 * 1024
FFN_VMEM_LIMIT_BYTES = 60 * 1024 * 1024
ATTN_VMEM_LIMIT_BYTES = 62 * 1024 * 1024

MIXER_ROW_TILE = 256
ATTN_ROW_TILE = 512
FFN_ROW_TILE = 1024
FFN_TILE = 512
FFN_PREP_ROWS = 128
FFN_FINAL_PARTS = 2


def _rms(x, w):
    ms = jnp.mean(x * x, axis=-1, keepdims=True)
    return x * lax.rsqrt(ms + EPS) * w


def _silu(x):
    return x * jax.nn.sigmoid(x)


def _dot(a, b):
    return jnp.dot(a, b, preferred_element_type=F32)


def _dot_nt(a, b):
    return lax.dot_general(a, b, (((1,), (1,)), ((), ())), preferred_element_type=F32)


def _dot_tn(a, b):
    return lax.dot_general(a, b, (((0,), (0,)), ((), ())), preferred_element_type=F32)


def _shift_rows(cur, prev8, shift):
    rolled = pltpu.roll(cur, shift, axis=0)
    head = pltpu.roll(prev8, shift, axis=0)
    rid = lax.broadcasted_iota(jnp.int32, prev8.shape, 0)
    first = jnp.where(rid < shift, head, rolled[:SUBLANES])
    return jnp.concatenate([first, rolled[SUBLANES:]], axis=0)


def _causal_conv3(cur, prev8, w):
    return (w[0:1] * _shift_rows(cur, prev8, 2)
            + w[1:2] * _shift_rows(cur, prev8, 1)
            + w[2:3] * cur)


def _normalised_tile(x_ref, nw_ref, h_scr):
    h_scr[...] = _rms(x_ref[...], nw_ref[...]).astype(BF16)
    return h_scr[...]


def _cast_specs(to_cast, n_steps):
    def spec(w):
        rows = w.shape[0]
        rb = pl.cdiv(pl.cdiv(rows, n_steps), BF16_ROWS) * BF16_ROWS
        while rows % rb:
            rb += BF16_ROWS
        last = rows // rb - 1
        return pl.BlockSpec((rb, w.shape[1]), lambda i: (jnp.minimum(i, last), 0))

    return [spec(w) for w in to_cast]


def _cast_blocks(src_refs, dst_refs):
    for src, dst in zip(src_refs, dst_refs):
        dst[...] = src[...].astype(dst.dtype)


def _resident(shape):
    return pl.BlockSpec(shape, lambda i: (0,) * len(shape), pipeline_mode=pl.Buffered(1))


def _kv_kernel(mem_ref, nw_ref, wk_ref, wv_ref, o_ref, h_scr):
    @pl.when(pl.program_id(0) == 0)
    def _():
        h_scr[...] = _rms(mem_ref[...], nw_ref[...]).astype(BF16)

    h = h_scr[...]
    o_ref[0] = _dot(h, wk_ref[...].astype(BF16)).astype(o_ref.dtype)
    o_ref[1] = _dot(h, wv_ref[...].astype(BF16)).astype(o_ref.dtype)


def _kv_proj(mem2d, norm_w, wk, wv):
    rows, d = mem2d.shape
    tn = 512
    return pl.pallas_call(
        _kv_kernel,
        grid=(d // tn,),
        in_specs=[
            pl.BlockSpec((rows, d), lambda n: (0, 0)),
            pl.BlockSpec((1, d), lambda n: (0, 0)),
            pl.BlockSpec((d, tn), lambda n: (0, n)),
            pl.BlockSpec((d, tn), lambda n: (0, n)),
        ],
        out_specs=pl.BlockSpec((2, rows, tn), lambda n: (0, 0, n)),
        out_shape=jax.ShapeDtypeStruct((2, rows, d), BF16),
        scratch_shapes=[pltpu.VMEM((rows, d), BF16)],
        compiler_params=pltpu.CompilerParams(
            dimension_semantics=("arbitrary",),
            vmem_limit_bytes=KV_VMEM_LIMIT_BYTES),
        name="kv_proj",
    )(mem2d, norm_w, wk, wv)


def _chunk_cumsum(x):
    pos = lax.broadcasted_iota(jnp.int32, x.shape, 0) % HGRN_CHUNK
    s = 1
    while s < HGRN_CHUNK:
        x = x + jnp.where(pos >= s, pltpu.roll(x, s, axis=0), 0.0)
        s *= 2
    return x


def _block_diag2(a, b):
    z = jnp.zeros(a.shape, a.dtype)
    return jnp.concatenate([jnp.concatenate([a, z], axis=1),
                            jnp.concatenate([z, b], axis=1)], axis=0)


def _intra_chunk_by_offset(qf, kk, b, v):
    pos = lax.broadcasted_iota(jnp.int32, qf.shape, 0) % HGRN_CHUNK
    ones = jnp.ones((LANES, LANES), BF16)
    head_sum = _block_diag2(ones, ones)

    def one_offset(d, acc):
        in_chunk = pos >= d
        decay = jnp.exp(jnp.where(in_chunk, b - pltpu.roll(b, d, axis=0), 0.0))
        w = jnp.where(in_chunk, qf * pltpu.roll(kk, d, axis=0) * decay, 0.0)
        return acc + _dot(w.astype(BF16), head_sum) * pltpu.roll(v, d, axis=0)

    return lax.fori_loop(0, HGRN_CHUNK, one_offset, jnp.zeros(qf.shape, F32))


def _mixer_kernel(x_ref, nw_ref, win_ref, wout_ref, lb_ref, hnw_ref, scw_ref, *rest,
                  tiles_per_seq, layer, stable, n_cast):
    cast_in, (o_ref, bmin_ref) = rest[:n_cast], rest[n_cast:n_cast + 2]
    cast_out = rest[n_cast + 2:2 * n_cast + 2]
    h_scr, state_scr, carry_scr = rest[2 * n_cast + 2:]
    _cast_blocks(cast_in, cast_out)
    i = pl.program_id(0)
    tm = x_ref.shape[0]
    hd = LANES
    gw = 2 * hd
    width = lb_ref.shape[1]
    n_pairs = width // gw
    n_chunks = tm // HGRN_CHUNK
    chunks = [slice(c * HGRN_CHUNK, (c + 1) * HGRN_CHUNK) for c in range(n_chunks)]
    blocks = [slice(r0, r0 + HGRN_BLOCK) for r0 in range(0, tm, HGRN_BLOCK)]

    @pl.when(i % tiles_per_seq == 0)
    def _():
        state_scr[...] = jnp.zeros(state_scr.shape, F32)
        carry_scr[...] = jnp.zeros(carry_scr.shape, F32)

    lb_all = jnp.sum(jax.nn.softmax(lb_ref[...], axis=0)[0:layer + 1], axis=0, keepdims=True)
    hnw = hnw_ref[...]
    scw = scw_ref[...]
    if not stable:
        t_id = lax.broadcasted_iota(jnp.int32, (HGRN_BLOCK, 2 * HGRN_BLOCK), 0)
        s_id = lax.broadcasted_iota(jnp.int32, (HGRN_BLOCK, 2 * HGRN_BLOCK), 1) % HGRN_BLOCK
        keep = ((t_id // HGRN_CHUNK) == (s_id // HGRN_CHUNK)) & (s_id <= t_id)

    h = _normalised_tile(x_ref, nw_ref, h_scr)
    gated, convs, b_mins = [], [], []
    for p in range(n_pairs):
        lanes = slice(p * gw, (p + 1) * gw)

        def proj(group):
            return _dot(h, win_ref[:, group * width + p * gw:group * width + (p + 1) * gw])

        f_pre = proj(1)
        q = proj(0)
        lb = lb_all[:, lanes]
        f = lb + (1.0 - lb) * jax.nn.sigmoid(f_pre)
        logf = jnp.log(f)
        kk = 1.0 - f
        qf = _silu(q)
        b = _chunk_cumsum(logf)
        v = proj(2)
        vb = v.astype(BF16)
        g = proj(3)
        cb = proj(4)
        cc = proj(5)
        qt = (qf * jnp.exp(b)).astype(BF16)

        if not stable:
            kt = (kk * jnp.exp(-b)).astype(BF16)
            scores = [_dot_nt(qt[rows], _block_diag2(kt[rows, :hd], kt[rows, hd:]))
                      for rows in blocks]

        b_lasts = [b[rows][HGRN_CHUNK - 1:HGRN_CHUNK] for rows in chunks]
        upds = [_dot_tn(vb[rows], (kk[rows] * jnp.exp(bl - b[rows])).astype(BF16))
                for rows, bl in zip(chunks, b_lasts)]
        ch = proj(6)

        if stable:
            intra = _intra_chunk_by_offset(qf, kk, b, v)
        else:
            intra = jnp.concatenate(
                [_dot(jnp.where(keep, sc, 0.0).astype(BF16),
                      _block_diag2(vb[rows, :hd], vb[rows, hd:]))
                 for rows, sc in zip(blocks, scores)], axis=0)

        st0 = state_scr[2 * p]
        st1 = state_scr[2 * p + 1]
        starts = []
        for upd, bl in zip(upds, b_lasts):
            starts.append(_block_diag2(st0.astype(BF16), st1.astype(BF16)))
            decay = jnp.exp(bl)
            st0 = st0 * decay[:, :hd] + upd[:hd, :hd]
            st1 = st1 * decay[:, hd:] + upd[hd:, hd:]
        state_scr[2 * p] = st0
        state_scr[2 * p + 1] = st1
        inter = [_dot_nt(qt[rows], start) for rows, start in zip(chunks, starts)]

        u = cc * ch
        prev8 = carry_scr[p]
        carry_scr[p] = u[tm - SUBLANES:]
        convs.append((cb * _causal_conv3(u, prev8, scw[:, lanes])).astype(BF16))

        o = intra + jnp.concatenate(inter, axis=0)
        o = jnp.concatenate([_rms(o[:, :hd], hnw), _rms(o[:, hd:], hnw)], axis=1)
        gated.append((o * _silu(g)).astype(BF16))
        b_mins.append(functools.reduce(jnp.minimum, b_lasts))

    mixed = jnp.concatenate(gated + convs, axis=1)
    o_ref[...] = x_ref[...] + _dot(mixed, wout_ref[...])

    b_min = jnp.concatenate(b_mins, axis=1)

    @pl.when(i == 0)
    def _():
        bmin_ref[...] = b_min

    @pl.when(i > 0)
    def _():
        bmin_ref[...] = jnp.minimum(bmin_ref[...], b_min)


def _mixer(x2d, norm_w, w_in, w_out, hgrn_lb, hgrn_norm_w, sconv_w, to_cast=(), *,
           seq_len, layer, stable):
    m, d = x2d.shape
    tm = MIXER_ROW_TILE
    width = hgrn_lb.shape[1]
    n_tiles = m // tm
    cast_specs = _cast_specs(to_cast, n_tiles)
    kern = functools.partial(_mixer_kernel, tiles_per_seq=seq_len // tm, layer=layer,
                             stable=stable, n_cast=len(to_cast))
    out, b_min, *casted = pl.pallas_call(
        kern,
        grid=(n_tiles,),
        in_specs=[
            pl.BlockSpec((tm, d), lambda i: (i, 0)),
            _resident((1, d)),
            _resident(w_in.shape),
            _resident(w_out.shape),
            _resident(hgrn_lb.shape),
            _resident((1, LANES)),
            _resident(sconv_w.shape),
            *cast_specs,
        ],
        out_specs=[pl.BlockSpec((tm, d), lambda i: (i, 0)),
                   pl.BlockSpec((1, width), lambda i: (0, 0)),
                   *cast_specs],
        out_shape=[jax.ShapeDtypeStruct((m, d), F32),
                   jax.ShapeDtypeStruct((1, width), F32),
                   *(jax.ShapeDtypeStruct(w.shape, BF16) for w in to_cast)],
        scratch_shapes=[
            pltpu.VMEM((tm, d), BF16),
            pltpu.VMEM((HGRN_HEADS, LANES, LANES), F32),
            pltpu.VMEM((width // (2 * LANES), SUBLANES, 2 * LANES), F32),
        ],
        compiler_params=pltpu.CompilerParams(
            dimension_semantics=("arbitrary",),
            vmem_limit_bytes=MIXER_VMEM_LIMIT_BYTES),
        name="hybrid_mixer_stable" if stable else "hybrid_mixer",
    )(x2d, norm_w, w_in, w_out, hgrn_lb, hgrn_norm_w, sconv_w, *to_cast)
    return out, b_min, casted


def _attn_kernel(x_ref, nw_ref, wq_ref, k_ref, v_ref, wo_ref, *rest, head_dim, n_cast):
    cast_in, o_ref = rest[:n_cast], rest[n_cast]
    cast_out, h_scr = rest[n_cast + 1:2 * n_cast + 1], rest[2 * n_cast + 1]
    h = _normalised_tile(x_ref, nw_ref, h_scr)
    scale = head_dim ** -0.5
    cols = [slice(hh * head_dim, (hh + 1) * head_dim) for hh in range(MEM_HEADS)]
    qs = [_dot(h, wq_ref[:, c]).astype(BF16) for c in cols]
    _cast_blocks(cast_in, cast_out)
    ss = [_dot_nt(q, k_ref[:, c]) * scale for q, c in zip(qs, cols)]
    out = x_ref[...]
    for s, c in zip(ss, cols):
        e = jnp.exp(s - jnp.max(s, axis=-1, keepdims=True))
        pr = (e / jnp.sum(e, axis=-1, keepdims=True)).astype(BF16)
        out = out + _dot(_dot(pr, v_ref[:, c]).astype(BF16), wo_ref[c, :])
    o_ref[...] = out


def _attn(x2d, norm_w, wq, kv, wo, to_cast, *, seq_len, mem_len):
    m, d = x2d.shape
    tm = ATTN_ROW_TILE
    tiles_per_seq = seq_len // tm
    n_tiles = m // tm

    cast_specs = _cast_specs(to_cast, n_tiles)
    kern = functools.partial(_attn_kernel, head_dim=d // MEM_HEADS, n_cast=len(to_cast))
    out, *casted = pl.pallas_call(
        kern,
        grid=(n_tiles,),
        in_specs=[
            pl.BlockSpec((tm, d), lambda i: (i, 0)),
            _resident((1, d)),
            _resident((d, d)),
            pl.BlockSpec((None, None, mem_len, d), lambda i: (0, i // tiles_per_seq, 0, 0)),
            pl.BlockSpec((None, None, mem_len, d), lambda i: (1, i // tiles_per_seq, 0, 0)),
            _resident((d, d)),
            *cast_specs,
        ],
        out_specs=[pl.BlockSpec((tm, d), lambda i: (i, 0)), *cast_specs],
        out_shape=[jax.ShapeDtypeStruct((m, d), F32),
                   *(jax.ShapeDtypeStruct(w.shape, BF16) for w in to_cast)],
        scratch_shapes=[pltpu.VMEM((tm, d), BF16)],
        compiler_params=pltpu.CompilerParams(
            dimension_semantics=("arbitrary",),
            vmem_limit_bytes=ATTN_VMEM_LIMIT_BYTES),
        name="mem_cross_attn",
    )(x2d, norm_w, wq, kv, kv, wo, *to_cast)
    return out, casted


def _ffn_kernel(x_ref, nw_ref, wg_hbm, wu_hbm, cw_ref, cb_ref, wd_hbm, fnw_ref, o_ref,
                h_scr, wg_buf, wu_buf, wd_buf, sems, carry_scr, *,
                tiles_per_seq, n_tiles, final_norm):
    i = pl.program_id(0)
    tm = x_ref.shape[0]
    nf, _, tf = cw_ref.shape

    def weight_copies(f, slot):
        cols = pl.ds(f * tf if isinstance(f, int) else pl.multiple_of(f * tf, tf), tf)
        return (pltpu.make_async_copy(wg_hbm.at[:, cols], wg_buf.at[slot], sems.at[0, slot]),
                pltpu.make_async_copy(wu_hbm.at[:, cols], wu_buf.at[slot], sems.at[1, slot]),
                pltpu.make_async_copy(wd_hbm.at[cols, :], wd_buf.at[slot], sems.at[2, slot]))

    @pl.when(i == 0)
    def _():
        for copy in weight_copies(0, 0):
            copy.start()

    @pl.when(i % tiles_per_seq == 0)
    def _():
        carry_scr[...] = jnp.zeros(carry_scr.shape, F32)

    def compute_tile(f, slot, row_parts=1, finish_rows=None):
        h = h_scr[...]
        a = _dot(h, wg_buf[slot])
        up = _dot(h, wu_buf[slot])
        prev8 = carry_scr[f]
        carry_scr[f] = a[tm - SUBLANES:]
        act = (_silu(_causal_conv3(a, prev8, cw_ref[f]) + cb_ref[f]) * up).astype(BF16)
        part = tm // row_parts
        for r0 in range(0, tm, part):
            o_ref[r0:r0 + part, :] += _dot(act[r0:r0 + part], wd_buf[slot])
            if finish_rows is not None:
                finish_rows(r0, part)

    slot0 = (i * nf) % 2
    for copy in weight_copies(1, 1 - slot0):
        copy.start()
    for copy in weight_copies(0, slot0):
        copy.wait()
    for r0 in range(0, tm, FFN_PREP_ROWS):
        xv = x_ref[r0:r0 + FFN_PREP_ROWS, :]
        h_scr[r0:r0 + FFN_PREP_ROWS, :] = _rms(xv, nw_ref[...]).astype(BF16)
        o_ref[r0:r0 + FFN_PREP_ROWS, :] = xv
    compute_tile(0, slot0)

    def column_tile(f, carry):
        slot = (i * nf + f) % 2
        for copy in weight_copies(f + 1, 1 - slot):
            copy.start()
        for copy in weight_copies(f, slot):
            copy.wait()
        compute_tile(f, slot)
        return carry

    lax.fori_loop(1, nf - 1, column_tile, 0)

    last = nf - 1
    slot_last = (i * nf + last) % 2

    @pl.when(i < n_tiles - 1)
    def _():
        for copy in weight_copies(0, 1 - slot_last):
            copy.start()

    for copy in weight_copies(last, slot_last):
        copy.wait()

    def final_rms(r0, rows):
        for b0 in range(r0, r0 + rows, FFN_PREP_ROWS):
            blk = slice(b0, b0 + FFN_PREP_ROWS)
            o_ref[blk, :] = _rms(o_ref[blk, :], fnw_ref[...])

    if final_norm:
        compute_tile(last, slot_last, FFN_FINAL_PARTS, final_rms)
    else:
        compute_tile(last, slot_last)


def _ffn(x2d, norm_w, w_gate, w_up, conv_w, conv_b, w_down, final_w, *, seq_len, final_norm):
    m, d = x2d.shape
    d_ff = w_gate.shape[1]
    tm, tf = FFN_ROW_TILE, FFN_TILE
    nf = d_ff // tf
    assert nf >= 3, "first and last column tiles are peeled around a loop over the rest"
    n_tiles = m // tm
    conv_w = conv_w.reshape(CONV_TAPS, nf, tf).transpose(1, 0, 2)
    conv_b = conv_b.reshape(nf, 1, tf)
    kern = functools.partial(_ffn_kernel, tiles_per_seq=seq_len // tm, n_tiles=n_tiles,
                             final_norm=final_norm)
    hbm = pl.BlockSpec(memory_space=pl.ANY)
    return pl.pallas_call(
        kern,
        grid=(n_tiles,),
        in_specs=[
            pl.BlockSpec((tm, d), lambda i: (i, 0)),
            _resident((1, d)),
            hbm, hbm,
            _resident(conv_w.shape),
            _resident(conv_b.shape),
            hbm,
            _resident((1, d)),
        ],
        out_specs=pl.BlockSpec((tm, d), lambda i: (i, 0)),
        out_shape=jax.ShapeDtypeStruct((m, d), F32),
        scratch_shapes=[
            pltpu.VMEM((tm, d), BF16),
            pltpu.VMEM((2, d, tf), BF16),
            pltpu.VMEM((2, d, tf), BF16),
            pltpu.VMEM((2, tf, d), BF16),
            pltpu.SemaphoreType.DMA((3, 2)),
            pltpu.VMEM((nf, SUBLANES, tf), F32),
        ],
        compiler_params=pltpu.CompilerParams(
            dimension_semantics=("arbitrary",),
            vmem_limit_bytes=FFN_VMEM_LIMIT_BYTES),
        name="conv_ffn",
    )(x2d, norm_w, w_gate, w_up, conv_w, conv_b, w_down, final_w)


def kernel(x, mem, hgrn_lb, norm1_w, w_in, hgrn_norm_w, sconv_w, w_out, norm2_w, mem_norm_w,
           wq, wk, wv, wo, norm3_w, w_gate, w_up, ffn_conv_w, ffn_conv_b, w_down, final_norm_w):
    batch, seq_len, d = x.shape
    mem_len = mem.shape[1]
    depth = norm1_w.shape[0]
    assert all(seq_len % t == 0 for t in (MIXER_ROW_TILE, ATTN_ROW_TILE, FFN_ROW_TILE))
    assert MIXER_ROW_TILE % HGRN_BLOCK == 0 and HGRN_BLOCK % HGRN_CHUNK == 0
    assert hgrn_lb.shape[1] == HGRN_HEADS * LANES and HGRN_HEADS % 2 == 0
    assert w_in.shape[2] == N_PROJ_GROUPS * HGRN_HEADS * LANES and w_gate.shape[2] % FFN_TILE == 0

    xs = x.reshape(batch * seq_len, d)
    mem2d = mem.reshape(batch * mem_len, d)
    row = lambda w: w.reshape(1, -1)
    for l in range(depth):
        mixer = functools.partial(
            _mixer, xs, row(norm1_w[l]), w_in[l].astype(BF16), w_out[l].astype(BF16), hgrn_lb,
            row(hgrn_norm_w[l]), sconv_w[l], seq_len=seq_len, layer=l)
        fast, b_min, (wq_b, wo_b) = mixer((wq[l], wo[l]), stable=False)
        xs = lax.cond(jnp.min(b_min) < -HGRN_SAFE_LOG_DECAY,
                      lambda: mixer(stable=True)[0], lambda: fast)
        kv = _kv_proj(mem2d, row(mem_norm_w[l]), wk[l], wv[l])
        kv = kv.reshape(2, batch, mem_len, d)
        xs, (wg_b, wu_b, wd_b) = _attn(
            xs, row(norm2_w[l]), wq_b, kv, wo_b,
            (w_gate[l], w_up[l], w_down[l]), seq_len=seq_len, mem_len=mem_len)
        xs = _ffn(xs, row(norm3_w[l]), wg_b, wu_b, ffn_conv_w[l], row(ffn_conv_b[l]), wd_b,
                  row(final_norm_w), seq_len=seq_len, final_norm=(l == depth - 1))
    return xs.reshape(batch, seq_len, d)
```

```python
import functools

import jax
import jax.numpy as jnp
from jax import lax
from jax.experimental import pallas as pl
from jax.experimental.pallas import tpu as pltpu

F32 = jnp.float32
BF16 = jnp.bfloat16

EPS = 1e-6
HGRN_HEADS = 8
HGRN_CHUNK = 64
HGRN_SAFE_LOG_DECAY = 75.0
HGRN_BLOCK = 128
CONV_TAPS = 3
MEM_HEADS = 4
N_PROJ_GROUPS = 7
LANES = 128
SUBLANES = 8
BF16_ROWS = 16
KV_VMEM_LIMIT_BYTES = 56 * 1024 * 1024
MIXER_VMEM_LIMIT_BYTES = 60 * 1024 * 1024
FFN_VMEM_LIMIT_BYTES = 60 * 1024 * 1024
ATTN_VMEM_LIMIT_BYTES = 62 * 1024 * 1024

MIXER_ROW_TILE = 256
ATTN_ROW_TILE = 512
FFN_ROW_TILE = 1024
FFN_TILE = 512
FFN_PREP_ROWS = 128
FFN_FINAL_PARTS = 4


def _rms(x, w):
    ms = jnp.mean(x * x, axis=-1, keepdims=True)
    return x * lax.rsqrt(ms + EPS) * w


def _silu(x):
    return x * jax.nn.sigmoid(x)


def _dot(a, b):
    return jnp.dot(a, b, preferred_element_type=F32)


def _dot_nt(a, b):
    return lax.dot_general(a, b, (((1,), (1,)), ((), ())), preferred_element_type=F32)


def _dot_tn(a, b):
    return lax.dot_general(a, b, (((0,), (0,)), ((), ())), preferred_element_type=F32)


def _shift_rows(cur, prev8, shift):
    rolled = pltpu.roll(cur, shift, axis=0)
    head = pltpu.roll(prev8, shift, axis=0)
    rid = lax.broadcasted_iota(jnp.int32, prev8.shape, 0)
    first = jnp.where(rid < shift, head, rolled[:SUBLANES])
    return jnp.concatenate([first, rolled[SUBLANES:]], axis=0)


def _causal_conv3(cur, prev8, w):
    return (w[0:1] * _shift_rows(cur, prev8, 2)
            + w[1:2] * _shift_rows(cur, prev8, 1)
            + w[2:3] * cur)


def _normalised_tile(x_ref, nw_ref, h_scr):
    h_scr[...] = _rms(x_ref[...], nw_ref[...]).astype(BF16)
    return h_scr[...]


def _cast_specs(to_cast, n_steps):
    def spec(w):
        rows = w.shape[0]
        rb = pl.cdiv(pl.cdiv(rows, n_steps), BF16_ROWS) * BF16_ROWS
        while rows % rb:
            rb += BF16_ROWS
        last = rows // rb - 1
        return pl.BlockSpec((rb, w.shape[1]), lambda i: (jnp.minimum(i, last), 0))

    return [spec(w) for w in to_cast]


def _cast_blocks(src_refs, dst_refs):
    for src, dst in zip(src_refs, dst_refs):
        dst[...] = src[...].astype(dst.dtype)


def _resident(shape):
    return pl.BlockSpec(shape, lambda i: (0,) * len(shape), pipeline_mode=pl.Buffered(1))


def _kv_kernel(mem_ref, nw_ref, wk_ref, wv_ref, o_ref, h_scr):
    @pl.when(pl.program_id(0) == 0)
    def _():
        h_scr[...] = _rms(mem_ref[...], nw_ref[...]).astype(BF16)

    h = h_scr[...]
    o_ref[0] = _dot(h, wk_ref[...].astype(BF16)).astype(o_ref.dtype)
    o_ref[1] = _dot(h, wv_ref[...].astype(BF16)).astype(o_ref.dtype)


def _kv_proj(mem2d, norm_w, wk, wv):
    rows, d = mem2d.shape
    tn = 512
    return pl.pallas_call(
        _kv_kernel,
        grid=(d // tn,),
        in_specs=[
            pl.BlockSpec((rows, d), lambda n: (0, 0)),
            pl.BlockSpec((1, d), lambda n: (0, 0)),
            pl.BlockSpec((d, tn), lambda n: (0, n)),
            pl.BlockSpec((d, tn), lambda n: (0, n)),
        ],
        out_specs=pl.BlockSpec((2, rows, tn), lambda n: (0, 0, n)),
        out_shape=jax.ShapeDtypeStruct((2, rows, d), BF16),
        scratch_shapes=[pltpu.VMEM((rows, d), BF16)],
        compiler_params=pltpu.CompilerParams(
            dimension_semantics=("arbitrary",),
            vmem_limit_bytes=KV_VMEM_LIMIT_BYTES),
        name="kv_proj",
    )(mem2d, norm_w, wk, wv)


def _chunk_cumsum(x):
    pos = lax.broadcasted_iota(jnp.int32, x.shape, 0) % HGRN_CHUNK
    s = 1
    while s < HGRN_CHUNK:
        x = x + jnp.where(pos >= s, pltpu.roll(x, s, axis=0), 0.0)
        s *= 2
    return x


def _block_diag2(a, b):
    z = jnp.zeros(a.shape, a.dtype)
    return jnp.concatenate([jnp.concatenate([a, z], axis=1),
                            jnp.concatenate([z, b], axis=1)], axis=0)


def _intra_chunk_by_offset(qf, kk, b, v):
    pos = lax.broadcasted_iota(jnp.int32, qf.shape, 0) % HGRN_CHUNK
    ones = jnp.ones((LANES, LANES), BF16)
    head_sum = _block_diag2(ones, ones)

    def one_offset(d, acc):
        in_chunk = pos >= d
        decay = jnp.exp(jnp.where(in_chunk, b - pltpu.roll(b, d, axis=0), 0.0))
        w = jnp.where(in_chunk, qf * pltpu.roll(kk, d, axis=0) * decay, 0.0)
        return acc + _dot(w.astype(BF16), head_sum) * pltpu.roll(v, d, axis=0)

    return lax.fori_loop(0, HGRN_CHUNK, one_offset, jnp.zeros(qf.shape, F32))


def _mixer_kernel(x_ref, nw_ref, win_ref, wout_ref, lb_ref, hnw_ref, scw_ref, *rest,
                  tiles_per_seq, layer, stable, n_cast):
    cast_in, (o_ref, bmin_ref) = rest[:n_cast], rest[n_cast:n_cast + 2]
    cast_out = rest[n_cast + 2:2 * n_cast + 2]
    h_scr, state_scr, carry_scr = rest[2 * n_cast + 2:]
    _cast_blocks(cast_in, cast_out)
    i = pl.program_id(0)
    tm = x_ref.shape[0]
    hd = LANES
    gw = 2 * hd
    width = lb_ref.shape[1]
    n_pairs = width // gw
    n_chunks = tm // HGRN_CHUNK
    chunks = [slice(c * HGRN_CHUNK, (c + 1) * HGRN_CHUNK) for c in range(n_chunks)]
    blocks = [slice(r0, r0 + HGRN_BLOCK) for r0 in range(0, tm, HGRN_BLOCK)]

    @pl.when(i % tiles_per_seq == 0)
    def _():
        state_scr[...] = jnp.zeros(state_scr.shape, F32)
        carry_scr[...] = jnp.zeros(carry_scr.shape, F32)

    lb_all = jnp.sum(jax.nn.softmax(lb_ref[...], axis=0)[0:layer + 1], axis=0, keepdims=True)
    hnw = hnw_ref[...]
    scw = scw_ref[...]
    if not stable:
        t_id = lax.broadcasted_iota(jnp.int32, (HGRN_BLOCK, 2 * HGRN_BLOCK), 0)
        s_id = lax.broadcasted_iota(jnp.int32, (HGRN_BLOCK, 2 * HGRN_BLOCK), 1) % HGRN_BLOCK
        keep = ((t_id // HGRN_CHUNK) == (s_id // HGRN_CHUNK)) & (s_id <= t_id)

    h = _normalised_tile(x_ref, nw_ref, h_scr)
    gated, convs, b_mins = [], [], []
    for p in range(n_pairs):
        lanes = slice(p * gw, (p + 1) * gw)

        def proj(group):
            return _dot(h, win_ref[:, group * width + p * gw:group * width + (p + 1) * gw])

        f_pre = proj(1)
        q = proj(0)
        lb = lb_all[:, lanes]
        f = lb + (1.0 - lb) * jax.nn.sigmoid(f_pre)
        logf = jnp.log(f)
        kk = 1.0 - f
        qf = _silu(q)
        b = _chunk_cumsum(logf)
        v = proj(2)
        vb = v.astype(BF16)
        g = proj(3)
        cb = proj(4)
        cc = proj(5)
        qt = (qf * jnp.exp(b)).astype(BF16)

        if not stable:
            kt = (kk * jnp.exp(-b)).astype(BF16)
            scores = [_dot_nt(qt[rows], _block_diag2(kt[rows, :hd], kt[rows, hd:]))
                      for rows in blocks]

        b_lasts = [b[rows][HGRN_CHUNK - 1:HGRN_CHUNK] for rows in chunks]
        upds = [_dot_tn(vb[rows], (kk[rows] * jnp.exp(bl - b[rows])).astype(BF16))
                for rows, bl in zip(chunks, b_lasts)]
        ch = proj(6)

        if stable:
            intra = _intra_chunk_by_offset(qf, kk, b, v)
        else:
            intra = jnp.concatenate(
                [_dot(jnp.where(keep, sc, 0.0).astype(BF16),
                      _block_diag2(vb[rows, :hd], vb[rows, hd:]))
                 for rows, sc in zip(blocks, scores)], axis=0)

        st0 = state_scr[2 * p]
        st1 = state_scr[2 * p + 1]
        starts = []
        for upd, bl in zip(upds, b_lasts):
            starts.append(_block_diag2(st0.astype(BF16), st1.astype(BF16)))
            decay = jnp.exp(bl)
            st0 = st0 * decay[:, :hd] + upd[:hd, :hd]
            st1 = st1 * decay[:, hd:] + upd[hd:, hd:]
        state_scr[2 * p] = st0
        state_scr[2 * p + 1] = st1
        inter = [_dot_nt(qt[rows], start) for rows, start in zip(chunks, starts)]

        u = cc * ch
        prev8 = carry_scr[p]
        carry_scr[p] = u[tm - SUBLANES:]
        convs.append((cb * _causal_conv3(u, prev8, scw[:, lanes])).astype(BF16))

        o = intra + jnp.concatenate(inter, axis=0)
        o = jnp.concatenate([_rms(o[:, :hd], hnw), _rms(o[:, hd:], hnw)], axis=1)
        gated.append((o * _silu(g)).astype(BF16))
        b_mins.append(functools.reduce(jnp.minimum, b_lasts))

    mixed = jnp.concatenate(gated + convs, axis=1)
    o_ref[...] = x_ref[...] + _dot(mixed, wout_ref[...])

    b_min = jnp.concatenate(b_mins, axis=1)

    @pl.when(i == 0)
    def _():
        bmin_ref[...] = b_min

    @pl.when(i > 0)
    def _():
        bmin_ref[...] = jnp.minimum(bmin_ref[...], b_min)


def _mixer(x2d, norm_w, w_in, w_out, hgrn_lb, hgrn_norm_w, sconv_w, to_cast=(), *,
           seq_len, layer, stable):
    m, d = x2d.shape
    tm = MIXER_ROW_TILE
    width = hgrn_lb.shape[1]
    n_tiles = m // tm
    cast_specs = _cast_specs(to_cast, n_tiles)
    kern = functools.partial(_mixer_kernel, tiles_per_seq=seq_len // tm, layer=layer,
                             stable=stable, n_cast=len(to_cast))
    out, b_min, *casted = pl.pallas_call(
        kern,
        grid=(n_tiles,),
        in_specs=[
            pl.BlockSpec((tm, d), lambda i: (i, 0)),
            _resident((1, d)),
            _resident(w_in.shape),
            _resident(w_out.shape),
            _resident(hgrn_lb.shape),
            _resident((1, LANES)),
            _resident(sconv_w.shape),
            *cast_specs,
        ],
        out_specs=[pl.BlockSpec((tm, d), lambda i: (i, 0)),
                   pl.BlockSpec((1, width), lambda i: (0, 0)),
                   *cast_specs],
        out_shape=[jax.ShapeDtypeStruct((m, d), F32),
                   jax.ShapeDtypeStruct((1, width), F32),
                   *(jax.ShapeDtypeStruct(w.shape, BF16) for w in to_cast)],
        scratch_shapes=[
            pltpu.VMEM((tm, d), BF16),
            pltpu.VMEM((HGRN_HEADS, LANES, LANES), F32),
            pltpu.VMEM((width // (2 * LANES), SUBLANES, 2 * LANES), F32),
        ],
        compiler_params=pltpu.CompilerParams(
            dimension_semantics=("arbitrary",),
            vmem_limit_bytes=MIXER_VMEM_LIMIT_BYTES),
        name="hybrid_mixer_stable" if stable else "hybrid_mixer",
    )(x2d, norm_w, w_in, w_out, hgrn_lb, hgrn_norm_w, sconv_w, *to_cast)
    return out, b_min, casted


def _attn_kernel(x_ref, nw_ref, wq_ref, k_ref, v_ref, wo_ref, *rest, head_dim, n_cast):
    cast_in, o_ref = rest[:n_cast], rest[n_cast]
    cast_out, h_scr = rest[n_cast + 1:2 * n_cast + 1], rest[2 * n_cast + 1]
    h = _normalised_tile(x_ref, nw_ref, h_scr)
    scale = head_dim ** -0.5
    cols = [slice(hh * head_dim, (hh + 1) * head_dim) for hh in range(MEM_HEADS)]
    qs = [_dot(h, wq_ref[:, c]).astype(BF16) for c in cols]
    _cast_blocks(cast_in, cast_out)
    ss = [_dot_nt(q, k_ref[:, c]) * scale for q, c in zip(qs, cols)]
    out = x_ref[...]
    for s, c in zip(ss, cols):
        e = jnp.exp(s - jnp.max(s, axis=-1, keepdims=True))
        pr = (e / jnp.sum(e, axis=-1, keepdims=True)).astype(BF16)
        out = out + _dot(_dot(pr, v_ref[:, c]).astype(BF16), wo_ref[c, :])
    o_ref[...] = out


def _attn(x2d, norm_w, wq, kv, wo, to_cast, *, seq_len, mem_len):
    m, d = x2d.shape
    tm = ATTN_ROW_TILE
    tiles_per_seq = seq_len // tm
    n_tiles = m // tm

    cast_specs = _cast_specs(to_cast, n_tiles)
    kern = functools.partial(_attn_kernel, head_dim=d // MEM_HEADS, n_cast=len(to_cast))
    out, *casted = pl.pallas_call(
        kern,
        grid=(n_tiles,),
        in_specs=[
            pl.BlockSpec((tm, d), lambda i: (i, 0)),
            _resident((1, d)),
            _resident((d, d)),
            pl.BlockSpec((None, None, mem_len, d), lambda i: (0, i // tiles_per_seq, 0, 0)),
            pl.BlockSpec((None, None, mem_len, d), lambda i: (1, i // tiles_per_seq, 0, 0)),
            _resident((d, d)),
            *cast_specs,
        ],
        out_specs=[pl.BlockSpec((tm, d), lambda i: (i, 0)), *cast_specs],
        out_shape=[jax.ShapeDtypeStruct((m, d), F32),
                   *(jax.ShapeDtypeStruct(w.shape, BF16) for w in to_cast)],
        scratch_shapes=[pltpu.VMEM((tm, d), BF16)],
        compiler_params=pltpu.CompilerParams(
            dimension_semantics=("arbitrary",),
            vmem_limit_bytes=ATTN_VMEM_LIMIT_BYTES),
        name="mem_cross_attn",
    )(x2d, norm_w, wq, kv, kv, wo, *to_cast)
    return out, casted


def _ffn_kernel(x_ref, nw_ref, wg_hbm, wu_hbm, cw_ref, cb_ref, wd_hbm, fnw_ref, o_ref,
                h_scr, wg_buf, wu_buf, wd_buf, sems, carry_scr, *,
                tiles_per_seq, n_tiles, final_norm):
    i = pl.program_id(0)
    tm = x_ref.shape[0]
    nf, _, tf = cw_ref.shape

    def weight_copies(f, slot):
        cols = pl.ds(f * tf if isinstance(f, int) else pl.multiple_of(f * tf, tf), tf)
        return (pltpu.make_async_copy(wg_hbm.at[:, cols], wg_buf.at[slot], sems.at[0, slot]),
                pltpu.make_async_copy(wu_hbm.at[:, cols], wu_buf.at[slot], sems.at[1, slot]),
                pltpu.make_async_copy(wd_hbm.at[cols, :], wd_buf.at[slot], sems.at[2, slot]))

    @pl.when(i == 0)
    def _():
        for copy in weight_copies(0, 0):
            copy.start()

    @pl.when(i % tiles_per_seq == 0)
    def _():
        carry_scr[...] = jnp.zeros(carry_scr.shape, F32)

    def compute_tile(f, slot, row_parts=1, finish_rows=None):
        h = h_scr[...]
        a = _dot(h, wg_buf[slot])
        up = _dot(h, wu_buf[slot])
        prev8 = carry_scr[f]
        carry_scr[f] = a[tm - SUBLANES:]
        act = (_silu(_causal_conv3(a, prev8, cw_ref[f]) + cb_ref[f]) * up).astype(BF16)
        part = tm // row_parts
        for r0 in range(0, tm, part):
            o_ref[r0:r0 + part, :] += _dot(act[r0:r0 + part], wd_buf[slot])
            if finish_rows is not None:
                finish_rows(r0, part)

    slot0 = (i * nf) % 2
    for copy in weight_copies(1, 1 - slot0):
        copy.start()
    for copy in weight_copies(0, slot0):
        copy.wait()
    for r0 in range(0, tm, FFN_PREP_ROWS):
        xv = x_ref[r0:r0 + FFN_PREP_ROWS, :]
        h_scr[r0:r0 + FFN_PREP_ROWS, :] = _rms(xv, nw_ref[...]).astype(BF16)
        o_ref[r0:r0 + FFN_PREP_ROWS, :] = xv
    compute_tile(0, slot0)

    def column_tile(f, carry):
        slot = (i * nf + f) % 2
        for copy in weight_copies(f + 1, 1 - slot):
            copy.start()
        for copy in weight_copies(f, slot):
            copy.wait()
        compute_tile(f, slot)
        return carry

    lax.fori_loop(1, nf - 1, column_tile, 0)

    last = nf - 1
    slot_last = (i * nf + last) % 2

    @pl.when(i < n_tiles - 1)
    def _():
        for copy in weight_copies(0, 1 - slot_last):
            copy.start()

    for copy in weight_copies(last, slot_last):
        copy.wait()

    def final_rms(r0, rows):
        for b0 in range(r0, r0 + rows, FFN_PREP_ROWS):
            blk = slice(b0, b0 + FFN_PREP_ROWS)
            o_ref[blk, :] = _rms(o_ref[blk, :], fnw_ref[...])

    if final_norm:
        compute_tile(last, slot_last, FFN_FINAL_PARTS, final_rms)
    else:
        compute_tile(last, slot_last)


def _ffn(x2d, norm_w, w_gate, w_up, conv_w, conv_b, w_down, final_w, *, seq_len, final_norm):
    m, d = x2d.shape
    d_ff = w_gate.shape[1]
    tm, tf = FFN_ROW_TILE, FFN_TILE
    nf = d_ff // tf
    assert nf >= 3, "first and last column tiles are peeled around a loop over the rest"
    n_tiles = m // tm
    conv_w = conv_w.reshape(CONV_TAPS, nf, tf).transpose(1, 0, 2)
    conv_b = conv_b.reshape(nf, 1, tf)
    kern = functools.partial(_ffn_kernel, tiles_per_seq=seq_len // tm, n_tiles=n_tiles,
                             final_norm=final_norm)
    hbm = pl.BlockSpec(memory_space=pl.ANY)
    return pl.pallas_call(
        kern,
        grid=(n_tiles,),
        in_specs=[
            pl.BlockSpec((tm, d), lambda i: (i, 0)),
            _resident((1, d)),
            hbm, hbm,
            _resident(conv_w.shape),
            _resident(conv_b.shape),
            hbm,
            _resident((1, d)),
        ],
        out_specs=pl.BlockSpec((tm, d), lambda i: (i, 0)),
        out_shape=jax.ShapeDtypeStruct((m, d), F32),
        scratch_shapes=[
            pltpu.VMEM((tm, d), BF16),
            pltpu.VMEM((2, d, tf), BF16),
            pltpu.VMEM((2, d, tf), BF16),
            pltpu.VMEM((2, tf, d), BF16),
            pltpu.SemaphoreType.DMA((3, 2)),
            pltpu.VMEM((nf, SUBLANES, tf), F32),
        ],
        compiler_params=pltpu.CompilerParams(
            dimension_semantics=("arbitrary",),
            vmem_limit_bytes=FFN_VMEM_LIMIT_BYTES),
        name="conv_ffn",
    )(x2d, norm_w, w_gate, w_up, conv_w, conv_b, w_down, final_w)


def kernel(x, mem, hgrn_lb, norm1_w, w_in, hgrn_norm_w, sconv_w, w_out, norm2_w, mem_norm_w,
           wq, wk, wv, wo, norm3_w, w_gate, w_up, ffn_conv_w, ffn_conv_b, w_down, final_norm_w):
    batch, seq_len, d = x.shape
    mem_len = mem.shape[1]
    depth = norm1_w.shape[0]
    assert all(seq_len % t == 0 for t in (MIXER_ROW_TILE, ATTN_ROW_TILE, FFN_ROW_TILE))
    assert MIXER_ROW_TILE % HGRN_BLOCK == 0 and HGRN_BLOCK % HGRN_CHUNK == 0
    assert hgrn_lb.shape[1] == HGRN_HEADS * LANES and HGRN_HEADS % 2 == 0
    assert w_in.shape[2] == N_PROJ_GROUPS * HGRN_HEADS * LANES and w_gate.shape[2] % FFN_TILE == 0

    xs = x.reshape(batch * seq_len, d)
    mem2d = mem.reshape(batch * mem_len, d)
    row = lambda w: w.reshape(1, -1)
    for l in range(depth):
        mixer = functools.partial(
            _mixer, xs, row(norm1_w[l]), w_in[l].astype(BF16), w_out[l].astype(BF16), hgrn_lb,
            row(hgrn_norm_w[l]), sconv_w[l], seq_len=seq_len, layer=l)
        fast, b_min, (wq_b, wo_b) = mixer((wq[l], wo[l]), stable=False)
        xs = lax.cond(jnp.min(b_min) < -HGRN_SAFE_LOG_DECAY,
                      lambda: mixer(stable=True)[0], lambda: fast)
        kv = _kv_proj(mem2d, row(mem_norm_w[l]), wk[l], wv[l])
        kv = kv.reshape(2, batch, mem_len, d)
        xs, (wg_b, wu_b, wd_b) = _attn(
            xs, row(norm2_w[l]), wq_b, kv, wo_b,
            (w_gate[l], w_up[l], w_down[l]), seq_len=seq_len, mem_len=mem_len)
        xs = _ffn(xs, row(norm3_w[l]), wg_b, wu_b, ffn_conv_w[l], row(ffn_conv_b[l]), wd_b,
                  row(final_norm_w), seq_len=seq_len, final_norm=(l == depth - 1))
    return xs.reshape(batch, seq_len, d)
```

```python
import functools

import jax
import jax.numpy as jnp
from jax import lax
from jax.experimental import pallas as pl
from jax.experimental.pallas import tpu as pltpu

F32 = jnp.float32
BF16 = jnp.bfloat16

EPS = 1e-6
HGRN_HEADS = 8
HGRN_CHUNK = 64
HGRN_SAFE_LOG_DECAY = 75.0
HGRN_BLOCK = 128
CONV_TAPS = 3
MEM_HEADS = 4
N_PROJ_GROUPS = 7
LANES = 128
SUBLANES = 8
BF16_ROWS = 16
KV_VMEM_LIMIT_BYTES = 56 * 1024 * 1024
MIXER_VMEM_LIMIT_BYTES = 60 * 1024 * 1024
FFN_VMEM_LIMIT_BYTES = 60 * 1024 * 1024
ATTN_VMEM_LIMIT_BYTES = 62 * 1024 * 1024

KV_COL_TILE = 512
MIXER_ROW_TILE = 256
ATTN_ROW_TILE = 512
FFN_ROW_TILE = 1024
FFN_TILE = 512
FFN_PREP_ROWS = 128
FFN_FINAL_PARTS = 2


def _rms(x, w):
    ms = jnp.mean(x * x, axis=-1, keepdims=True)
    return x * lax.rsqrt(ms + EPS) * w


def _silu(x):
    return x * jax.nn.sigmoid(x)


def _dot(a, b):
    return jnp.dot(a, b, preferred_element_type=F32)


def _dot_nt(a, b):
    return lax.dot_general(a, b, (((1,), (1,)), ((), ())), preferred_element_type=F32)


def _dot_tn(a, b):
    return lax.dot_general(a, b, (((0,), (0,)), ((), ())), preferred_element_type=F32)


def _shift_rows(cur, prev8, shift):
    rolled = pltpu.roll(cur, shift, axis=0)
    head = pltpu.roll(prev8, shift, axis=0)
    rid = lax.broadcasted_iota(jnp.int32, prev8.shape, 0)
    first = jnp.where(rid < shift, head, rolled[:SUBLANES])
    return jnp.concatenate([first, rolled[SUBLANES:]], axis=0)


def _causal_conv3(cur, prev8, w):
    return (w[0:1] * _shift_rows(cur, prev8, 2)
            + w[1:2] * _shift_rows(cur, prev8, 1)
            + w[2:3] * cur)


def _normalised_tile(x_ref, nw_ref, h_scr):
    h_scr[...] = _rms(x_ref[...], nw_ref[...]).astype(BF16)
    return h_scr[...]


def _cast_specs(to_cast, n_steps):
    def spec(w):
        rows = w.shape[0]
        rb = pl.cdiv(pl.cdiv(rows, n_steps), BF16_ROWS) * BF16_ROWS
        while rows % rb:
            rb += BF16_ROWS
        last = rows // rb - 1
        return pl.BlockSpec((rb, w.shape[1]), lambda i: (jnp.minimum(i, last), 0))

    return [spec(w) for w in to_cast]


def _cast_blocks(src_refs, dst_refs):
    for src, dst in zip(src_refs, dst_refs):
        dst[...] = src[...].astype(dst.dtype)


def _resident(shape):
    return pl.BlockSpec(shape, lambda i: (0,) * len(shape), pipeline_mode=pl.Buffered(1))


def _kv_kernel(mem_ref, nw_ref, wk_ref, wv_ref, o_ref, h_scr):
    @pl.when(pl.program_id(0) == 0)
    def _():
        h_scr[...] = _rms(mem_ref[...], nw_ref[...]).astype(BF16)

    h = h_scr[...]
    o_ref[0] = _dot(h, wk_ref[...].astype(BF16)).astype(o_ref.dtype)
    o_ref[1] = _dot(h, wv_ref[...].astype(BF16)).astype(o_ref.dtype)


def _kv_proj(mem2d, norm_w, wk, wv):
    rows, d = mem2d.shape
    tn = KV_COL_TILE
    return pl.pallas_call(
        _kv_kernel,
        grid=(d // tn,),
        in_specs=[
            pl.BlockSpec((rows, d), lambda n: (0, 0)),
            pl.BlockSpec((1, d), lambda n: (0, 0)),
            pl.BlockSpec((d, tn), lambda n: (0, n)),
            pl.BlockSpec((d, tn), lambda n: (0, n)),
        ],
        out_specs=pl.BlockSpec((2, rows, tn), lambda n: (0, 0, n)),
        out_shape=jax.ShapeDtypeStruct((2, rows, d), BF16),
        scratch_shapes=[pltpu.VMEM((rows, d), BF16)],
        compiler_params=pltpu.CompilerParams(
            dimension_semantics=("arbitrary",),
            vmem_limit_bytes=KV_VMEM_LIMIT_BYTES),
        name="kv_proj",
    )(mem2d, norm_w, wk, wv)


def _chunk_cumsum(x):
    pos = lax.broadcasted_iota(jnp.int32, x.shape, 0) % HGRN_CHUNK
    s = 1
    while s < HGRN_CHUNK:
        x = x + jnp.where(pos >= s, pltpu.roll(x, s, axis=0), 0.0)
        s *= 2
    return x


def _block_diag2(a, b):
    z = jnp.zeros(a.shape, a.dtype)
    return jnp.concatenate([jnp.concatenate([a, z], axis=1),
                            jnp.concatenate([z, b], axis=1)], axis=0)


def _intra_chunk_by_offset(qf, kk, b, v):
    pos = lax.broadcasted_iota(jnp.int32, qf.shape, 0) % HGRN_CHUNK
    ones = jnp.ones((LANES, LANES), BF16)
    head_sum = _block_diag2(ones, ones)

    def one_offset(d, acc):
        in_chunk = pos >= d
        decay = jnp.exp(jnp.where(in_chunk, b - pltpu.roll(b, d, axis=0), 0.0))
        w = jnp.where(in_chunk, qf * pltpu.roll(kk, d, axis=0) * decay, 0.0)
        return acc + _dot(w.astype(BF16), head_sum) * pltpu.roll(v, d, axis=0)

    return lax.fori_loop(0, HGRN_CHUNK, one_offset, jnp.zeros(qf.shape, F32))


def _mixer_kernel(x_ref, nw_ref, win_ref, wout_ref, lb_ref, hnw_ref, scw_ref, *rest,
                  tiles_per_seq, layer, stable, n_cast):
    cast_in, (o_ref, bmin_ref) = rest[:n_cast], rest[n_cast:n_cast + 2]
    cast_out = rest[n_cast + 2:2 * n_cast + 2]
    h_scr, state_scr, carry_scr = rest[2 * n_cast + 2:]
    _cast_blocks(cast_in, cast_out)
    i = pl.program_id(0)
    tm = x_ref.shape[0]
    hd = LANES
    gw = 2 * hd
    width = lb_ref.shape[1]
    n_pairs = width // gw
    n_chunks = tm // HGRN_CHUNK
    chunks = [slice(c * HGRN_CHUNK, (c + 1) * HGRN_CHUNK) for c in range(n_chunks)]
    blocks = [slice(r0, r0 + HGRN_BLOCK) for r0 in range(0, tm, HGRN_BLOCK)]

    @pl.when(i % tiles_per_seq == 0)
    def _():
        state_scr[...] = jnp.zeros(state_scr.shape, F32)
        carry_scr[...] = jnp.zeros(carry_scr.shape, F32)

    lb_all = jnp.sum(jax.nn.softmax(lb_ref[...], axis=0)[0:layer + 1], axis=0, keepdims=True)
    hnw = hnw_ref[...]
    scw = scw_ref[...]
    if not stable:
        t_id = lax.broadcasted_iota(jnp.int32, (HGRN_BLOCK, 2 * HGRN_BLOCK), 0)
        s_id = lax.broadcasted_iota(jnp.int32, (HGRN_BLOCK, 2 * HGRN_BLOCK), 1) % HGRN_BLOCK
        keep = ((t_id // HGRN_CHUNK) == (s_id // HGRN_CHUNK)) & (s_id <= t_id)

    h = _normalised_tile(x_ref, nw_ref, h_scr)
    gated, convs, b_mins = [], [], []
    for p in range(n_pairs):
        lanes = slice(p * gw, (p + 1) * gw)

        def proj(group):
            return _dot(h, win_ref[:, group * width + p * gw:group * width + (p + 1) * gw])

        f_pre = proj(1)
        q = proj(0)
        lb = lb_all[:, lanes]
        f = lb + (1.0 - lb) * jax.nn.sigmoid(f_pre)
        logf = jnp.log(f)
        kk = 1.0 - f
        qf = _silu(q)
        b = _chunk_cumsum(logf)
        v = proj(2)
        vb = v.astype(BF16)
        g = proj(3)
        cb = proj(4)
        cc = proj(5)
        qt = (qf * jnp.exp(b)).astype(BF16)

        if not stable:
            kt = (kk * jnp.exp(-b)).astype(BF16)
            scores = [_dot_nt(qt[rows], _block_diag2(kt[rows, :hd], kt[rows, hd:]))
                      for rows in blocks]

        b_lasts = [b[rows][HGRN_CHUNK - 1:HGRN_CHUNK] for rows in chunks]
        upds = [_dot_tn(vb[rows], (kk[rows] * jnp.exp(bl - b[rows])).astype(BF16))
                for rows, bl in zip(chunks, b_lasts)]
        ch = proj(6)

        if stable:
            intra = _intra_chunk_by_offset(qf, kk, b, v)
        else:
            intra = jnp.concatenate(
                [_dot(jnp.where(keep, sc, 0.0).astype(BF16),
                      _block_diag2(vb[rows, :hd], vb[rows, hd:]))
                 for rows, sc in zip(blocks, scores)], axis=0)

        st0 = state_scr[2 * p]
        st1 = state_scr[2 * p + 1]
        starts = []
        for upd, bl in zip(upds, b_lasts):
            starts.append(_block_diag2(st0.astype(BF16), st1.astype(BF16)))
            decay = jnp.exp(bl)
            st0 = st0 * decay[:, :hd] + upd[:hd, :hd]
            st1 = st1 * decay[:, hd:] + upd[hd:, hd:]
        state_scr[2 * p] = st0
        state_scr[2 * p + 1] = st1
        inter = [_dot_nt(qt[rows], start) for rows, start in zip(chunks, starts)]

        u = cc * ch
        prev8 = carry_scr[p]
        carry_scr[p] = u[tm - SUBLANES:]
        convs.append((cb * _causal_conv3(u, prev8, scw[:, lanes])).astype(BF16))

        o = intra + jnp.concatenate(inter, axis=0)
        o = jnp.concatenate([_rms(o[:, :hd], hnw), _rms(o[:, hd:], hnw)], axis=1)
        gated.append((o * _silu(g)).astype(BF16))
        b_mins.append(functools.reduce(jnp.minimum, b_lasts))

    mixed = jnp.concatenate(gated + convs, axis=1)
    o_ref[...] = x_ref[...] + _dot(mixed, wout_ref[...])

    b_min = jnp.concatenate(b_mins, axis=1)

    @pl.when(i == 0)
    def _():
        bmin_ref[...] = b_min

    @pl.when(i > 0)
    def _():
        bmin_ref[...] = jnp.minimum(bmin_ref[...], b_min)


def _mixer(x2d, norm_w, w_in, w_out, hgrn_lb, hgrn_norm_w, sconv_w, to_cast=(), *,
           seq_len, layer, stable):
    m, d = x2d.shape
    tm = MIXER_ROW_TILE
    width = hgrn_lb.shape[1]
    n_tiles = m // tm
    cast_specs = _cast_specs(to_cast, n_tiles)
    kern = functools.partial(_mixer_kernel, tiles_per_seq=seq_len // tm, layer=layer,
                             stable=stable, n_cast=len(to_cast))
    out, b_min, *casted = pl.pallas_call(
        kern,
        grid=(n_tiles,),
        in_specs=[
            pl.BlockSpec((tm, d), lambda i: (i, 0)),
            _resident((1, d)),
            _resident(w_in.shape),
            _resident(w_out.shape),
            _resident(hgrn_lb.shape),
            _resident((1, LANES)),
            _resident(sconv_w.shape),
            *cast_specs,
        ],
        out_specs=[pl.BlockSpec((tm, d), lambda i: (i, 0)),
                   pl.BlockSpec((1, width), lambda i: (0, 0)),
                   *cast_specs],
        out_shape=[jax.ShapeDtypeStruct((m, d), F32),
                   jax.ShapeDtypeStruct((1, width), F32),
                   *(jax.ShapeDtypeStruct(w.shape, BF16) for w in to_cast)],
        scratch_shapes=[
            pltpu.VMEM((tm, d), BF16),
            pltpu.VMEM((HGRN_HEADS, LANES, LANES), F32),
            pltpu.VMEM((width // (2 * LANES), SUBLANES, 2 * LANES), F32),
        ],
        compiler_params=pltpu.CompilerParams(
            dimension_semantics=("arbitrary",),
            vmem_limit_bytes=MIXER_VMEM_LIMIT_BYTES),
        name="hybrid_mixer_stable" if stable else "hybrid_mixer",
    )(x2d, norm_w, w_in, w_out, hgrn_lb, hgrn_norm_w, sconv_w, *to_cast)
    return out, b_min, casted


def _attn_kernel(x_ref, nw_ref, wq_ref, k_ref, v_ref, wo_ref, *rest, head_dim, n_cast):
    cast_in, o_ref = rest[:n_cast], rest[n_cast]
    cast_out, h_scr = rest[n_cast + 1:2 * n_cast + 1], rest[2 * n_cast + 1]
    h = _normalised_tile(x_ref, nw_ref, h_scr)
    scale = head_dim ** -0.5
    cols = [slice(hh * head_dim, (hh + 1) * head_dim) for hh in range(MEM_HEADS)]
    qs = [_dot(h, wq_ref[:, c]).astype(BF16) for c in cols]
    _cast_blocks(cast_in, cast_out)
    ss = [_dot_nt(q, k_ref[:, c]) * scale for q, c in zip(qs, cols)]
    out = x_ref[...]
    for s, c in zip(ss, cols):
        e = jnp.exp(s - jnp.max(s, axis=-1, keepdims=True))
        pr = (e / jnp.sum(e, axis=-1, keepdims=True)).astype(BF16)
        out = out + _dot(_dot(pr, v_ref[:, c]).astype(BF16), wo_ref[c, :])
    o_ref[...] = out


def _attn(x2d, norm_w, wq, kv, wo, to_cast, *, seq_len, mem_len):
    m, d = x2d.shape
    tm = ATTN_ROW_TILE
    tiles_per_seq = seq_len // tm
    n_tiles = m // tm

    cast_specs = _cast_specs(to_cast, n_tiles)
    kern = functools.partial(_attn_kernel, head_dim=d // MEM_HEADS, n_cast=len(to_cast))
    out, *casted = pl.pallas_call(
        kern,
        grid=(n_tiles,),
        in_specs=[
            pl.BlockSpec((tm, d), lambda i: (i, 0)),
            _resident((1, d)),
            _resident((d, d)),
            pl.BlockSpec((None, None, mem_len, d), lambda i: (0, i // tiles_per_seq, 0, 0)),
            pl.BlockSpec((None, None, mem_len, d), lambda i: (1, i // tiles_per_seq, 0, 0)),
            _resident((d, d)),
            *cast_specs,
        ],
        out_specs=[pl.BlockSpec((tm, d), lambda i: (i, 0)), *cast_specs],
        out_shape=[jax.ShapeDtypeStruct((m, d), F32),
                   *(jax.ShapeDtypeStruct(w.shape, BF16) for w in to_cast)],
        scratch_shapes=[pltpu.VMEM((tm, d), BF16)],
        compiler_params=pltpu.CompilerParams(
            dimension_semantics=("arbitrary",),
            vmem_limit_bytes=ATTN_VMEM_LIMIT_BYTES),
        name="mem_cross_attn",
    )(x2d, norm_w, wq, kv, kv, wo, *to_cast)
    return out, casted


def _ffn_kernel(x_ref, nw_ref, wg_hbm, wu_hbm, cw_ref, cb_ref, wd_hbm, fnw_ref, o_ref,
                h_scr, wg_buf, wu_buf, wd_buf, sems, carry_scr, *,
                tiles_per_seq, n_tiles, final_norm):
    i = pl.program_id(0)
    tm = x_ref.shape[0]
    nf, _, tf = cw_ref.shape

    def weight_copies(f, slot):
        cols = pl.ds(f * tf if isinstance(f, int) else pl.multiple_of(f * tf, tf), tf)
        return (pltpu.make_async_copy(wg_hbm.at[:, cols], wg_buf.at[slot], sems.at[0, slot]),
                pltpu.make_async_copy(wu_hbm.at[:, cols], wu_buf.at[slot], sems.at[1, slot]),
                pltpu.make_async_copy(wd_hbm.at[cols, :], wd_buf.at[slot], sems.at[2, slot]))

    @pl.when(i == 0)
    def _():
        for copy in weight_copies(0, 0):
            copy.start()

    @pl.when(i % tiles_per_seq == 0)
    def _():
        carry_scr[...] = jnp.zeros(carry_scr.shape, F32)

    def compute_tile(f, slot, row_parts=1, finish_rows=None):
        h = h_scr[...]
        a = _dot(h, wg_buf[slot])
        up = _dot(h, wu_buf[slot])
        prev8 = carry_scr[f]
        carry_scr[f] = a[tm - SUBLANES:]
        act = (_silu(_causal_conv3(a, prev8, cw_ref[f]) + cb_ref[f]) * up).astype(BF16)
        part = tm // row_parts
        for r0 in range(0, tm, part):
            o_ref[r0:r0 + part, :] += _dot(act[r0:r0 + part], wd_buf[slot])
            if finish_rows is not None:
                finish_rows(r0, part)

    slot0 = (i * nf) % 2
    for copy in weight_copies(1, 1 - slot0):
        copy.start()
    for copy in weight_copies(0, slot0):
        copy.wait()
    for r0 in range(0, tm, FFN_PREP_ROWS):
        xv = x_ref[r0:r0 + FFN_PREP_ROWS, :]
        h_scr[r0:r0 + FFN_PREP_ROWS, :] = _rms(xv, nw_ref[...]).astype(BF16)
        o_ref[r0:r0 + FFN_PREP_ROWS, :] = xv
    compute_tile(0, slot0)

    def column_tile(f, carry):
        slot = (i * nf + f) % 2
        for copy in weight_copies(f + 1, 1 - slot):
            copy.start()
        for copy in weight_copies(f, slot):
            copy.wait()
        compute_tile(f, slot)
        return carry

    lax.fori_loop(1, nf - 1, column_tile, 0)

    last = nf - 1
    slot_last = (i * nf + last) % 2

    @pl.when(i < n_tiles - 1)
    def _():
        for copy in weight_copies(0, 1 - slot_last):
            copy.start()

    for copy in weight_copies(last, slot_last):
        copy.wait()

    def final_rms(r0, rows):
        for b0 in range(r0, r0 + rows, FFN_PREP_ROWS):
            blk = slice(b0, b0 + FFN_PREP_ROWS)
            o_ref[blk, :] = _rms(o_ref[blk, :], fnw_ref[...])

    if final_norm:
        compute_tile(last, slot_last, FFN_FINAL_PARTS, final_rms)
    else:
        compute_tile(last, slot_last)


def _ffn(x2d, norm_w, w_gate, w_up, conv_w, conv_b, w_down, final_w, *, seq_len, final_norm):
    m, d = x2d.shape
    d_ff = w_gate.shape[1]
    tm, tf = FFN_ROW_TILE, FFN_TILE
    nf = d_ff // tf
    assert nf >= 3, "first and last column tiles are peeled around a loop over the rest"
    n_tiles = m // tm
    conv_w = conv_w.reshape(CONV_TAPS, nf, tf).transpose(1, 0, 2)
    conv_b = conv_b.reshape(nf, 1, tf)
    kern = functools.partial(_ffn_kernel, tiles_per_seq=seq_len // tm, n_tiles=n_tiles,
                             final_norm=final_norm)
    hbm = pl.BlockSpec(memory_space=pl.ANY)
    return pl.pallas_call(
        kern,
        grid=(n_tiles,),
        in_specs=[
            pl.BlockSpec((tm, d), lambda i: (i, 0)),
            _resident((1, d)),
            hbm, hbm,
            _resident(conv_w.shape),
            _resident(conv_b.shape),
            hbm,
            _resident((1, d)),
        ],
        out_specs=pl.BlockSpec((tm, d), lambda i: (i, 0)),
        out_shape=jax.ShapeDtypeStruct((m, d), F32),
        scratch_shapes=[
            pltpu.VMEM((tm, d), BF16),
            pltpu.VMEM((2, d, tf), BF16),
            pltpu.VMEM((2, d, tf), BF16),
            pltpu.VMEM((2, tf, d), BF16),
            pltpu.SemaphoreType.DMA((3, 2)),
            pltpu.VMEM((nf, SUBLANES, tf), F32),
        ],
        compiler_params=pltpu.CompilerParams(
            dimension_semantics=("arbitrary",),
            vmem_limit_bytes=FFN_VMEM_LIMIT_BYTES),
        name="conv_ffn",
    )(x2d, norm_w, w_gate, w_up, conv_w, conv_b, w_down, final_w)


def kernel(x, mem, hgrn_lb, norm1_w, w_in, hgrn_norm_w, sconv_w, w_out, norm2_w, mem_norm_w,
           wq, wk, wv, wo, norm3_w, w_gate, w_up, ffn_conv_w, ffn_conv_b, w_down, final_norm_w):
    batch, seq_len, d = x.shape
    mem_len = mem.shape[1]
    depth = norm1_w.shape[0]
    assert all(seq_len % t == 0 for t in (MIXER_ROW_TILE, ATTN_ROW_TILE, FFN_ROW_TILE))
    assert MIXER_ROW_TILE % HGRN_BLOCK == 0 and HGRN_BLOCK % HGRN_CHUNK == 0
    assert hgrn_lb.shape[1] == HGRN_HEADS * LANES and HGRN_HEADS % 2 == 0
    assert w_in.shape[2] == N_PROJ_GROUPS * HGRN_HEADS * LANES and w_gate.shape[2] % FFN_TILE == 0

    xs = x.reshape(batch * seq_len, d)
    mem2d = mem.reshape(batch * mem_len, d)
    row = lambda w: w.reshape(1, -1)
    for l in range(depth):
        mixer = functools.partial(
            _mixer, xs, row(norm1_w[l]), w_in[l].astype(BF16), w_out[l].astype(BF16), hgrn_lb,
            row(hgrn_norm_w[l]), sconv_w[l], seq_len=seq_len, layer=l)
        fast, b_min, (wq_b, wo_b) = mixer((wq[l], wo[l]), stable=False)
        xs = lax.cond(jnp.min(b_min) < -HGRN_SAFE_LOG_DECAY,
                      lambda: mixer(stable=True)[0], lambda: fast)
        kv = _kv_proj(mem2d, row(mem_norm_w[l]), wk[l], wv[l])
        kv = kv.reshape(2, batch, mem_len, d)
        xs, (wg_b, wu_b, wd_b) = _attn(
            xs, row(norm2_w[l]), wq_b, kv, wo_b,
            (w_gate[l], w_up[l], w_down[l]), seq_len=seq_len, mem_len=mem_len)
        xs = _ffn(xs, row(norm3_w[l]), wg_b, wu_b, ffn_conv_w[l], row(ffn_conv_b[l]), wd_b,
                  row(final_norm_w), seq_len=seq_len, final_norm=(l == depth - 1))
    return xs.reshape(batch, seq_len, d)
```

```python
import functools

import jax
import jax.numpy as jnp
from jax import lax
from jax.experimental import pallas as pl
from jax.experimental.pallas import tpu as pltpu

F32 = jnp.float32
BF16 = jnp.bfloat16

EPS = 1e-6
HGRN_HEADS = 8
HGRN_CHUNK = 64
HGRN_SAFE_LOG_DECAY = 75.0
HGRN_BLOCK = 128
CONV_TAPS = 3
MEM_HEADS = 4
N_PROJ_GROUPS = 7
LANES = 128
SUBLANES = 8
BF16_ROWS = 16
KV_VMEM_LIMIT_BYTES = 56 * 1024 * 1024
MIXER_VMEM_LIMIT_BYTES = 60 * 1024 * 1024
FFN_VMEM_LIMIT_BYTES = 60 * 1024 * 1024
ATTN_VMEM_LIMIT_BYTES = 62 * 1024 * 1024

KV_COL_TILE = 512
MIXER_ROW_TILE = 256
ATTN_ROW_TILE = 512
FFN_ROW_TILE = 1024
FFN_TILE = 512
FFN_PREP_ROWS = 128
FFN_FINAL_PARTS = 2


def _rms(x, w):
    ms = jnp.mean(x * x, axis=-1, keepdims=True)
    return x * lax.rsqrt(ms + EPS) * w


def _silu(x):
    return x * jax.nn.sigmoid(x)


def _dot(a, b):
    return jnp.dot(a, b, preferred_element_type=F32)


def _dot_nt(a, b):
    return lax.dot_general(a, b, (((1,), (1,)), ((), ())), preferred_element_type=F32)


def _dot_tn(a, b):
    return lax.dot_general(a, b, (((0,), (0,)), ((), ())), preferred_element_type=F32)


def _shift_rows(cur, prev8, shift):
    rolled = pltpu.roll(cur, shift, axis=0)
    head = pltpu.roll(prev8, shift, axis=0)
    rid = lax.broadcasted_iota(jnp.int32, prev8.shape, 0)
    first = jnp.where(rid < shift, head, rolled[:SUBLANES])
    return jnp.concatenate([first, rolled[SUBLANES:]], axis=0)


def _causal_conv3(cur, prev8, w):
    return (w[0:1] * _shift_rows(cur, prev8, 2)
            + w[1:2] * _shift_rows(cur, prev8, 1)
            + w[2:3] * cur)


def _normalised_tile(x_ref, nw_ref, h_scr):
    h_scr[...] = _rms(x_ref[...], nw_ref[...]).astype(BF16)
    return h_scr[...]


def _cast_specs(to_cast, n_steps):
    def spec(w):
        rows = w.shape[0]
        rb = pl.cdiv(pl.cdiv(rows, n_steps), BF16_ROWS) * BF16_ROWS
        while rows % rb:
            rb += BF16_ROWS
        last = rows // rb - 1
        return pl.BlockSpec((rb, w.shape[1]), lambda i: (jnp.minimum(i, last), 0))

    return [spec(w) for w in to_cast]


def _cast_blocks(src_refs, dst_refs):
    for src, dst in zip(src_refs, dst_refs):
        dst[...] = src[...].astype(dst.dtype)


def _resident(shape):
    return pl.BlockSpec(shape, lambda i: (0,) * len(shape), pipeline_mode=pl.Buffered(1))


def _kv_kernel(mem_ref, nw_ref, wk_ref, wv_ref, o_ref, h_scr):
    @pl.when(pl.program_id(0) == 0)
    def _():
        h_scr[...] = _rms(mem_ref[...], nw_ref[...]).astype(BF16)

    h = h_scr[...]
    o_ref[0] = _dot(h, wk_ref[...].astype(BF16)).astype(o_ref.dtype)
    o_ref[1] = _dot(h, wv_ref[...].astype(BF16)).astype(o_ref.dtype)


def _kv_proj(mem2d, norm_w, wk, wv):
    rows, d = mem2d.shape
    tn = KV_COL_TILE
    return pl.pallas_call(
        _kv_kernel,
        grid=(d // tn,),
        in_specs=[
            pl.BlockSpec((rows, d), lambda n: (0, 0)),
            pl.BlockSpec((1, d), lambda n: (0, 0)),
            pl.BlockSpec((d, tn), lambda n: (0, n)),
            pl.BlockSpec((d, tn), lambda n: (0, n)),
        ],
        out_specs=pl.BlockSpec((2, rows, tn), lambda n: (0, 0, n)),
        out_shape=jax.ShapeDtypeStruct((2, rows, d), BF16),
        scratch_shapes=[pltpu.VMEM((rows, d), BF16)],
        compiler_params=pltpu.CompilerParams(
            dimension_semantics=("arbitrary",),
            vmem_limit_bytes=KV_VMEM_LIMIT_BYTES),
        name="kv_proj",
    )(mem2d, norm_w, wk, wv)


def _chunk_cumsum(x):
    pos = lax.broadcasted_iota(jnp.int32, x.shape, 0) % HGRN_CHUNK
    s = 1
    while s < HGRN_CHUNK:
        x = x + jnp.where(pos >= s, pltpu.roll(x, s, axis=0), 0.0)
        s *= 2
    return x


def _block_diag2(a, b):
    z = jnp.zeros(a.shape, a.dtype)
    return jnp.concatenate([jnp.concatenate([a, z], axis=1),
                            jnp.concatenate([z, b], axis=1)], axis=0)


def _intra_chunk_by_offset(qf, kk, b, v):
    pos = lax.broadcasted_iota(jnp.int32, qf.shape, 0) % HGRN_CHUNK
    ones = jnp.ones((LANES, LANES), BF16)
    head_sum = _block_diag2(ones, ones)

    def one_offset(d, acc):
        in_chunk = pos >= d
        decay = jnp.exp(jnp.where(in_chunk, b - pltpu.roll(b, d, axis=0), 0.0))
        w = jnp.where(in_chunk, qf * pltpu.roll(kk, d, axis=0) * decay, 0.0)
        return acc + _dot(w.astype(BF16), head_sum) * pltpu.roll(v, d, axis=0)

    return lax.fori_loop(0, HGRN_CHUNK, one_offset, jnp.zeros(qf.shape, F32))


def _mixer_kernel(x_ref, nw_ref, win_ref, wout_ref, lb_ref, hnw_ref, scw_ref, *rest,
                  tiles_per_seq, layer, stable, n_cast):
    cast_in, (o_ref, bmin_ref) = rest[:n_cast], rest[n_cast:n_cast + 2]
    cast_out = rest[n_cast + 2:2 * n_cast + 2]
    h_scr, state_scr, carry_scr = rest[2 * n_cast + 2:]
    _cast_blocks(cast_in, cast_out)
    i = pl.program_id(0)
    tm = x_ref.shape[0]
    hd = LANES
    gw = 2 * hd
    width = lb_ref.shape[1]
    n_pairs = width // gw
    n_chunks = tm // HGRN_CHUNK
    chunks = [slice(c * HGRN_CHUNK, (c + 1) * HGRN_CHUNK) for c in range(n_chunks)]
    blocks = [slice(r0, r0 + HGRN_BLOCK) for r0 in range(0, tm, HGRN_BLOCK)]

    @pl.when(i % tiles_per_seq == 0)
    def _():
        state_scr[...] = jnp.zeros(state_scr.shape, F32)
        carry_scr[...] = jnp.zeros(carry_scr.shape, F32)

    lb_all = jnp.sum(jax.nn.softmax(lb_ref[...], axis=0)[0:layer + 1], axis=0, keepdims=True)
    hnw = hnw_ref[...]
    scw = scw_ref[...]
    if not stable:
        t_id = lax.broadcasted_iota(jnp.int32, (HGRN_BLOCK, 2 * HGRN_BLOCK), 0)
        s_id = lax.broadcasted_iota(jnp.int32, (HGRN_BLOCK, 2 * HGRN_BLOCK), 1) % HGRN_BLOCK
        keep = ((t_id // HGRN_CHUNK) == (s_id // HGRN_CHUNK)) & (s_id <= t_id)

    h = _normalised_tile(x_ref, nw_ref, h_scr)
    gated, convs, b_mins = [], [], []
    for p in range(n_pairs):
        lanes = slice(p * gw, (p + 1) * gw)

        def proj(group):
            return _dot(h, win_ref[:, group * width + p * gw:group * width + (p + 1) * gw])

        f_pre = proj(1)
        q = proj(0)
        lb = lb_all[:, lanes]
        f = lb + (1.0 - lb) * jax.nn.sigmoid(f_pre)
        logf = jnp.log(f)
        kk = 1.0 - f
        qf = _silu(q)
        b = _chunk_cumsum(logf)
        v = proj(2)
        vb = v.astype(BF16)
        g = proj(3)
        cb = proj(4)
        cc = proj(5)
        qt = (qf * jnp.exp(b)).astype(BF16)

        if not stable:
            kt = (kk * jnp.exp(-b)).astype(BF16)
            scores = [_dot_nt(qt[rows], _block_diag2(kt[rows, :hd], kt[rows, hd:]))
                      for rows in blocks]

        b_lasts = [b[rows][HGRN_CHUNK - 1:HGRN_CHUNK] for rows in chunks]
        upds = [_dot_tn(vb[rows], (kk[rows] * jnp.exp(bl - b[rows])).astype(BF16))
                for rows, bl in zip(chunks, b_lasts)]
        ch = proj(6)

        if stable:
            intra = _intra_chunk_by_offset(qf, kk, b, v)
        else:
            intra = jnp.concatenate(
                [_dot(jnp.where(keep, sc, 0.0).astype(BF16),
                      _block_diag2(vb[rows, :hd], vb[rows, hd:]))
                 for rows, sc in zip(blocks, scores)], axis=0)

        st0 = state_scr[2 * p]
        st1 = state_scr[2 * p + 1]
        starts = []
        for upd, bl in zip(upds, b_lasts):
            starts.append(_block_diag2(st0.astype(BF16), st1.astype(BF16)))
            decay = jnp.exp(bl)
            st0 = st0 * decay[:, :hd] + upd[:hd, :hd]
            st1 = st1 * decay[:, hd:] + upd[hd:, hd:]
        state_scr[2 * p] = st0
        state_scr[2 * p + 1] = st1
        inter = [_dot_nt(qt[rows], start) for rows, start in zip(chunks, starts)]

        u = cc * ch
        prev8 = carry_scr[p]
        carry_scr[p] = u[tm - SUBLANES:]
        convs.append((cb * _causal_conv3(u, prev8, scw[:, lanes])).astype(BF16))

        o = intra + jnp.concatenate(inter, axis=0)
        o = jnp.concatenate([_rms(o[:, :hd], hnw), _rms(o[:, hd:], hnw)], axis=1)
        gated.append((o * _silu(g)).astype(BF16))
        b_mins.append(functools.reduce(jnp.minimum, b_lasts))

    mixed = jnp.concatenate(gated + convs, axis=1)
    o_ref[...] = x_ref[...] + _dot(mixed, wout_ref[...])

    b_min = jnp.concatenate(b_mins, axis=1)

    @pl.when(i == 0)
    def _():
        bmin_ref[...] = b_min

    @pl.when(i > 0)
    def _():
        bmin_ref[...] = jnp.minimum(bmin_ref[...], b_min)


def _mixer(x2d, norm_w, w_in, w_out, hgrn_lb, hgrn_norm_w, sconv_w, to_cast=(), *,
           seq_len, layer, stable):
    m, d = x2d.shape
    tm = MIXER_ROW_TILE
    width = hgrn_lb.shape[1]
    n_tiles = m // tm
    cast_specs = _cast_specs(to_cast, n_tiles)
    kern = functools.partial(_mixer_kernel, tiles_per_seq=seq_len // tm, layer=layer,
                             stable=stable, n_cast=len(to_cast))
    out, b_min, *casted = pl.pallas_call(
        kern,
        grid=(n_tiles,),
        in_specs=[
            pl.BlockSpec((tm, d), lambda i: (i, 0)),
            _resident((1, d)),
            _resident(w_in.shape),
            _resident(w_out.shape),
            _resident(hgrn_lb.shape),
            _resident((1, LANES)),
            _resident(sconv_w.shape),
            *cast_specs,
        ],
        out_specs=[pl.BlockSpec((tm, d), lambda i: (i, 0)),
                   pl.BlockSpec((1, width), lambda i: (0, 0)),
                   *cast_specs],
        out_shape=[jax.ShapeDtypeStruct((m, d), F32),
                   jax.ShapeDtypeStruct((1, width), F32),
                   *(jax.ShapeDtypeStruct(w.shape, BF16) for w in to_cast)],
        scratch_shapes=[
            pltpu.VMEM((tm, d), BF16),
            pltpu.VMEM((HGRN_HEADS, LANES, LANES), F32),
            pltpu.VMEM((width // (2 * LANES), SUBLANES, 2 * LANES), F32),
        ],
        compiler_params=pltpu.CompilerParams(
            dimension_semantics=("arbitrary",),
            vmem_limit_bytes=MIXER_VMEM_LIMIT_BYTES),
        name="hybrid_mixer_stable" if stable else "hybrid_mixer",
    )(x2d, norm_w, w_in, w_out, hgrn_lb, hgrn_norm_w, sconv_w, *to_cast)
    return out, b_min, casted


def _attn_kernel(x_ref, nw_ref, wq_ref, k_ref, v_ref, wo_ref, *rest, head_dim, n_cast):
    cast_in, o_ref = rest[:n_cast], rest[n_cast]
    cast_out, h_scr = rest[n_cast + 1:2 * n_cast + 1], rest[2 * n_cast + 1]
    h = _normalised_tile(x_ref, nw_ref, h_scr)
    scale = head_dim ** -0.5
    cols = [slice(hh * head_dim, (hh + 1) * head_dim) for hh in range(MEM_HEADS)]
    qs = [_dot(h, wq_ref[:, c]).astype(BF16) for c in cols]
    _cast_blocks(cast_in, cast_out)
    ss = [_dot_nt(q, k_ref[:, c]) * scale for q, c in zip(qs, cols)]
    out = x_ref[...]
    for s, c in zip(ss, cols):
        e = jnp.exp(s - jnp.max(s, axis=-1, keepdims=True))
        pr = (e / jnp.sum(e, axis=-1, keepdims=True)).astype(BF16)
        out = out + _dot(_dot(pr, v_ref[:, c]).astype(BF16), wo_ref[c, :])
    o_ref[...] = out


def _attn(x2d, norm_w, wq, kv, wo, to_cast, *, seq_len, mem_len):
    m, d = x2d.shape
    tm = ATTN_ROW_TILE
    tiles_per_seq = seq_len // tm
    n_tiles = m // tm

    cast_specs = _cast_specs(to_cast, n_tiles)
    kern = functools.partial(_attn_kernel, head_dim=d // MEM_HEADS, n_cast=len(to_cast))
    out, *casted = pl.pallas_call(
        kern,
        grid=(n_tiles,),
        in_specs=[
            pl.BlockSpec((tm, d), lambda i: (i, 0)),
            _resident((1, d)),
            _resident((d, d)),
            pl.BlockSpec((None, None, mem_len, d), lambda i: (0, i // tiles_per_seq, 0, 0)),
            pl.BlockSpec((None, None, mem_len, d), lambda i: (1, i // tiles_per_seq, 0, 0)),
            _resident((d, d)),
            *cast_specs,
        ],
        out_specs=[pl.BlockSpec((tm, d), lambda i: (i, 0)), *cast_specs],
        out_shape=[jax.ShapeDtypeStruct((m, d), F32),
                   *(jax.ShapeDtypeStruct(w.shape, BF16) for w in to_cast)],
        scratch_shapes=[pltpu.VMEM((tm, d), BF16)],
        compiler_params=pltpu.CompilerParams(
            dimension_semantics=("arbitrary",),
            vmem_limit_bytes=ATTN_VMEM_LIMIT_BYTES),
        name="mem_cross_attn",
    )(x2d, norm_w, wq, kv, kv, wo, *to_cast)
    return out, casted


def _ffn_kernel(x_ref, nw_ref, wg_hbm, wu_hbm, cw_ref, cb_ref, wd_hbm, fnw_ref, o_ref,
                h_scr, wg_buf, wu_buf, wd_buf, sems, carry_scr, *,
                tiles_per_seq, n_tiles, final_norm):
    i = pl.program_id(0)
    tm = x_ref.shape[0]
    nf, _, tf = cw_ref.shape

    def weight_copies(f, slot):
        cols = pl.ds(f * tf if isinstance(f, int) else pl.multiple_of(f * tf, tf), tf)
        return (pltpu.make_async_copy(wg_hbm.at[:, cols], wg_buf.at[slot], sems.at[0, slot]),
                pltpu.make_async_copy(wu_hbm.at[:, cols], wu_buf.at[slot], sems.at[1, slot]),
                pltpu.make_async_copy(wd_hbm.at[cols, :], wd_buf.at[slot], sems.at[2, slot]))

    def start_copies(copies):
        for k, copy in enumerate(copies):
            copy.start(priority=k % 2)

    @pl.when(i == 0)
    def _():
        start_copies(weight_copies(0, 0))

    @pl.when(i % tiles_per_seq == 0)
    def _():
        carry_scr[...] = jnp.zeros(carry_scr.shape, F32)

    def compute_tile(f, slot, row_parts=1, finish_rows=None):
        h = h_scr[...]
        a = _dot(h, wg_buf[slot])
        up = _dot(h, wu_buf[slot])
        prev8 = carry_scr[f]
        carry_scr[f] = a[tm - SUBLANES:]
        act = (_silu(_causal_conv3(a, prev8, cw_ref[f]) + cb_ref[f]) * up).astype(BF16)
        part = tm // row_parts
        for r0 in range(0, tm, part):
            o_ref[r0:r0 + part, :] += _dot(act[r0:r0 + part], wd_buf[slot])
            if finish_rows is not None:
                finish_rows(r0, part)

    slot0 = (i * nf) % 2
    start_copies(weight_copies(1, 1 - slot0))
    for copy in weight_copies(0, slot0):
        copy.wait()
    for r0 in range(0, tm, FFN_PREP_ROWS):
        xv = x_ref[r0:r0 + FFN_PREP_ROWS, :]
        h_scr[r0:r0 + FFN_PREP_ROWS, :] = _rms(xv, nw_ref[...]).astype(BF16)
        o_ref[r0:r0 + FFN_PREP_ROWS, :] = xv
    compute_tile(0, slot0)

    def column_tile(f, carry):
        slot = (i * nf + f) % 2
        start_copies(weight_copies(f + 1, 1 - slot))
        for copy in weight_copies(f, slot):
            copy.wait()
        compute_tile(f, slot)
        return carry

    lax.fori_loop(1, nf - 1, column_tile, 0)

    last = nf - 1
    slot_last = (i * nf + last) % 2

    @pl.when(i < n_tiles - 1)
    def _():
        start_copies(weight_copies(0, 1 - slot_last))

    for copy in weight_copies(last, slot_last):
        copy.wait()

    def final_rms(r0, rows):
        for b0 in range(r0, r0 + rows, FFN_PREP_ROWS):
            blk = slice(b0, b0 + FFN_PREP_ROWS)
            o_ref[blk, :] = _rms(o_ref[blk, :], fnw_ref[...])

    if final_norm:
        compute_tile(last, slot_last, FFN_FINAL_PARTS, final_rms)
    else:
        compute_tile(last, slot_last)


def _ffn(x2d, norm_w, w_gate, w_up, conv_w, conv_b, w_down, final_w, *, seq_len, final_norm):
    m, d = x2d.shape
    d_ff = w_gate.shape[1]
    tm, tf = FFN_ROW_TILE, FFN_TILE
    nf = d_ff // tf
    assert nf >= 3, "first and last column tiles are peeled around a loop over the rest"
    n_tiles = m // tm
    conv_w = conv_w.reshape(CONV_TAPS, nf, tf).transpose(1, 0, 2)
    conv_b = conv_b.reshape(nf, 1, tf)
    kern = functools.partial(_ffn_kernel, tiles_per_seq=seq_len // tm, n_tiles=n_tiles,
                             final_norm=final_norm)
    hbm = pl.BlockSpec(memory_space=pl.ANY)
    return pl.pallas_call(
        kern,
        grid=(n_tiles,),
        in_specs=[
            pl.BlockSpec((tm, d), lambda i: (i, 0)),
            _resident((1, d)),
            hbm, hbm,
            _resident(conv_w.shape),
            _resident(conv_b.shape),
            hbm,
            _resident((1, d)),
        ],
        out_specs=pl.BlockSpec((tm, d), lambda i: (i, 0)),
        out_shape=jax.ShapeDtypeStruct((m, d), F32),
        scratch_shapes=[
            pltpu.VMEM((tm, d), BF16),
            pltpu.VMEM((2, d, tf), BF16),
            pltpu.VMEM((2, d, tf), BF16),
            pltpu.VMEM((2, tf, d), BF16),
            pltpu.SemaphoreType.DMA((3, 2)),
            pltpu.VMEM((nf, SUBLANES, tf), F32),
        ],
        compiler_params=pltpu.CompilerParams(
            dimension_semantics=("arbitrary",),
            vmem_limit_bytes=FFN_VMEM_LIMIT_BYTES),
        name="conv_ffn",
    )(x2d, norm_w, w_gate, w_up, conv_w, conv_b, w_down, final_w)


def kernel(x, mem, hgrn_lb, norm1_w, w_in, hgrn_norm_w, sconv_w, w_out, norm2_w, mem_norm_w,
           wq, wk, wv, wo, norm3_w, w_gate, w_up, ffn_conv_w, ffn_conv_b, w_down, final_norm_w):
    batch, seq_len, d = x.shape
    mem_len = mem.shape[1]
    depth = norm1_w.shape[0]
    assert all(seq_len % t == 0 for t in (MIXER_ROW_TILE, ATTN_ROW_TILE, FFN_ROW_TILE))
    assert MIXER_ROW_TILE % HGRN_BLOCK == 0 and HGRN_BLOCK % HGRN_CHUNK == 0
    assert hgrn_lb.shape[1] == HGRN_HEADS * LANES and HGRN_HEADS % 2 == 0
    assert w_in.shape[2] == N_PROJ_GROUPS * HGRN_HEADS * LANES and w_gate.shape[2] % FFN_TILE == 0

    xs = x.reshape(batch * seq_len, d)
    mem2d = mem.reshape(batch * mem_len, d)
    row = lambda w: w.reshape(1, -1)
    for l in range(depth):
        mixer = functools.partial(
            _mixer, xs, row(norm1_w[l]), w_in[l].astype(BF16), w_out[l].astype(BF16), hgrn_lb,
            row(hgrn_norm_w[l]), sconv_w[l], seq_len=seq_len, layer=l)
        fast, b_min, (wq_b, wo_b) = mixer((wq[l], wo[l]), stable=False)
        xs = lax.cond(jnp.min(b_min) < -HGRN_SAFE_LOG_DECAY,
                      lambda: mixer(stable=True)[0], lambda: fast)
        kv = _kv_proj(mem2d, row(mem_norm_w[l]), wk[l], wv[l])
        kv = kv.reshape(2, batch, mem_len, d)
        xs, (wg_b, wu_b, wd_b) = _attn(
            xs, row(norm2_w[l]), wq_b, kv, wo_b,
            (w_gate[l], w_up[l], w_down[l]), seq_len=seq_len, mem_len=mem_len)
        xs = _ffn(xs, row(norm3_w[l]), wg_b, wu_b, ffn_conv_w[l], row(ffn_conv_b[l]), wd_b,
                  row(final_norm_w), seq_len=seq_len, final_norm=(l == depth - 1))
    return xs.reshape(batch, seq_len, d)
```

```python
import functools

import jax
import jax.numpy as jnp
from jax import lax
from jax.experimental import pallas as pl
from jax.experimental.pallas import tpu as pltpu

F32 = jnp.float32
BF16 = jnp.bfloat16

EPS = 1e-6
HGRN_HEADS = 8
HGRN_CHUNK = 64
HGRN_SAFE_LOG_DECAY = 75.0
HGRN_BLOCK = 128
CONV_TAPS = 3
MEM_HEADS = 4
N_PROJ_GROUPS = 7
LANES = 128
SUBLANES = 8
BF16_ROWS = 16
KV_VMEM_LIMIT_BYTES = 56 * 1024 * 1024
MIXER_VMEM_LIMIT_BYTES = 60 * 1024 * 1024
FFN_VMEM_LIMIT_BYTES = 60 * 1024 * 1024
ATTN_VMEM_LIMIT_BYTES = 62 * 1024 * 1024

KV_COL_TILE = 512
MIXER_ROW_TILE = 256
ATTN_ROW_TILE = 512
FFN_ROW_TILE = 1024
FFN_TILE = 512
FFN_PREP_ROWS = 128
FFN_FINAL_PARTS = 2


def _rms(x, w):
    ms = jnp.mean(x * x, axis=-1, keepdims=True)
    return x * lax.rsqrt(ms + EPS) * w


def _silu(x):
    return x * jax.nn.sigmoid(x)


def _dot(a, b):
    return jnp.dot(a, b, preferred_element_type=F32)


def _dot_nt(a, b):
    return lax.dot_general(a, b, (((1,), (1,)), ((), ())), preferred_element_type=F32)


def _dot_tn(a, b):
    return lax.dot_general(a, b, (((0,), (0,)), ((), ())), preferred_element_type=F32)


def _shift_rows(cur, prev8, shift):
    rolled = pltpu.roll(cur, shift, axis=0)
    head = pltpu.roll(prev8, shift, axis=0)
    rid = lax.broadcasted_iota(jnp.int32, prev8.shape, 0)
    first = jnp.where(rid < shift, head, rolled[:SUBLANES])
    return jnp.concatenate([first, rolled[SUBLANES:]], axis=0)


def _causal_conv3(cur, prev8, w):
    return (w[0:1] * _shift_rows(cur, prev8, 2)
            + w[1:2] * _shift_rows(cur, prev8, 1)
            + w[2:3] * cur)


def _normalised_tile(x_ref, nw_ref, h_scr):
    h_scr[...] = _rms(x_ref[...], nw_ref[...]).astype(BF16)
    return h_scr[...]


def _cast_specs(to_cast, n_steps):
    def spec(w):
        rows = w.shape[0]
        rb = pl.cdiv(pl.cdiv(rows, n_steps), BF16_ROWS) * BF16_ROWS
        while rows % rb:
            rb += BF16_ROWS
        last = rows // rb - 1
        return pl.BlockSpec((rb, w.shape[1]), lambda i: (jnp.minimum(i, last), 0))

    return [spec(w) for w in to_cast]


def _cast_blocks(src_refs, dst_refs):
    for src, dst in zip(src_refs, dst_refs):
        dst[...] = src[...].astype(dst.dtype)


def _resident(shape):
    return pl.BlockSpec(shape, lambda i: (0,) * len(shape), pipeline_mode=pl.Buffered(1))


def _kv_kernel(mem_ref, nw_ref, wk_ref, wv_ref, kt_ref, v_ref, h_scr):
    @pl.when(pl.program_id(0) == 0)
    def _():
        h_scr[...] = _rms(mem_ref[...], nw_ref[...]).astype(BF16)

    h = h_scr[...]
    kt_ref[...] = _dot(h, wk_ref[...].astype(BF16)).T.astype(kt_ref.dtype)
    v_ref[...] = _dot(h, wv_ref[...].astype(BF16)).astype(v_ref.dtype)


def _kv_proj(mem2d, norm_w, wk, wv):
    rows, d = mem2d.shape
    tn = KV_COL_TILE
    return pl.pallas_call(
        _kv_kernel,
        grid=(d // tn,),
        in_specs=[
            pl.BlockSpec((rows, d), lambda n: (0, 0)),
            pl.BlockSpec((1, d), lambda n: (0, 0)),
            pl.BlockSpec((d, tn), lambda n: (0, n)),
            pl.BlockSpec((d, tn), lambda n: (0, n)),
        ],
        out_specs=[pl.BlockSpec((tn, rows), lambda n: (n, 0)),
                   pl.BlockSpec((rows, tn), lambda n: (0, n))],
        out_shape=[jax.ShapeDtypeStruct((d, rows), BF16),
                   jax.ShapeDtypeStruct((rows, d), BF16)],
        scratch_shapes=[pltpu.VMEM((rows, d), BF16)],
        compiler_params=pltpu.CompilerParams(
            dimension_semantics=("arbitrary",),
            vmem_limit_bytes=KV_VMEM_LIMIT_BYTES),
        name="kv_proj",
    )(mem2d, norm_w, wk, wv)


def _chunk_cumsum(x):
    pos = lax.broadcasted_iota(jnp.int32, x.shape, 0) % HGRN_CHUNK
    s = 1
    while s < HGRN_CHUNK:
        x = x + jnp.where(pos >= s, pltpu.roll(x, s, axis=0), 0.0)
        s *= 2
    return x


def _block_diag2(a, b):
    z = jnp.zeros(a.shape, a.dtype)
    return jnp.concatenate([jnp.concatenate([a, z], axis=1),
                            jnp.concatenate([z, b], axis=1)], axis=0)


def _intra_chunk_by_offset(qf, kk, b, v):
    pos = lax.broadcasted_iota(jnp.int32, qf.shape, 0) % HGRN_CHUNK
    ones = jnp.ones((LANES, LANES), BF16)
    head_sum = _block_diag2(ones, ones)

    def one_offset(d, acc):
        in_chunk = pos >= d
        decay = jnp.exp(jnp.where(in_chunk, b - pltpu.roll(b, d, axis=0), 0.0))
        w = jnp.where(in_chunk, qf * pltpu.roll(kk, d, axis=0) * decay, 0.0)
        return acc + _dot(w.astype(BF16), head_sum) * pltpu.roll(v, d, axis=0)

    return lax.fori_loop(0, HGRN_CHUNK, one_offset, jnp.zeros(qf.shape, F32))


def _mixer_kernel(x_ref, nw_ref, win_ref, wout_ref, lb_ref, hnw_ref, scw_ref, *rest,
                  tiles_per_seq, layer, stable, n_cast):
    cast_in, (o_ref, bmin_ref) = rest[:n_cast], rest[n_cast:n_cast + 2]
    cast_out = rest[n_cast + 2:2 * n_cast + 2]
    h_scr, state_scr, carry_scr = rest[2 * n_cast + 2:]
    _cast_blocks(cast_in, cast_out)
    i = pl.program_id(0)
    tm = x_ref.shape[0]
    hd = LANES
    gw = 2 * hd
    width = lb_ref.shape[1]
    n_pairs = width // gw
    n_chunks = tm // HGRN_CHUNK
    chunks = [slice(c * HGRN_CHUNK, (c + 1) * HGRN_CHUNK) for c in range(n_chunks)]
    blocks = [slice(r0, r0 + HGRN_BLOCK) for r0 in range(0, tm, HGRN_BLOCK)]

    @pl.when(i % tiles_per_seq == 0)
    def _():
        state_scr[...] = jnp.zeros(state_scr.shape, F32)
        carry_scr[...] = jnp.zeros(carry_scr.shape, F32)

    lb_all = jnp.sum(jax.nn.softmax(lb_ref[...], axis=0)[0:layer + 1], axis=0, keepdims=True)
    hnw = hnw_ref[...]
    scw = scw_ref[...]
    if not stable:
        t_id = lax.broadcasted_iota(jnp.int32, (HGRN_BLOCK, 2 * HGRN_BLOCK), 0)
        s_id = lax.broadcasted_iota(jnp.int32, (HGRN_BLOCK, 2 * HGRN_BLOCK), 1) % HGRN_BLOCK
        keep = ((t_id // HGRN_CHUNK) == (s_id // HGRN_CHUNK)) & (s_id <= t_id)

    h = _normalised_tile(x_ref, nw_ref, h_scr)
    gated, convs, b_mins = [], [], []
    for p in range(n_pairs):
        lanes = slice(p * gw, (p + 1) * gw)

        def proj(group):
            return _dot(h, win_ref[:, group * width + p * gw:group * width + (p + 1) * gw])

        f_pre = proj(1)
        q = proj(0)
        lb = lb_all[:, lanes]
        f = lb + (1.0 - lb) * jax.nn.sigmoid(f_pre)
        logf = jnp.log(f)
        kk = 1.0 - f
        qf = _silu(q)
        b = _chunk_cumsum(logf)
        v = proj(2)
        vb = v.astype(BF16)
        g = proj(3)
        cb = proj(4)
        cc = proj(5)
        qt = (qf * jnp.exp(b)).astype(BF16)

        if not stable:
            kt = (kk * jnp.exp(-b)).astype(BF16)
            scores = [_dot_nt(qt[rows], _block_diag2(kt[rows, :hd], kt[rows, hd:]))
                      for rows in blocks]

        b_lasts = [b[rows][HGRN_CHUNK - 1:HGRN_CHUNK] for rows in chunks]
        upds = [_dot_tn(vb[rows], (kk[rows] * jnp.exp(bl - b[rows])).astype(BF16))
                for rows, bl in zip(chunks, b_lasts)]
        ch = proj(6)

        if stable:
            intra = _intra_chunk_by_offset(qf, kk, b, v)
        else:
            intra = jnp.concatenate(
                [_dot(jnp.where(keep, sc, 0.0).astype(BF16),
                      _block_diag2(vb[rows, :hd], vb[rows, hd:]))
                 for rows, sc in zip(blocks, scores)], axis=0)

        st0 = state_scr[2 * p]
        st1 = state_scr[2 * p + 1]
        starts = []
        for upd, bl in zip(upds, b_lasts):
            starts.append(_block_diag2(st0.astype(BF16), st1.astype(BF16)))
            decay = jnp.exp(bl)
            st0 = st0 * decay[:, :hd] + upd[:hd, :hd]
            st1 = st1 * decay[:, hd:] + upd[hd:, hd:]
        state_scr[2 * p] = st0
        state_scr[2 * p + 1] = st1
        inter = [_dot_nt(qt[rows], start) for rows, start in zip(chunks, starts)]

        u = cc * ch
        prev8 = carry_scr[p]
        carry_scr[p] = u[tm - SUBLANES:]
        convs.append((cb * _causal_conv3(u, prev8, scw[:, lanes])).astype(BF16))

        o = intra + jnp.concatenate(inter, axis=0)
        o = jnp.concatenate([_rms(o[:, :hd], hnw), _rms(o[:, hd:], hnw)], axis=1)
        gated.append((o * _silu(g)).astype(BF16))
        b_mins.append(functools.reduce(jnp.minimum, b_lasts))

    mixed = jnp.concatenate(gated + convs, axis=1)
    o_ref[...] = x_ref[...] + _dot(mixed, wout_ref[...])

    b_min = jnp.concatenate(b_mins, axis=1)

    @pl.when(i == 0)
    def _():
        bmin_ref[...] = b_min

    @pl.when(i > 0)
    def _():
        bmin_ref[...] = jnp.minimum(bmin_ref[...], b_min)


def _mixer(x2d, norm_w, w_in, w_out, hgrn_lb, hgrn_norm_w, sconv_w, to_cast=(), *,
           seq_len, layer, stable):
    m, d = x2d.shape
    tm = MIXER_ROW_TILE
    width = hgrn_lb.shape[1]
    n_tiles = m // tm
    cast_specs = _cast_specs(to_cast, n_tiles)
    kern = functools.partial(_mixer_kernel, tiles_per_seq=seq_len // tm, layer=layer,
                             stable=stable, n_cast=len(to_cast))
    out, b_min, *casted = pl.pallas_call(
        kern,
        grid=(n_tiles,),
        in_specs=[
            pl.BlockSpec((tm, d), lambda i: (i, 0)),
            _resident((1, d)),
            _resident(w_in.shape),
            _resident(w_out.shape),
            _resident(hgrn_lb.shape),
            _resident((1, LANES)),
            _resident(sconv_w.shape),
            *cast_specs,
        ],
        out_specs=[pl.BlockSpec((tm, d), lambda i: (i, 0)),
                   pl.BlockSpec((1, width), lambda i: (0, 0)),
                   *cast_specs],
        out_shape=[jax.ShapeDtypeStruct((m, d), F32),
                   jax.ShapeDtypeStruct((1, width), F32),
                   *(jax.ShapeDtypeStruct(w.shape, BF16) for w in to_cast)],
        scratch_shapes=[
            pltpu.VMEM((tm, d), BF16),
            pltpu.VMEM((HGRN_HEADS, LANES, LANES), F32),
            pltpu.VMEM((width // (2 * LANES), SUBLANES, 2 * LANES), F32),
        ],
        compiler_params=pltpu.CompilerParams(
            dimension_semantics=("arbitrary",),
            vmem_limit_bytes=MIXER_VMEM_LIMIT_BYTES),
        name="hybrid_mixer_stable" if stable else "hybrid_mixer",
    )(x2d, norm_w, w_in, w_out, hgrn_lb, hgrn_norm_w, sconv_w, *to_cast)
    return out, b_min, casted


def _attn_kernel(x_ref, nw_ref, wq_ref, k_ref, v_ref, wo_ref, *rest, head_dim, n_cast):
    cast_in, o_ref = rest[:n_cast], rest[n_cast]
    cast_out, h_scr = rest[n_cast + 1:2 * n_cast + 1], rest[2 * n_cast + 1]
    h = _normalised_tile(x_ref, nw_ref, h_scr)
    scale = head_dim ** -0.5
    cols = [slice(hh * head_dim, (hh + 1) * head_dim) for hh in range(MEM_HEADS)]
    qs = [_dot(h, wq_ref[:, c]).astype(BF16) for c in cols]
    _cast_blocks(cast_in, cast_out)
    ss = [_dot(q, k_ref[c, :]) * scale for q, c in zip(qs, cols)]
    out = x_ref[...]
    for s, c in zip(ss, cols):
        e = jnp.exp(s - jnp.max(s, axis=-1, keepdims=True))
        pr = (e / jnp.sum(e, axis=-1, keepdims=True)).astype(BF16)
        out = out + _dot(_dot(pr, v_ref[:, c]).astype(BF16), wo_ref[c, :])
    o_ref[...] = out


def _attn(x2d, norm_w, wq, k_t, v, wo, to_cast, *, seq_len, mem_len):
    m, d = x2d.shape
    tm = ATTN_ROW_TILE
    tiles_per_seq = seq_len // tm
    n_tiles = m // tm

    cast_specs = _cast_specs(to_cast, n_tiles)
    kern = functools.partial(_attn_kernel, head_dim=d // MEM_HEADS, n_cast=len(to_cast))
    out, *casted = pl.pallas_call(
        kern,
        grid=(n_tiles,),
        in_specs=[
            pl.BlockSpec((tm, d), lambda i: (i, 0)),
            _resident((1, d)),
            _resident((d, d)),
            pl.BlockSpec((d, mem_len), lambda i: (0, i // tiles_per_seq)),
            pl.BlockSpec((mem_len, d), lambda i: (i // tiles_per_seq, 0)),
            _resident((d, d)),
            *cast_specs,
        ],
        out_specs=[pl.BlockSpec((tm, d), lambda i: (i, 0)), *cast_specs],
        out_shape=[jax.ShapeDtypeStruct((m, d), F32),
                   *(jax.ShapeDtypeStruct(w.shape, BF16) for w in to_cast)],
        scratch_shapes=[pltpu.VMEM((tm, d), BF16)],
        compiler_params=pltpu.CompilerParams(
            dimension_semantics=("arbitrary",),
            vmem_limit_bytes=ATTN_VMEM_LIMIT_BYTES),
        name="mem_cross_attn",
    )(x2d, norm_w, wq, k_t, v, wo, *to_cast)
    return out, casted


def _ffn_kernel(x_ref, nw_ref, wg_hbm, wu_hbm, cw_ref, cb_ref, wd_hbm, fnw_ref, o_ref,
                h_scr, wg_buf, wu_buf, wd_buf, sems, carry_scr, *,
                tiles_per_seq, n_tiles, final_norm):
    i = pl.program_id(0)
    tm = x_ref.shape[0]
    nf, _, tf = cw_ref.shape

    def weight_copies(f, slot):
        cols = pl.ds(f * tf if isinstance(f, int) else pl.multiple_of(f * tf, tf), tf)
        return (pltpu.make_async_copy(wg_hbm.at[:, cols], wg_buf.at[slot], sems.at[0, slot]),
                pltpu.make_async_copy(wu_hbm.at[:, cols], wu_buf.at[slot], sems.at[1, slot]),
                pltpu.make_async_copy(wd_hbm.at[cols, :], wd_buf.at[slot], sems.at[2, slot]))

    @pl.when(i == 0)
    def _():
        for copy in weight_copies(0, 0):
            copy.start()

    @pl.when(i % tiles_per_seq == 0)
    def _():
        carry_scr[...] = jnp.zeros(carry_scr.shape, F32)

    def compute_tile(f, slot, row_parts=1, finish_rows=None):
        h = h_scr[...]
        a = _dot(h, wg_buf[slot])
        up = _dot(h, wu_buf[slot])
        prev8 = carry_scr[f]
        carry_scr[f] = a[tm - SUBLANES:]
        act = (_silu(_causal_conv3(a, prev8, cw_ref[f]) + cb_ref[f]) * up).astype(BF16)
        part = tm // row_parts
        for r0 in range(0, tm, part):
            o_ref[r0:r0 + part, :] += _dot(act[r0:r0 + part], wd_buf[slot])
            if finish_rows is not None:
                finish_rows(r0, part)

    slot0 = (i * nf) % 2
    for copy in weight_copies(1, 1 - slot0):
        copy.start()
    for copy in weight_copies(0, slot0):
        copy.wait()
    for r0 in range(0, tm, FFN_PREP_ROWS):
        xv = x_ref[r0:r0 + FFN_PREP_ROWS, :]
        h_scr[r0:r0 + FFN_PREP_ROWS, :] = _rms(xv, nw_ref[...]).astype(BF16)
        o_ref[r0:r0 + FFN_PREP_ROWS, :] = xv
    compute_tile(0, slot0)

    def column_tile(f, carry):
        slot = (i * nf + f) % 2
        for copy in weight_copies(f + 1, 1 - slot):
            copy.start()
        for copy in weight_copies(f, slot):
            copy.wait()
        compute_tile(f, slot)
        return carry

    lax.fori_loop(1, nf - 1, column_tile, 0)

    last = nf - 1
    slot_last = (i * nf + last) % 2

    @pl.when(i < n_tiles - 1)
    def _():
        for copy in weight_copies(0, 1 - slot_last):
            copy.start()

    for copy in weight_copies(last, slot_last):
        copy.wait()

    def final_rms(r0, rows):
        for b0 in range(r0, r0 + rows, FFN_PREP_ROWS):
            blk = slice(b0, b0 + FFN_PREP_ROWS)
            o_ref[blk, :] = _rms(o_ref[blk, :], fnw_ref[...])

    if final_norm:
        compute_tile(last, slot_last, FFN_FINAL_PARTS, final_rms)
    else:
        compute_tile(last, slot_last)


def _ffn(x2d, norm_w, w_gate, w_up, conv_w, conv_b, w_down, final_w, *, seq_len, final_norm):
    m, d = x2d.shape
    d_ff = w_gate.shape[1]
    tm, tf = FFN_ROW_TILE, FFN_TILE
    nf = d_ff // tf
    assert nf >= 3, "first and last column tiles are peeled around a loop over the rest"
    n_tiles = m // tm
    conv_w = conv_w.reshape(CONV_TAPS, nf, tf).transpose(1, 0, 2)
    conv_b = conv_b.reshape(nf, 1, tf)
    kern = functools.partial(_ffn_kernel, tiles_per_seq=seq_len // tm, n_tiles=n_tiles,
                             final_norm=final_norm)
    hbm = pl.BlockSpec(memory_space=pl.ANY)
    return pl.pallas_call(
        kern,
        grid=(n_tiles,),
        in_specs=[
            pl.BlockSpec((tm, d), lambda i: (i, 0)),
            _resident((1, d)),
            hbm, hbm,
            _resident(conv_w.shape),
            _resident(conv_b.shape),
            hbm,
            _resident((1, d)),
        ],
        out_specs=pl.BlockSpec((tm, d), lambda i: (i, 0)),
        out_shape=jax.ShapeDtypeStruct((m, d), F32),
        scratch_shapes=[
            pltpu.VMEM((tm, d), BF16),
            pltpu.VMEM((2, d, tf), BF16),
            pltpu.VMEM((2, d, tf), BF16),
            pltpu.VMEM((2, tf, d), BF16),
            pltpu.SemaphoreType.DMA((3, 2)),
            pltpu.VMEM((nf, SUBLANES, tf), F32),
        ],
        compiler_params=pltpu.CompilerParams(
            dimension_semantics=("arbitrary",),
            vmem_limit_bytes=FFN_VMEM_LIMIT_BYTES),
        name="conv_ffn",
    )(x2d, norm_w, w_gate, w_up, conv_w, conv_b, w_down, final_w)


def kernel(x, mem, hgrn_lb, norm1_w, w_in, hgrn_norm_w, sconv_w, w_out, norm2_w, mem_norm_w,
           wq, wk, wv, wo, norm3_w, w_gate, w_up, ffn_conv_w, ffn_conv_b, w_down, final_norm_w):
    batch, seq_len, d = x.shape
    mem_len = mem.shape[1]
    depth = norm1_w.shape[0]
    assert all(seq_len % t == 0 for t in (MIXER_ROW_TILE, ATTN_ROW_TILE, FFN_ROW_TILE))
    assert MIXER_ROW_TILE % HGRN_BLOCK == 0 and HGRN_BLOCK % HGRN_CHUNK == 0
    assert hgrn_lb.shape[1] == HGRN_HEADS * LANES and HGRN_HEADS % 2 == 0
    assert w_in.shape[2] == N_PROJ_GROUPS * HGRN_HEADS * LANES and w_gate.shape[2] % FFN_TILE == 0

    xs = x.reshape(batch * seq_len, d)
    mem2d = mem.reshape(batch * mem_len, d)
    row = lambda w: w.reshape(1, -1)
    for l in range(depth):
        mixer = functools.partial(
            _mixer, xs, row(norm1_w[l]), w_in[l].astype(BF16), w_out[l].astype(BF16), hgrn_lb,
            row(hgrn_norm_w[l]), sconv_w[l], seq_len=seq_len, layer=l)
        fast, b_min, (wq_b, wo_b) = mixer((wq[l], wo[l]), stable=False)
        xs = lax.cond(jnp.min(b_min) < -HGRN_SAFE_LOG_DECAY,
                      lambda: mixer(stable=True)[0], lambda: fast)
        k_t, v = _kv_proj(mem2d, row(mem_norm_w[l]), wk[l], wv[l])
        xs, (wg_b, wu_b, wd_b) = _attn(
            xs, row(norm2_w[l]), wq_b, k_t, v, wo_b,
            (w_gate[l], w_up[l], w_down[l]), seq_len=seq_len, mem_len=mem_len)
        xs = _ffn(xs, row(norm3_w[l]), wg_b, wu_b, ffn_conv_w[l], row(ffn_conv_b[l]), wd_b,
                  row(final_norm_w), seq_len=seq_len, final_norm=(l == depth - 1))
    return xs.reshape(batch, seq_len, d)
```
